```python
import math
import jax, jax.numpy as jnp
from jax import lax
import numpy as np

D_MODEL = 2048
BATCH = 1
SEQ = 8192
DEPTH = 1
DEC_BATCH = 128
DEC_SEQ = 4
PAST_LEN = 2048
PAGE_SIZE = 128

MIX_WIDTH = D_MODEL
ATT_WIDTH = MIX_WIDTH // 2
REC_WIDTH = MIX_WIDTH - ATT_WIDTH
ATT_HEAD_DIM = 64
ATT_V_DIM = 2 * ATT_HEAD_DIM
N_ATT_HEADS = ATT_WIDTH // ATT_V_DIM
ROT_DIM = ATT_HEAD_DIM // 4
ROPE_THETA = 500000.0
REC_EXPAND = 128
N_REC_HEADS = REC_WIDTH // REC_EXPAND
REC_DK = REC_EXPAND
REC_DV = REC_WIDTH // N_REC_HEADS
REC_CHUNK = 64
D_FF = 4 * D_MODEL
Q_BLOCK = 128
EPS = 1e-6
POOL_FACTOR = 1.25
PROJ_SIZES = (N_ATT_HEADS * 2 * ATT_HEAD_DIM, N_ATT_HEADS * 2 * ATT_HEAD_DIM, N_ATT_HEADS * ATT_V_DIM,
              N_REC_HEADS * REC_DK, N_REC_HEADS * REC_DK, N_REC_HEADS * REC_DV, N_REC_HEADS * REC_DV)

kernel_name = "hymba_diffattn_hgrn2_decode_step"


def rmsnorm(x, g):
    xf = x.astype(jnp.float32)
    y = xf * lax.rsqrt(jnp.mean(xf * xf, axis=-1, keepdims=True) + EPS)
    return (y * g.astype(jnp.float32)).astype(x.dtype)


def rope_partial(x, pos):
    half = ROT_DIM // 2
    inv = ROPE_THETA ** (-jnp.arange(half, dtype=jnp.float32) * 2.0 / ROT_DIM)
    ang = pos.astype(jnp.float32)[:, None] * inv[None, :]
    cos = jnp.cos(ang)[None, :, None, None, :]
    sin = jnp.sin(ang)[None, :, None, None, :]
    xf = x.astype(jnp.float32)
    x1 = xf[..., :half]
    x2 = xf[..., half:ROT_DIM]
    out = jnp.concatenate([x1 * cos - x2 * sin, x2 * cos + x1 * sin, xf[..., ROT_DIM:]], axis=-1)
    return out.astype(x.dtype)


def project(h, norm_g, w_in_l, q_g, k_g, pos):
    B, T = h.shape[:2]
    xn = rmsnorm(h, norm_g)
    proj = xn @ w_in_l
    splits = [int(s) for s in np.cumsum(PROJ_SIZES)[:-1]]
    q, k, v, rq, rf, ri, rg = jnp.split(proj, splits, axis=-1)
    q = rope_partial(rmsnorm(q.reshape(B, T, N_ATT_HEADS, 2, ATT_HEAD_DIM), q_g), pos)
    k = rope_partial(rmsnorm(k.reshape(B, T, N_ATT_HEADS, 2, ATT_HEAD_DIM), k_g), pos)
    v = v.reshape(B, T, N_ATT_HEADS, ATT_V_DIM)
    return q, k, v, rq, rf, ri, rg


def diff_attn(q, k, v, q_pos, k_pos, lam):
    scale = 1.0 / math.sqrt(ATT_HEAD_DIM)
    s = jnp.einsum('bqhcd,bkhcd->bhcqk', q, k).astype(jnp.float32) * scale
    mask = k_pos[None, :] <= q_pos[:, None]
    s = jnp.where(mask, s, -jnp.inf)
    p = jax.nn.softmax(s, axis=-1)
    a = p[:, :, 0] - lam * p[:, :, 1]
    return jnp.einsum('bhqk,bkhd->bqhd', a.astype(v.dtype), v)


def diff_attn_prompt(q, k, v, lam):
    B, T = q.shape[:2]
    nb = T // Q_BLOCK
    qb = q.reshape(B, nb, Q_BLOCK, N_ATT_HEADS, 2, ATT_HEAD_DIM).swapaxes(0, 1)
    pos = jnp.arange(T)
    qpos = pos.reshape(nb, Q_BLOCK)
    out = lax.map(lambda a: diff_attn(a[0], k, v, a[1], pos, lam), (qb, qpos))
    return out.swapaxes(0, 1).reshape(B, T, N_ATT_HEADS, ATT_V_DIM)


def hgrn2_chunked(q, logf, k, v, s0, chunk):
    B, T, H, DK = q.shape
    DV = v.shape[-1]
    n = T // chunk

    def to_chunks(a):
        return a.reshape(B, n, chunk, H, a.shape[-1]).transpose(1, 0, 3, 2, 4)

    qc, fc, kc, vc = to_chunks(q), to_chunks(logf), to_chunks(k), to_chunks(v)
    b = jnp.cumsum(fc, axis=3)
    b_last = b[:, :, :, -1:, :]
    q_in = qc * jnp.exp(b)
    k_in = kc * jnp.exp(-b)
    k_st = kc * jnp.exp(b_last - b)
    causal = jnp.tril(jnp.ones((chunk, chunk), dtype=bool))
    att = jnp.where(causal, jnp.einsum('nbhtk,nbhsk->nbhts', q_in, k_in), 0.0)
    o_intra = jnp.einsum('nbhts,nbhsv->nbhtv', att, vc)

    def step(S, inp):
        qi, ks, vv, bl = inp
        o = jnp.einsum('bhtk,bhkv->bhtv', qi, S)
        S = jnp.exp(bl[:, :, 0, :])[..., None] * S + jnp.einsum('bhsk,bhsv->bhkv', ks, vv)
        return S, o

    s_fin, o_inter = lax.scan(step, s0, (q_in, k_st, vc, b_last))
    o = (o_intra + o_inter).transpose(1, 0, 3, 2, 4).reshape(B, T, H, DV)
    return o, s_fin


def recurrent(rq, rf, ri, rg, lb, out_g, s0):
    B, T = rq.shape[:2]
    f32 = jnp.float32
    q = jax.nn.silu(rq.astype(f32)).reshape(B, T, N_REC_HEADS, REC_DK)
    fg = lb + (1.0 - lb) * jax.nn.sigmoid(rf.astype(f32))
    logf = jnp.log(fg).reshape(B, T, N_REC_HEADS, REC_DK)
    k = (1.0 - fg).reshape(B, T, N_REC_HEADS, REC_DK)
    v = ri.astype(f32).reshape(B, T, N_REC_HEADS, REC_DV)
    chunk = math.gcd(T, REC_CHUNK)
    o, s = hgrn2_chunked(q, logf, k, v, s0.astype(f32), chunk)
    o = rmsnorm(o, out_g) * jax.nn.silu(rg.astype(f32)).reshape(B, T, N_REC_HEADS, REC_DV)
    return o.reshape(B, T, REC_WIDTH).astype(rq.dtype), s


def finish(h, att, rec, beta_a, beta_r, w_out_l, ffn_g, w_up_l, w_down_l):
    mix = jnp.concatenate([att * beta_a, rec * beta_r], axis=-1) @ w_out_l
    h = h + mix
    u = jax.nn.relu(rmsnorm(h, ffn_g) @ w_up_l)
    return h + (u * u) @ w_down_l


def setup_inputs(seed: int = 0) -> dict:
    key = jax.random.key(seed)
    ks = jax.random.split(key, 24)
    f32 = jnp.float32
    n_pages = PAST_LEN // PAGE_SIZE
    n_pool = int(math.ceil(POOL_FACTOR * DEC_BATCH * n_pages))
    nrm = lambda k, shape, s: (jax.random.normal(k, shape, f32) * s)
    gain = lambda k, shape: 1.0 + 0.02 * jax.random.normal(k, shape, f32)
    proj_w = sum(PROJ_SIZES)
    perm = jax.random.permutation(ks[5], n_pool)[: DEC_BATCH * n_pages]
    return {
        "x_prompt": nrm(ks[0], (BATCH, SEQ, D_MODEL), 1.0),
        "x_sample": nrm(ks[1], (DEC_BATCH, DEC_SEQ, D_MODEL), 1.0),
        "cache_k": nrm(ks[2], (DEPTH, n_pool, PAGE_SIZE, N_ATT_HEADS, 2, ATT_HEAD_DIM), 1.0),
        "cache_v": nrm(ks[3], (DEPTH, n_pool, PAGE_SIZE, N_ATT_HEADS, ATT_V_DIM), 1.0),
        "state_rec": nrm(ks[4], (DEPTH, DEC_BATCH, N_REC_HEADS, REC_DK, REC_DV), 0.3),
        "page_table": perm.reshape(DEC_BATCH, n_pages).astype(jnp.int32),
        "attn_norm": gain(ks[6], (DEPTH, D_MODEL)),
        "w_in": nrm(ks[7], (DEPTH, D_MODEL, proj_w), D_MODEL ** -0.5),
        "q_norm": gain(ks[8], (DEPTH, ATT_HEAD_DIM)),
        "k_norm": gain(ks[9], (DEPTH, ATT_HEAD_DIM)),
        "lambda_q1": nrm(ks[10], (DEPTH, ATT_HEAD_DIM), 0.1),
        "lambda_k1": nrm(ks[11], (DEPTH, ATT_HEAD_DIM), 0.1),
        "lambda_q2": nrm(ks[12], (DEPTH, ATT_HEAD_DIM), 0.1),
        "lambda_k2": nrm(ks[13], (DEPTH, ATT_HEAD_DIM), 0.1),
        "att_out_norm": gain(ks[14], (DEPTH, ATT_V_DIM)),
        "rec_lb_logits": nrm(ks[15], (DEPTH + 1, N_REC_HEADS * REC_DK), 0.1),
        "rec_out_norm": gain(ks[16], (DEPTH, REC_DV)),
        "beta_att": gain(ks[17], (DEPTH, ATT_WIDTH)),
        "beta_rec": gain(ks[18], (DEPTH, REC_WIDTH)),
        "w_out": nrm(ks[19], (DEPTH, MIX_WIDTH, D_MODEL), MIX_WIDTH ** -0.5),
        "ffn_norm": gain(ks[20], (DEPTH, D_MODEL)),
        "w_up": nrm(ks[21], (DEPTH, D_MODEL, D_FF), D_MODEL ** -0.5),
        "w_down": nrm(ks[22], (DEPTH, D_FF, D_MODEL), 0.5 * D_FF ** -0.5),
    }


def reference(x_prompt, x_sample, cache_k, cache_v, state_rec, page_table, attn_norm, w_in, q_norm, k_norm,
              lambda_q1, lambda_k1, lambda_q2, lambda_k2, att_out_norm, rec_lb_logits, rec_out_norm,
              beta_att, beta_rec, w_out, ffn_norm, w_up, w_down):
    f32 = jnp.float32
    B_p, T_p = x_prompt.shape[:2]
    B_s, T_s = x_sample.shape[:2]
    n_pages = page_table.shape[1]
    past = n_pages * cache_k.shape[2]
    pos_p = jnp.arange(T_p)
    pos_s = past + jnp.arange(T_s)
    k_pos_s = jnp.concatenate([jnp.arange(past), pos_s])
    lb_all = jnp.cumsum(jax.nn.softmax(rec_lb_logits.astype(f32), axis=0), axis=0)

    hp, hs = x_prompt, x_sample
    kp_l, vp_l, sp_l, ks_l, vs_l, ss_l = [], [], [], [], [], []
    for l in range(DEPTH):
        lam_init = 0.8 - 0.6 * math.exp(-0.3 * l)
        lam = (jnp.exp(jnp.sum(lambda_q1[l].astype(f32) * lambda_k1[l].astype(f32)))
               - jnp.exp(jnp.sum(lambda_q2[l].astype(f32) * lambda_k2[l].astype(f32))) + lam_init)
        lb = lb_all[l]

        q, k, v, rq, rf, ri, rg = project(hp, attn_norm[l], w_in[l], q_norm[l], k_norm[l], pos_p)
        att = diff_attn_prompt(q, k, v, lam)
        att = (rmsnorm(att, att_out_norm[l]) * (1.0 - lam_init)).reshape(B_p, T_p, ATT_WIDTH)
        s0 = jnp.zeros((B_p, N_REC_HEADS, REC_DK, REC_DV), f32)
        rec, s_new = recurrent(rq, rf, ri, rg, lb, rec_out_norm[l], s0)
        hp = finish(hp, att, rec, beta_att[l], beta_rec[l], w_out[l], ffn_norm[l], w_up[l], w_down[l])
        kp_l.append(k)
        vp_l.append(v)
        sp_l.append(s_new.astype(state_rec.dtype))

        q, k, v, rq, rf, ri, rg = project(hs, attn_norm[l], w_in[l], q_norm[l], k_norm[l], pos_s)
        k_past = cache_k[l, page_table].reshape(B_s, past, N_ATT_HEADS, 2, ATT_HEAD_DIM)
        v_past = cache_v[l, page_table].reshape(B_s, past, N_ATT_HEADS, ATT_V_DIM)
        k_all = jnp.concatenate([k_past, k.astype(k_past.dtype)], axis=1)
        v_all = jnp.concatenate([v_past, v.astype(v_past.dtype)], axis=1)
        att = diff_attn(q, k_all, v_all, pos_s, k_pos_s, lam)
        att = (rmsnorm(att, att_out_norm[l]) * (1.0 - lam_init)).reshape(B_s, T_s, ATT_WIDTH).astype(hs.dtype)
        rec, s_new = recurrent(rq, rf, ri, rg, lb, rec_out_norm[l], state_rec[l])
        hs = finish(hs, att, rec, beta_att[l], beta_rec[l], w_out[l], ffn_norm[l], w_up[l], w_down[l])
        ks_l.append(k)
        vs_l.append(v)
        ss_l.append(s_new.astype(state_rec.dtype))

    k_prompt = jnp.stack(kp_l)
    v_prompt = jnp.stack(vp_l)
    s_prompt = jnp.stack(sp_l)
    k_sample = jnp.stack(ks_l)
    v_sample = jnp.stack(vs_l)
    s_sample = jnp.stack(ss_l)
    return (hp, hs, k_prompt, v_prompt, s_prompt, k_sample, v_sample, s_sample)
```

```python
import functools
import math

import jax
import jax.numpy as jnp
from jax import lax
from jax.experimental import pallas as pl
from jax.experimental.pallas import tpu as pltpu

F32 = jnp.float32
BF16 = jnp.bfloat16

EPS = 1e-6
ROPE_THETA = 500000.0
MASK_VALUE = -1e30
LANES = 128
VMEM_LIMIT_BYTES = 56 * 1024 * 1024
REC_CHUNK = 64

_NT = (((1,), (1,)), ((), ()))
_TA = (((0,), (0,)), ((), ()))


def _params(*semantics):
    return pltpu.CompilerParams(dimension_semantics=semantics, vmem_limit_bytes=VMEM_LIMIT_BYTES)


def _pick(n, pref):
    if n <= pref:
        return n
    t = pref
    while n % t:
        t //= 2
    return t


def _split3(x):
    h1 = x.astype(BF16)
    r1 = x - h1.astype(F32)
    h2 = r1.astype(BF16)
    h3 = (r1 - h2.astype(F32)).astype(BF16)
    return h1, h2, h3


def _prep_body(lq1_ref, lk1_ref, lq2_ref, lk2_ref, logit_ref, lam_ref, lb_ref, *, lam_inits):
    for l, lam_init in enumerate(lam_inits):
        s1 = jnp.sum(lq1_ref[l:l + 1, :] * lk1_ref[l:l + 1, :], axis=1, keepdims=True)
        s2 = jnp.sum(lq2_ref[l:l + 1, :] * lk2_ref[l:l + 1, :], axis=1, keepdims=True)
        lam = jnp.exp(s1) - jnp.exp(s2) + lam_init
        lam_ref[l:l + 1, :] = jnp.broadcast_to(lam, (1, LANES))
    x = logit_ref[...]
    e = jnp.exp(x - jnp.max(x, axis=0, keepdims=True))
    sm = e / jnp.sum(e, axis=0, keepdims=True)
    acc = jnp.zeros((1, x.shape[1]), F32)
    for r in range(x.shape[0]):
        acc = acc + sm[r:r + 1, :]
        lb_ref[r:r + 1, :] = acc


def _prep(lq1, lk1, lq2, lk2, logits, lam_inits):
    depth = lq1.shape[0]
    return pl.pallas_call(
        functools.partial(_prep_body, lam_inits=lam_inits),
        out_shape=(jax.ShapeDtypeStruct((depth, LANES), F32),
                   jax.ShapeDtypeStruct(logits.shape, F32)),
        name="prep",
    )(lq1, lk1, lq2, lk2, logits)


def _proj_body(x_ref, g_ref, w_ref, qkg_ref, gmat_ref, cos_ref, sa_ref, sb_ref, o_ref, xn_ref,
               *, head_dim, rot_half):
    j = pl.program_id(1)

    @pl.when(j == 0)
    def _():
        x = x_ref[...]
        ms = jnp.mean(x * x, axis=-1, keepdims=True)
        xn_ref[...] = (x * lax.rsqrt(ms + EPS) * g_ref[...]).astype(BF16)

    y = jnp.dot(xn_ref[...], w_ref[...], preferred_element_type=F32)

    @pl.when(j < 2)
    def _():
        y2 = y * y
        hi = y2.astype(BF16)
        lo = (y2 - hi.astype(F32)).astype(BF16)
        gm = gmat_ref[...]
        ss = (jnp.dot(hi, gm, preferred_element_type=F32)
              + jnp.dot(lo, gm, preferred_element_type=F32))
        yn = y * lax.rsqrt(ss * (1.0 / head_dim) + EPS) * qkg_ref[0]
        c, sa, sb = cos_ref[...], sa_ref[...], sb_ref[...]
        for t in range(yn.shape[1] // LANES):
            yt = yn[:, t * LANES:(t + 1) * LANES]
            o_ref[:, t * LANES:(t + 1) * LANES] = (
                yt * c + pltpu.roll(yt, LANES - rot_half, 1) * sa + pltpu.roll(yt, rot_half, 1) * sb)

    @pl.when(j >= 2)
    def _():
        o_ref[...] = y


def _proj(x, norm_g, w_bf, qk_gain, gmat, cos_t, sa_t, sb_t, *, head_dim):
    m, d = x.shape
    gw = gmat.shape[0]
    ngroups = w_bf.shape[1] // gw
    tm = _pick(m, 512)
    return pl.pallas_call(
        functools.partial(_proj_body, head_dim=head_dim, rot_half=head_dim // 8),
        grid=(m // tm, ngroups),
        in_specs=[
            pl.BlockSpec((tm, d), lambda i, j: (i, 0)),
            pl.BlockSpec((1, d), lambda i, j: (0, 0)),
            pl.BlockSpec((d, gw), lambda i, j: (0, j)),
            pl.BlockSpec((1, 1, gw), lambda i, j: (jnp.minimum(j, 1), 0, 0)),
            pl.BlockSpec((gw, gw), lambda i, j: (0, 0)),
            pl.BlockSpec((tm, LANES), lambda i, j: (i, 0)),
            pl.BlockSpec((tm, LANES), lambda i, j: (i, 0)),
            pl.BlockSpec((tm, LANES), lambda i, j: (i, 0)),
        ],
        out_specs=pl.BlockSpec((tm, gw), lambda i, j: (i, j)),
        out_shape=jax.ShapeDtypeStruct((m, ngroups * gw), F32),
        scratch_shapes=[pltpu.VMEM((tm, d), BF16)],
        compiler_params=_params("parallel", "arbitrary"),
        name="proj",
    )(x, norm_g, w_bf, qk_gain, gmat, cos_t, sa_t, sb_t)


def _attn_body(lam_ref, q_ref, k_ref, v_ref, g_ref, beta_ref, o_ref,
               kb_ref, vb_ref, qs_ref, m_ref, l_ref, acc_ref, *, tq, out_scale):
    qi = pl.program_id(1)
    half = LANES // 2

    @pl.when(qi == 0)
    def _():
        kb_ref[...] = k_ref[...].astype(BF16)
        vb_ref[...] = v_ref[...].astype(BF16)

    q = q_ref[...] * (1.0 / math.sqrt(half))
    lane = lax.broadcasted_iota(jnp.int32, (1, LANES), 1)
    qs_ref[0:tq, :] = jnp.where(lane < half, q, 0.0).astype(BF16)
    qs_ref[tq:2 * tq, :] = jnp.where(lane >= half, q, 0.0).astype(BF16)
    m_ref[...] = jnp.full(m_ref.shape, MASK_VALUE, F32)
    l_ref[...] = jnp.zeros(l_ref.shape, F32)
    acc_ref[...] = jnp.zeros(acc_ref.shape, F32)

    def block(j, masked):
        off = pl.multiple_of(j * tq, tq)
        s = lax.dot_general(qs_ref[...], kb_ref[pl.ds(off, tq), :], _NT, preferred_element_type=F32)
        if masked:
            row = lax.broadcasted_iota(jnp.int32, s.shape, 0)
            row = jnp.where(row >= tq, row - tq, row)
            col = lax.broadcasted_iota(jnp.int32, s.shape, 1)
            s = jnp.where(col <= row, s, MASK_VALUE)
        m_prev = m_ref[...]
        m_next = jnp.maximum(m_prev, jnp.max(s, axis=1, keepdims=True))
        alpha = jnp.exp(m_prev - m_next)
        p = jnp.exp(s - pltpu.repeat(m_next, tq // LANES, axis=1))
        l_ref[...] = alpha * l_ref[...] + jnp.sum(p, axis=1, keepdims=True)
        acc_ref[...] = alpha * acc_ref[...] + jnp.dot(
            p.astype(BF16), vb_ref[pl.ds(off, tq), :], preferred_element_type=F32)
        m_ref[...] = m_next

    def body(j, carry):
        block(j, False)
        return carry

    lax.fori_loop(0, qi, body, 0)
    block(qi, True)

    a = acc_ref[...] / l_ref[...]
    o = a[0:tq, :] - lam_ref[...] * a[tq:2 * tq, :]
    ms = jnp.mean(o * o, axis=-1, keepdims=True)
    o_ref[...] = (o * lax.rsqrt(ms + EPS) * g_ref[...] * out_scale * beta_ref[...]).astype(o_ref.dtype)


def _attn_prompt(proj, lam, gain, beta, *, nheads, out_scale):
    t = proj.shape[0]
    tq = _pick(t, 512)
    assert tq % LANES == 0
    return pl.pallas_call(
        functools.partial(_attn_body, tq=tq, out_scale=out_scale),
        grid=(nheads, t // tq),
        in_specs=[
            pl.BlockSpec((1, LANES), lambda h, i: (0, 0)),
            pl.BlockSpec((tq, LANES), lambda h, i: (i, h)),
            pl.BlockSpec((t, LANES), lambda h, i: (0, nheads + h)),
            pl.BlockSpec((t, LANES), lambda h, i: (0, 2 * nheads + h)),
            pl.BlockSpec((1, LANES), lambda h, i: (0, 0)),
            pl.BlockSpec((1, LANES), lambda h, i: (0, h)),
        ],
        out_specs=pl.BlockSpec((tq, LANES), lambda h, i: (i, h)),
        out_shape=jax.ShapeDtypeStruct((t, nheads * LANES), BF16),
        scratch_shapes=[
            pltpu.VMEM((t, LANES), BF16),
            pltpu.VMEM((t, LANES), BF16),
            pltpu.VMEM((2 * tq, LANES), BF16),
            pltpu.VMEM((2 * tq, LANES), F32),
            pltpu.VMEM((2 * tq, LANES), F32),
            pltpu.VMEM((2 * tq, LANES), F32),
        ],
        compiler_params=_params("parallel", "arbitrary"),
        name="attn_prompt",
    )(lam, proj, proj, proj, gain, beta)


def _attn_decode_body(pt_ref, lam_ref, g_ref, beta_ref, bmask_ref, q_ref, kn_ref, vn_ref, *rest,
                      npages, page, nheads, ts, out_scale):
    del pt_ref
    kp_refs = rest[:npages]
    vp_refs = rest[npages:2 * npages]
    o_ref, ks_ref, vs_ref = rest[2 * npages:]
    past = npages * page
    tail = ks_ref.shape[0] - past
    w = ks_ref.shape[1]
    ncol = bmask_ref.shape[0]
    half = LANES // 2

    for p in range(npages):
        ks_ref[p * page:(p + 1) * page, :] = kp_refs[p][...].astype(BF16)
        vs_ref[p * page:(p + 1) * page, :] = vp_refs[p][...].astype(BF16)
    ks_ref[past:, :] = jnp.zeros((tail, w), BF16)
    vs_ref[past:, :] = jnp.zeros((tail, w), BF16)
    ks_ref[past:past + ts, :] = kn_ref[0].astype(BF16)
    vs_ref[past:past + ts, :] = vn_ref[0].astype(BF16)

    q = q_ref[0] * (1.0 / math.sqrt(half))
    qt = (jnp.tile(q, (ncol // ts, 1)) * bmask_ref[...]).astype(BF16)
    s = lax.dot_general(ks_ref[...], qt, _NT, preferred_element_type=F32)
    row = lax.broadcasted_iota(jnp.int32, s.shape, 0)
    col = lax.broadcasted_iota(jnp.int32, s.shape, 1)
    tq = col % ts
    s = jnp.where((row < past) | (row - past <= tq), s, MASK_VALUE)
    m = jnp.max(s, axis=0, keepdims=True)
    p = jnp.exp(s - m)
    l = jnp.sum(p, axis=0, keepdims=True)
    col1 = lax.broadcasted_iota(jnp.int32, (1, ncol), 1)
    wgt = jnp.where(col1 < ncol // 2, 1.0, -lam_ref[:, 0:1]) / l
    pw = (p * wgt).astype(BF16)
    o2 = lax.dot_general(pw, vs_ref[...], _TA, preferred_element_type=F32)
    for h in range(nheads):
        r0 = h * 8
        r1 = ncol // 2 + h * 8
        cs = slice(h * LANES, (h + 1) * LANES)
        o = o2[r0:r0 + 8, cs] + o2[r1:r1 + 8, cs]
        ms = jnp.mean(o * o, axis=-1, keepdims=True)
        o = o * lax.rsqrt(ms + EPS) * g_ref[...] * out_scale * beta_ref[:, cs]
        o_ref[0, :, cs] = o[0:ts, :].astype(o_ref.dtype)


def _attn_decode(page_table, lam, gain, beta, bmask, proj3, cache_k3, cache_v3, *, nheads, out_scale):
    nb, ts, _ = proj3.shape
    npages = page_table.shape[1]
    page, w = cache_k3.shape[1], cache_k3.shape[2]
    ncol = bmask.shape[0]
    tail = 128
    assert 8 % ts == 0 and w == nheads * LANES

    def page_spec(p):
        return pl.BlockSpec((None, page, w), lambda b, pt: (pt[b, p], 0, 0))

    grid_spec = pltpu.PrefetchScalarGridSpec(
        num_scalar_prefetch=1,
        grid=(nb,),
        in_specs=[
            pl.BlockSpec((1, LANES), lambda b, pt: (0, 0)),
            pl.BlockSpec((1, LANES), lambda b, pt: (0, 0)),
            pl.BlockSpec((1, w), lambda b, pt: (0, 0)),
            pl.BlockSpec((ncol, w), lambda b, pt: (0, 0)),
            pl.BlockSpec((1, ts, w), lambda b, pt: (b, 0, 0)),
            pl.BlockSpec((1, ts, w), lambda b, pt: (b, 0, 1)),
            pl.BlockSpec((1, ts, w), lambda b, pt: (b, 0, 2)),
        ] + [page_spec(p) for p in range(npages)] + [page_spec(p) for p in range(npages)],
        out_specs=pl.BlockSpec((1, ts, w), lambda b, pt: (b, 0, 0)),
        scratch_shapes=[
            pltpu.VMEM((npages * page + tail, w), BF16),
            pltpu.VMEM((npages * page + tail, w), BF16),
        ],
    )
    return pl.pallas_call(
        functools.partial(_attn_decode_body, npages=npages, page=page, nheads=nheads, ts=ts,
                          out_scale=out_scale),
        grid_spec=grid_spec,
        out_shape=jax.ShapeDtypeStruct((nb, ts, w), BF16),
        compiler_params=_params("arbitrary"),
        name="attn_decode",
    )(page_table, lam, gain, beta, bmask, proj3, proj3, proj3,
      *([cache_k3] * npages), *([cache_v3] * npages))


def _hgrn_body(rq_ref, rf_ref, ri_ref, rg_ref, lb_ref, gn_ref, beta_ref, tri_ref, *rest,
               nheads, dk, chunk, batched):
    if batched:
        s0_ref, rec_ref, sout_ref, st_ref, o_scr = rest
        rq, rf, ri, rg = rq_ref[0], rf_ref[0], ri_ref[0], rg_ref[0]
        for h in range(nheads):
            st_ref[h] = s0_ref[0, h].T
    else:
        rec_ref, sout_ref, st_ref, o_scr = rest
        rq, rf, ri, rg = rq_ref[...], rf_ref[...], ri_ref[...], rg_ref[...]

        @pl.when(pl.program_id(0) == 0)
        def _():
            st_ref[...] = jnp.zeros(st_ref.shape, F32)

    rows = rq.shape[0]
    lb = lb_ref[...]
    q = rq * jax.nn.sigmoid(rq)
    fg = lb + (1.0 - lb) * jax.nn.sigmoid(rf)
    logf = jnp.log(fg)
    kk = 1.0 - fg
    tri = tri_ref[...]
    h1, h2, h3 = _split3(logf)
    b = (jnp.dot(tri, h1, preferred_element_type=F32)
         + jnp.dot(tri, h2, preferred_element_type=F32)
         + jnp.dot(tri, h3, preferred_element_type=F32))
    q_in = (q * jnp.exp(b)).astype(BF16)
    k_in = (kk * jnp.exp(-b)).astype(BF16)
    v_bf = ri.astype(BF16)
    tr = lax.broadcasted_iota(jnp.int32, (chunk, chunk), 0)
    tc = lax.broadcasted_iota(jnp.int32, (chunk, chunk), 1)
    causal = tc <= tr
    for c in range(rows // chunk):
        rs = slice(c * chunk, (c + 1) * chunk)
        b_last = b[(c + 1) * chunk - 1:(c + 1) * chunk, :]
        k_st = (kk[rs, :] * jnp.exp(b_last - b[rs, :])).astype(BF16)
        decay = jnp.exp(b_last)
        for h in range(nheads):
            cs = slice(h * dk, (h + 1) * dk)
            qc, kc, vc = q_in[rs, cs], k_in[rs, cs], v_bf[rs, cs]
            att = lax.dot_general(qc, kc, _NT, preferred_element_type=F32)
            att = jnp.where(causal, att, 0.0).astype(BF16)
            st = st_ref[h]
            o_scr[rs, cs] = (jnp.dot(att, vc, preferred_element_type=F32)
                             + lax.dot_general(qc, st.astype(BF16), _NT, preferred_element_type=F32))
            st_ref[h] = st * decay[:, cs] + lax.dot_general(vc, k_st[:, cs], _TA,
                                                            preferred_element_type=F32)
    gate = rg * jax.nn.sigmoid(rg)
    for h in range(nheads):
        cs = slice(h * dk, (h + 1) * dk)
        o = o_scr[:, cs]
        ms = jnp.mean(o * o, axis=-1, keepdims=True)
        o = (o * lax.rsqrt(ms + EPS) * gn_ref[...]) * gate[:, cs] * beta_ref[:, cs]
        if batched:
            rec_ref[0, :, cs] = o.astype(rec_ref.dtype)
        else:
            rec_ref[:, cs] = o.astype(rec_ref.dtype)

    if batched:
        for h in range(nheads):
            sout_ref[0, h] = st_ref[h].T
    else:
        @pl.when(pl.program_id(0) == pl.num_programs(0) - 1)
        def _():
            for h in range(nheads):
                sout_ref[h] = st_ref[h].T


def _tri(rows, chunk):
    r = jnp.arange(rows)
    return ((r[:, None] >= r[None, :]) & (r[:, None] // chunk == r[None, :] // chunk)).astype(BF16)


def _hgrn_prompt(proj, lb, gain, beta, *, nheads, dk, dv):
    t = proj.shape[0]
    gw = nheads * dk
    chunk = math.gcd(t, REC_CHUNK)
    rows = _pick(t, 4 * chunk)
    assert dk == dv == LANES
    specs = [pl.BlockSpec((rows, gw), functools.partial(lambda i, g: (i, g), g=3 + g)) for g in range(4)]
    return pl.pallas_call(
        functools.partial(_hgrn_body, nheads=nheads, dk=dk, chunk=chunk, batched=False),
        grid=(t // rows,),
        in_specs=specs + [
            pl.BlockSpec((1, gw), lambda i: (0, 0)),
            pl.BlockSpec((1, dv), lambda i: (0, 0)),
            pl.BlockSpec((1, gw), lambda i: (0, 0)),
            pl.BlockSpec((rows, rows), lambda i: (0, 0)),
        ],
        out_specs=(pl.BlockSpec((rows, gw), lambda i: (i, 0)),
                   pl.BlockSpec((nheads, dk, dv), lambda i: (0, 0, 0))),
        out_shape=(jax.ShapeDtypeStruct((t, gw), BF16),
                   jax.ShapeDtypeStruct((nheads, dk, dv), F32)),
        scratch_shapes=[pltpu.VMEM((nheads, dv, dk), F32), pltpu.VMEM((rows, gw), F32)],
        compiler_params=_params("arbitrary"),
        name="hgrn_prompt",
    )(proj, proj, proj, proj, lb, gain, beta, _tri(rows, chunk))


def _hgrn_decode(proj3, lb, gain, beta, s0, *, nheads, dk, dv):
    nb, ts, _ = proj3.shape
    gw = nheads * dk
    chunk = math.gcd(ts, REC_CHUNK)
    assert dk == dv == LANES and chunk == ts
    specs = [pl.BlockSpec((1, ts, gw), functools.partial(lambda b, g: (b, 0, g), g=3 + g)) for g in range(4)]
    return pl.pallas_call(
        functools.partial(_hgrn_body, nheads=nheads, dk=dk, chunk=chunk, batched=True),
        grid=(nb,),
        in_specs=specs + [
            pl.BlockSpec((1, gw), lambda b: (0, 0)),
            pl.BlockSpec((1, dv), lambda b: (0, 0)),
            pl.BlockSpec((1, gw), lambda b: (0, 0)),
            pl.BlockSpec((ts, ts), lambda b: (0, 0)),
            pl.BlockSpec((1, nheads, dk, dv), lambda b: (b, 0, 0, 0)),
        ],
        out_specs=(pl.BlockSpec((1, ts, gw), lambda b: (b, 0, 0)),
                   pl.BlockSpec((1, nheads, dk, dv), lambda b: (b, 0, 0, 0))),
        out_shape=(jax.ShapeDtypeStruct((nb, ts, gw), BF16),
                   jax.ShapeDtypeStruct((nb, nheads, dk, dv), F32)),
        scratch_shapes=[pltpu.VMEM((nheads, dv, dk), F32), pltpu.VMEM((ts, gw), F32)],
        compiler_params=_params("parallel"),
        name="hgrn_decode",
    )(proj3, proj3, proj3, proj3, lb, gain, beta, _tri(ts, chunk), s0)


def _finish_body(x_ref, att_ref, rec_ref, wo_ref, g_ref, wu_ref, wd_ref, o_ref, hn_ref):
    j = pl.program_id(1)

    @pl.when(j == 0)
    def _():
        aw = att_ref.shape[1]
        h = (x_ref[...]
             + jnp.dot(att_ref[...], wo_ref[0:aw, :], preferred_element_type=F32)
             + jnp.dot(rec_ref[...], wo_ref[aw:, :], preferred_element_type=F32))
        ms = jnp.mean(h * h, axis=-1, keepdims=True)
        hn_ref[...] = (h * lax.rsqrt(ms + EPS) * g_ref[...]).astype(BF16)
        o_ref[...] = h

    u = jnp.maximum(jnp.dot(hn_ref[...], wu_ref[...], preferred_element_type=F32), 0.0)
    o_ref[...] += jnp.dot((u * u).astype(BF16), wd_ref[...], preferred_element_type=F32)


def _finish(x, att, rec, wo_bf, ffn_g, wu_bf, wd_bf):
    m, d = x.shape
    aw, rw = att.shape[1], rec.shape[1]
    ff = wu_bf.shape[1]
    tm = _pick(m, 512)
    fc = _pick(ff, 512)
    return pl.pallas_call(
        _finish_body,
        grid=(m // tm, ff // fc),
        in_specs=[
            pl.BlockSpec((tm, d), lambda i, j: (i, 0)),
            pl.BlockSpec((tm, aw), lambda i, j: (i, 0)),
            pl.BlockSpec((tm, rw), lambda i, j: (i, 0)),
            pl.BlockSpec((aw + rw, d), lambda i, j: (0, 0)),
            pl.BlockSpec((1, d), lambda i, j: (0, 0)),
            pl.BlockSpec((d, fc), lambda i, j: (0, j)),
            pl.BlockSpec((fc, d), lambda i, j: (j, 0)),
        ],
        out_specs=pl.BlockSpec((tm, d), lambda i, j: (i, 0)),
        out_shape=jax.ShapeDtypeStruct((m, d), F32),
        scratch_shapes=[pltpu.VMEM((tm, d), BF16)],
        compiler_params=_params("parallel", "arbitrary"),
        name="finish",
    )(x, att, rec, wo_bf, ffn_g, wu_bf, wd_bf)


def _rope_tables(pos, head_dim):
    rot = head_dim // 4
    half = rot // 2
    inv = ROPE_THETA ** (-jnp.arange(half, dtype=F32) * 2.0 / rot)
    ang = pos.astype(F32)[:, None] * inv[None, :]
    cos, sin = jnp.cos(ang), jnp.sin(ang)
    ones = jnp.ones((pos.shape[0], head_dim - rot), F32)
    zeros = jnp.zeros((pos.shape[0], head_dim - rot), F32)
    zh = jnp.zeros_like(sin)
    c = jnp.concatenate([cos, cos, ones], axis=1)
    sa = jnp.concatenate([-sin, zh, zeros], axis=1)
    sb = jnp.concatenate([zh, sin, zeros], axis=1)
    reps = LANES // head_dim
    return tuple(jnp.tile(a, (1, reps)) for a in (c, sa, sb))


def kernel(x_prompt, x_sample, cache_k, cache_v, state_rec, page_table, attn_norm, w_in, q_norm, k_norm,
           lambda_q1, lambda_k1, lambda_q2, lambda_k2, att_out_norm, rec_lb_logits, rec_out_norm,
           beta_att, beta_rec, w_out, ffn_norm, w_up, w_down):
    bp, tp, d = x_prompt.shape
    bs, ts, _ = x_sample.shape
    depth = w_in.shape[0]
    n_pool, page, nheads, _, head_dim = cache_k.shape[1:]
    vd = cache_v.shape[-1]
    nrec, dk, dv = state_rec.shape[2:]
    npages = page_table.shape[1]
    past = npages * page
    aw = nheads * vd
    gw = nrec * dk
    assert bp == 1 and 2 * head_dim == vd == LANES and aw == gw and w_in.shape[2] == 7 * gw

    lam_inits = tuple(0.8 - 0.6 * math.exp(-0.3 * l) for l in range(depth))
    lam_all, lb_all = _prep(lambda_q1, lambda_k1, lambda_q2, lambda_k2, rec_lb_logits, lam_inits)

    lane_map = jnp.arange(aw) // head_dim
    gmat = (lane_map[:, None] == lane_map[None, :]).astype(BF16)
    rope_p = _rope_tables(jnp.arange(tp), head_dim)
    rope_s = tuple(jnp.tile(a, (bs, 1)) for a in _rope_tables(past + jnp.arange(ts), head_dim))
    ncol = 2 * nheads * 8
    col = jnp.arange(ncol)
    bmask = ((col[:, None] // (nheads * 8) == (jnp.arange(aw)[None, :] // head_dim) % 2)
             & ((col[:, None] // 8) % nheads == jnp.arange(aw)[None, :] // vd)).astype(F32)

    hp = x_prompt.reshape(tp, d)
    hs = x_sample.reshape(bs * ts, d)
    outs = [[] for _ in range(6)]
    for l in range(depth):
        w_in_bf = w_in[l].astype(BF16)
        w_out_bf = w_out[l].astype(BF16)
        w_up_bf = w_up[l].astype(BF16)
        w_down_bf = w_down[l].astype(BF16)
        qk_gain = jnp.stack([jnp.tile(q_norm[l], aw // head_dim),
                             jnp.tile(k_norm[l], aw // head_dim)]).reshape(2, 1, aw)
        norm_g = attn_norm[l].reshape(1, d)
        lam = lam_all[l:l + 1]
        lb = lb_all[l:l + 1]
        att_g = att_out_norm[l].reshape(1, vd)
        rec_g = rec_out_norm[l].reshape(1, dv)
        b_att = beta_att[l].reshape(1, aw)
        b_rec = beta_rec[l].reshape(1, gw)
        ffn_g = ffn_norm[l].reshape(1, d)
        out_scale = 1.0 - lam_inits[l]

        proj = _proj(hp, norm_g, w_in_bf, qk_gain, gmat, *rope_p, head_dim=head_dim)
        att = _attn_prompt(proj, lam, att_g, b_att, nheads=nheads, out_scale=out_scale)
        rec, s_new = _hgrn_prompt(proj, lb, rec_g, b_rec, nheads=nrec, dk=dk, dv=dv)
        hp = _finish(hp, att, rec, w_out_bf, ffn_g, w_up_bf, w_down_bf)
        outs[0].append(proj[:, aw:2 * aw].reshape(bp, tp, nheads, 2, head_dim))
        outs[1].append(proj[:, 2 * aw:3 * aw].reshape(bp, tp, nheads, vd))
        outs[2].append(s_new.reshape(bp, nrec, dk, dv).astype(state_rec.dtype))

        proj_s = _proj(hs, norm_g, w_in_bf, qk_gain, gmat, *rope_s, head_dim=head_dim)
        proj3 = proj_s.reshape(bs, ts, 7 * gw)
        att_s = _attn_decode(page_table, lam, att_g, b_att, bmask, proj3,
                             cache_k[l].reshape(n_pool, page, aw), cache_v[l].reshape(n_pool, page, aw),
                             nheads=nheads, out_scale=out_scale)
        rec_s, s_new = _hgrn_decode(proj3, lb, rec_g, b_rec, state_rec[l], nheads=nrec, dk=dk, dv=dv)
        hs = _finish(hs, att_s.reshape(bs * ts, aw), rec_s.reshape(bs * ts, gw),
                     w_out_bf, ffn_g, w_up_bf, w_down_bf)
        outs[3].append(proj_s[:, aw:2 * aw].reshape(bs, ts, nheads, 2, head_dim))
        outs[4].append(proj_s[:, 2 * aw:3 * aw].reshape(bs, ts, nheads, vd))
        outs[5].append(s_new.astype(state_rec.dtype))

    return (hp.reshape(bp, tp, d), hs.reshape(bs, ts, d),
            jnp.stack(outs[0]), jnp.stack(outs[1]), jnp.stack(outs[2]),
            jnp.stack(outs[3]), jnp.stack(outs[4]), jnp.stack(outs[5]))
```

```python
import functools
import math

import jax
import jax.numpy as jnp
from jax import lax
from jax.experimental import pallas as pl
from jax.experimental.pallas import tpu as pltpu

F32 = jnp.float32
BF16 = jnp.bfloat16

EPS = 1e-6
ROPE_THETA = 500000.0
MASK_VALUE = -1e30
LANES = 128
MXU_TILE = 256
LOG2E = 1.4426950408889634
VMEM_LIMIT_BYTES = 56 * 1024 * 1024
REC_CHUNK = 64

_NT = (((1,), (1,)), ((), ()))
_TA = (((0,), (0,)), ((), ()))


def _params(*semantics):
    return pltpu.CompilerParams(dimension_semantics=semantics, vmem_limit_bytes=VMEM_LIMIT_BYTES)


def _pick(n, pref):
    if n <= pref:
        return n
    t = pref
    while n % t:
        t //= 2
    return t


def _split3(x):
    h1 = x.astype(BF16)
    r1 = x - h1.astype(F32)
    h2 = r1.astype(BF16)
    h3 = (r1 - h2.astype(F32)).astype(BF16)
    return h1, h2, h3


def _prep_body(lq1_ref, lk1_ref, lq2_ref, lk2_ref, logit_ref, lam_ref, lb_ref, *, lam_inits):
    for l, lam_init in enumerate(lam_inits):
        s1 = jnp.sum(lq1_ref[l:l + 1, :] * lk1_ref[l:l + 1, :], axis=1, keepdims=True)
        s2 = jnp.sum(lq2_ref[l:l + 1, :] * lk2_ref[l:l + 1, :], axis=1, keepdims=True)
        lam = jnp.exp(s1) - jnp.exp(s2) + lam_init
        lam_ref[l:l + 1, :] = jnp.broadcast_to(lam, (1, LANES))
    x = logit_ref[...]
    e = jnp.exp(x - jnp.max(x, axis=0, keepdims=True))
    sm = e / jnp.sum(e, axis=0, keepdims=True)
    acc = jnp.zeros((1, x.shape[1]), F32)
    for r in range(x.shape[0]):
        acc = acc + sm[r:r + 1, :]
        lb_ref[r:r + 1, :] = acc


def _prep(lq1, lk1, lq2, lk2, logits, lam_inits):
    depth = lq1.shape[0]
    return pl.pallas_call(
        functools.partial(_prep_body, lam_inits=lam_inits),
        out_shape=(jax.ShapeDtypeStruct((depth, LANES), F32),
                   jax.ShapeDtypeStruct(logits.shape, F32)),
        name="prep",
    )(lq1, lk1, lq2, lk2, logits)


def _proj_body(x_ref, g_ref, w_ref, qkg_ref, gmat_ref, cos_ref, sa_ref, sb_ref, o_ref, xn_ref,
               *, head_dim, rot_half):
    j = pl.program_id(1)

    @pl.when(j == 0)
    def _():
        x = x_ref[...]
        ms = jnp.mean(x * x, axis=-1, keepdims=True)
        xn_ref[...] = (x * lax.rsqrt(ms + EPS) * g_ref[...]).astype(BF16)

    y = jnp.dot(xn_ref[...], w_ref[...], preferred_element_type=F32)

    @pl.when(j < 2)
    def _():
        gm = gmat_ref[...]
        gt = gm.shape[0]
        gain = qkg_ref[0]
        c, sa, sb = cos_ref[...], sa_ref[...], sb_ref[...]
        for t in range(y.shape[1] // gt):
            yt = y[:, t * gt:(t + 1) * gt]
            y2 = yt * yt
            hi = y2.astype(BF16)
            lo = (y2 - hi.astype(F32)).astype(BF16)
            ss = (jnp.dot(hi, gm, preferred_element_type=F32)
                  + jnp.dot(lo, gm, preferred_element_type=F32))
            yn = yt * lax.rsqrt(ss * (1.0 / head_dim) + EPS) * gain[:, t * gt:(t + 1) * gt]
            for u in range(gt // LANES):
                yu = yn[:, u * LANES:(u + 1) * LANES]
                lane0 = t * gt + u * LANES
                o_ref[:, lane0:lane0 + LANES] = (
                    yu * c + pltpu.roll(yu, LANES - rot_half, 1) * sa + pltpu.roll(yu, rot_half, 1) * sb)

    @pl.when(j >= 2)
    def _():
        o_ref[...] = y


def _proj(x, norm_g, w_bf, qk_gain, gmat, cos_t, sa_t, sb_t, *, head_dim):
    m, d = x.shape
    gw = qk_gain.shape[2]
    gt = gmat.shape[0]
    ngroups = w_bf.shape[1] // gw
    tm = _pick(m, 512)
    assert gw % gt == 0 and gt % LANES == 0
    return pl.pallas_call(
        functools.partial(_proj_body, head_dim=head_dim, rot_half=head_dim // 8),
        grid=(m // tm, ngroups),
        in_specs=[
            pl.BlockSpec((tm, d), lambda i, j: (i, 0)),
            pl.BlockSpec((1, d), lambda i, j: (0, 0)),
            pl.BlockSpec((d, gw), lambda i, j: (0, j)),
            pl.BlockSpec((1, 1, gw), lambda i, j: (jnp.minimum(j, 1), 0, 0)),
            pl.BlockSpec((gt, gt), lambda i, j: (0, 0)),
            pl.BlockSpec((tm, LANES), lambda i, j: (i, 0)),
            pl.BlockSpec((tm, LANES), lambda i, j: (i, 0)),
            pl.BlockSpec((tm, LANES), lambda i, j: (i, 0)),
        ],
        out_specs=pl.BlockSpec((tm, gw), lambda i, j: (i, j)),
        out_shape=jax.ShapeDtypeStruct((m, ngroups * gw), F32),
        scratch_shapes=[pltpu.VMEM((tm, d), BF16)],
        compiler_params=_params("parallel", "arbitrary"),
        name="proj",
    )(x, norm_g, w_bf, qk_gain, gmat, cos_t, sa_t, sb_t)


def _attn_body(lam_ref, q_ref, k_ref, v_ref, g_ref, beta_ref, o_ref,
               kb_ref, vb_ref, qs_ref, m_ref, l_ref, acc_ref, *, tq, wide, out_scale):
    qi = pl.program_id(1)
    half = LANES // 2

    @pl.when(qi == 0)
    def _():
        kb_ref[...] = k_ref[...].astype(BF16)
        vb_ref[...] = v_ref[...].astype(BF16)

    q = q_ref[...] * (LOG2E / math.sqrt(half))
    lane = lax.broadcasted_iota(jnp.int32, (1, LANES), 1)
    qs_ref[0:tq, :] = jnp.where(lane < half, q, 0.0).astype(BF16)
    qs_ref[tq:2 * tq, :] = jnp.where(lane >= half, q, 0.0).astype(BF16)
    m_ref[...] = jnp.full(m_ref.shape, MASK_VALUE, F32)
    l_ref[...] = jnp.zeros(l_ref.shape, F32)
    acc_ref[...] = jnp.zeros(acc_ref.shape, F32)

    def block(off, width, masked):
        s = lax.dot_general(qs_ref[...], kb_ref[pl.ds(off, width), :], _NT, preferred_element_type=F32)
        if masked:
            row = lax.broadcasted_iota(jnp.int32, s.shape, 0)
            row = jnp.where(row >= tq, row - tq, row)
            col = lax.broadcasted_iota(jnp.int32, s.shape, 1)
            s = jnp.where(col <= row, s, MASK_VALUE)
        m_prev = m_ref[...]
        m_next = jnp.maximum(m_prev, jnp.max(s, axis=1, keepdims=True))
        alpha = jnp.exp2(m_prev - m_next)
        p = jnp.exp2(s - pltpu.repeat(m_next, width // LANES, axis=1))
        l_ref[...] = alpha * l_ref[...] + jnp.sum(p, axis=1, keepdims=True)
        acc_ref[...] = alpha * acc_ref[...] + jnp.dot(
            p.astype(BF16), vb_ref[pl.ds(off, width), :], preferred_element_type=F32)
        m_ref[...] = m_next

    per_wide = wide // tq
    nwide = qi // per_wide

    def wide_body(j, carry):
        block(pl.multiple_of(j * wide, wide), wide, False)
        return carry

    def narrow_body(j, carry):
        block(pl.multiple_of((nwide * per_wide + j) * tq, tq), tq, False)
        return carry

    lax.fori_loop(0, nwide, wide_body, 0)
    lax.fori_loop(0, qi - nwide * per_wide, narrow_body, 0)
    block(pl.multiple_of(qi * tq, tq), tq, True)

    a = acc_ref[...] / l_ref[...]
    o = a[0:tq, :] - lam_ref[...] * a[tq:2 * tq, :]
    ms = jnp.mean(o * o, axis=-1, keepdims=True)
    o_ref[...] = (o * lax.rsqrt(ms + EPS) * g_ref[...] * out_scale * beta_ref[...]).astype(o_ref.dtype)


def _attn_prompt(proj, lam, gain, beta, *, nheads, out_scale):
    t = proj.shape[0]
    tq = _pick(t, 512)
    wide = _pick(t, 4 * tq)
    assert tq % LANES == 0 and wide % tq == 0
    return pl.pallas_call(
        functools.partial(_attn_body, tq=tq, wide=wide, out_scale=out_scale),
        grid=(nheads, t // tq),
        in_specs=[
            pl.BlockSpec((1, LANES), lambda h, i: (0, 0)),
            pl.BlockSpec((tq, LANES), lambda h, i: (i, h)),
            pl.BlockSpec((t, LANES), lambda h, i: (0, nheads + h)),
            pl.BlockSpec((t, LANES), lambda h, i: (0, 2 * nheads + h)),
            pl.BlockSpec((1, LANES), lambda h, i: (0, 0)),
            pl.BlockSpec((1, LANES), lambda h, i: (0, h)),
        ],
        out_specs=pl.BlockSpec((tq, LANES), lambda h, i: (i, h)),
        out_shape=jax.ShapeDtypeStruct((t, nheads * LANES), BF16),
        scratch_shapes=[
            pltpu.VMEM((t, LANES), BF16),
            pltpu.VMEM((t, LANES), BF16),
            pltpu.VMEM((2 * tq, LANES), BF16),
            pltpu.VMEM((2 * tq, LANES), F32),
            pltpu.VMEM((2 * tq, LANES), F32),
            pltpu.VMEM((2 * tq, LANES), F32),
        ],
        compiler_params=_params("parallel", "arbitrary"),
        name="attn_prompt",
    )(lam, proj, proj, proj, gain, beta)


def _attn_decode_body(pt_ref, lam_ref, g_ref, beta_ref, bmask_ref, q_ref, kn_ref, vn_ref, *rest,
                      npages, page, nheads, ts, out_scale):
    del pt_ref
    kp_refs = rest[:npages]
    vp_refs = rest[npages:2 * npages]
    o_ref, ks_ref, vs_ref, kn_scr = rest[2 * npages:]
    past = npages * page
    tail, w = kn_scr.shape
    nrow = bmask_ref.shape[0]
    half = LANES // 2

    for p in range(npages):
        ks_ref[:, p * page:(p + 1) * page] = kp_refs[p][...].astype(BF16)
        for h in range(nheads):
            vs_ref[p * page:(p + 1) * page, h * LANES:(h + 1) * LANES] = vp_refs[p][:, h, :].astype(BF16)
    kn_scr[...] = jnp.zeros((tail, w), BF16)
    kn_scr[0:ts, :] = kn_ref[0].astype(BF16)
    vs_ref[past:, :] = jnp.zeros((tail, w), BF16)
    vs_ref[past:past + ts, :] = vn_ref[0].astype(BF16)

    q = q_ref[0] * (LOG2E / math.sqrt(half))
    qt = (jnp.tile(q, (nrow // ts, 1)) * bmask_ref[...]).astype(BF16)
    s_past = jnp.dot(qt, ks_ref[...], preferred_element_type=F32)
    s_new = lax.dot_general(qt, kn_scr[...], _NT, preferred_element_type=F32)
    row = lax.broadcasted_iota(jnp.int32, s_new.shape, 0)
    col = lax.broadcasted_iota(jnp.int32, s_new.shape, 1)
    s_new = jnp.where(col <= lax.rem(row, ts), s_new, MASK_VALUE)
    m = jnp.maximum(jnp.max(s_past, axis=1, keepdims=True), jnp.max(s_new, axis=1, keepdims=True))
    p_past = jnp.exp2(s_past - m)
    p_new = jnp.exp2(s_new - m)
    l = jnp.sum(p_past, axis=1, keepdims=True) + jnp.sum(p_new, axis=1, keepdims=True)
    row1 = lax.broadcasted_iota(jnp.int32, (nrow, 1), 0)
    hr = nrow // 2
    wgt = jnp.where(row1 < hr, 1.0, -lam_ref[:, 0:1]) / l
    pw_past = p_past * wgt
    pw_new = p_new * wgt
    a_past = (pw_past[0:hr, :] + pw_past[hr:, :]).astype(BF16)
    a_new = (pw_new[0:hr, :] + pw_new[hr:, :]).astype(BF16)
    o2 = (jnp.dot(a_past, vs_ref[0:past, :], preferred_element_type=F32)
          + jnp.dot(a_new, vs_ref[past:, :], preferred_element_type=F32))
    for h in range(nheads):
        cs = slice(h * LANES, (h + 1) * LANES)
        o = o2[h * 8:(h + 1) * 8, cs]
        ms = jnp.mean(o * o, axis=-1, keepdims=True)
        o = o * lax.rsqrt(ms + EPS) * g_ref[...] * out_scale * beta_ref[:, cs]
        o_ref[0, :, cs] = o[0:ts, :].astype(o_ref.dtype)


def _attn_decode(page_table, lam, gain, beta, bmask, proj3, cache_kt, cache_v4, *, nheads, out_scale):
    nb, ts, _ = proj3.shape
    npages = page_table.shape[1]
    w, page = cache_kt.shape[1], cache_kt.shape[2]
    nrow = bmask.shape[0]
    tail = LANES
    assert 8 % ts == 0 and w == nheads * LANES and cache_v4.shape[1:] == (page, nheads, LANES)

    def k_spec(p):
        return pl.BlockSpec((None, w, page), lambda b, pt: (pt[b, p], 0, 0))

    def v_spec(p):
        return pl.BlockSpec((None, page, nheads, LANES), lambda b, pt: (pt[b, p], 0, 0, 0))

    grid_spec = pltpu.PrefetchScalarGridSpec(
        num_scalar_prefetch=1,
        grid=(nb,),
        in_specs=[
            pl.BlockSpec((1, LANES), lambda b, pt: (0, 0)),
            pl.BlockSpec((1, LANES), lambda b, pt: (0, 0)),
            pl.BlockSpec((1, w), lambda b, pt: (0, 0)),
            pl.BlockSpec((nrow, w), lambda b, pt: (0, 0)),
            pl.BlockSpec((1, ts, w), lambda b, pt: (b, 0, 0)),
            pl.BlockSpec((1, ts, w), lambda b, pt: (b, 0, 1)),
            pl.BlockSpec((1, ts, w), lambda b, pt: (b, 0, 2)),
        ] + [k_spec(p) for p in range(npages)] + [v_spec(p) for p in range(npages)],
        out_specs=pl.BlockSpec((1, ts, w), lambda b, pt: (b, 0, 0)),
        scratch_shapes=[
            pltpu.VMEM((w, npages * page), BF16),
            pltpu.VMEM((npages * page + tail, w), BF16),
            pltpu.VMEM((tail, w), BF16),
        ],
    )
    return pl.pallas_call(
        functools.partial(_attn_decode_body, npages=npages, page=page, nheads=nheads, ts=ts,
                          out_scale=out_scale),
        grid_spec=grid_spec,
        out_shape=jax.ShapeDtypeStruct((nb, ts, w), BF16),
        compiler_params=_params("arbitrary"),
        name="attn_decode",
    )(page_table, lam, gain, beta, bmask, proj3, proj3, proj3,
      *([cache_kt] * npages), *([cache_v4] * npages))


def _hgrn_body(rq_ref, rf_ref, ri_ref, rg_ref, lb_ref, gn_ref, beta_ref, tri_ref, *rest,
               nheads, dk, chunk, batched):
    if batched:
        s0_ref, rec_ref, sout_ref, st_ref, o_scr = rest
        rq, rf, ri, rg = rq_ref[0], rf_ref[0], ri_ref[0], rg_ref[0]
        for h in range(nheads):
            st_ref[h] = s0_ref[0, h].T
    else:
        rec_ref, sout_ref, st_ref, o_scr = rest
        rq, rf, ri, rg = rq_ref[...], rf_ref[...], ri_ref[...], rg_ref[...]

        @pl.when(pl.program_id(0) == 0)
        def _():
            st_ref[...] = jnp.zeros(st_ref.shape, F32)

    rows = rq.shape[0]
    lb = lb_ref[...]
    q = rq * jax.nn.sigmoid(rq)
    fg = lb + (1.0 - lb) * jax.nn.sigmoid(rf)
    logf = jnp.log(fg)
    kk = 1.0 - fg
    tri = tri_ref[...]
    h1, h2, h3 = _split3(logf)
    b = (jnp.dot(tri, h1, preferred_element_type=F32)
         + jnp.dot(tri, h2, preferred_element_type=F32)
         + jnp.dot(tri, h3, preferred_element_type=F32))
    q_in = (q * jnp.exp(b)).astype(BF16)
    k_in = (kk * jnp.exp(-b)).astype(BF16)
    v_bf = ri.astype(BF16)
    tr = lax.broadcasted_iota(jnp.int32, (chunk, chunk), 0)
    tc = lax.broadcasted_iota(jnp.int32, (chunk, chunk), 1)
    causal = tc <= tr
    for c in range(rows // chunk):
        rs = slice(c * chunk, (c + 1) * chunk)
        b_last = b[(c + 1) * chunk - 1:(c + 1) * chunk, :]
        k_st = (kk[rs, :] * jnp.exp(b_last - b[rs, :])).astype(BF16)
        decay = jnp.exp(b_last)
        for h in range(nheads):
            cs = slice(h * dk, (h + 1) * dk)
            qc, kc, vc = q_in[rs, cs], k_in[rs, cs], v_bf[rs, cs]
            att = lax.dot_general(qc, kc, _NT, preferred_element_type=F32)
            att = jnp.where(causal, att, 0.0).astype(BF16)
            st = st_ref[h]
            o_scr[rs, cs] = (jnp.dot(att, vc, preferred_element_type=F32)
                             + lax.dot_general(qc, st.astype(BF16), _NT, preferred_element_type=F32))
            st_ref[h] = st * decay[:, cs] + lax.dot_general(vc, k_st[:, cs], _TA,
                                                            preferred_element_type=F32)
    gate = rg * jax.nn.sigmoid(rg)
    for h in range(nheads):
        cs = slice(h * dk, (h + 1) * dk)
        o = o_scr[:, cs]
        ms = jnp.mean(o * o, axis=-1, keepdims=True)
        o = (o * lax.rsqrt(ms + EPS) * gn_ref[...]) * gate[:, cs] * beta_ref[:, cs]
        if batched:
            rec_ref[0, :, cs] = o.astype(rec_ref.dtype)
        else:
            rec_ref[:, cs] = o.astype(rec_ref.dtype)

    if batched:
        for h in range(nheads):
            sout_ref[0, h] = st_ref[h].T
    else:
        @pl.when(pl.program_id(0) == pl.num_programs(0) - 1)
        def _():
            for h in range(nheads):
                sout_ref[h] = st_ref[h].T


def _tri(rows, chunk):
    r = jnp.arange(rows)
    return ((r[:, None] >= r[None, :]) & (r[:, None] // chunk == r[None, :] // chunk)).astype(BF16)


def _hgrn_prompt(proj, lb, gain, beta, *, nheads, dk, dv):
    t = proj.shape[0]
    gw = nheads * dk
    chunk = math.gcd(t, REC_CHUNK)
    rows = _pick(t, 4 * chunk)
    assert dk == dv == LANES
    specs = [pl.BlockSpec((rows, gw), functools.partial(lambda i, g: (i, g), g=3 + g)) for g in range(4)]
    return pl.pallas_call(
        functools.partial(_hgrn_body, nheads=nheads, dk=dk, chunk=chunk, batched=False),
        grid=(t // rows,),
        in_specs=specs + [
            pl.BlockSpec((1, gw), lambda i: (0, 0)),
            pl.BlockSpec((1, dv), lambda i: (0, 0)),
            pl.BlockSpec((1, gw), lambda i: (0, 0)),
            pl.BlockSpec((rows, rows), lambda i: (0, 0)),
        ],
        out_specs=(pl.BlockSpec((rows, gw), lambda i: (i, 0)),
                   pl.BlockSpec((nheads, dk, dv), lambda i: (0, 0, 0))),
        out_shape=(jax.ShapeDtypeStruct((t, gw), BF16),
                   jax.ShapeDtypeStruct((nheads, dk, dv), F32)),
        scratch_shapes=[pltpu.VMEM((nheads, dv, dk), F32), pltpu.VMEM((rows, gw), F32)],
        compiler_params=_params("arbitrary"),
        name="hgrn_prompt",
    )(proj, proj, proj, proj, lb, gain, beta, _tri(rows, chunk))


def _hgrn_decode(proj3, lb, gain, beta, s0, *, nheads, dk, dv):
    nb, ts, _ = proj3.shape
    gw = nheads * dk
    chunk = math.gcd(ts, REC_CHUNK)
    assert dk == dv == LANES and chunk == ts
    specs = [pl.BlockSpec((1, ts, gw), functools.partial(lambda b, g: (b, 0, g), g=3 + g)) for g in range(4)]
    return pl.pallas_call(
        functools.partial(_hgrn_body, nheads=nheads, dk=dk, chunk=chunk, batched=True),
        grid=(nb,),
        in_specs=specs + [
            pl.BlockSpec((1, gw), lambda b: (0, 0)),
            pl.BlockSpec((1, dv), lambda b: (0, 0)),
            pl.BlockSpec((1, gw), lambda b: (0, 0)),
            pl.BlockSpec((ts, ts), lambda b: (0, 0)),
            pl.BlockSpec((1, nheads, dk, dv), lambda b: (b, 0, 0, 0)),
        ],
        out_specs=(pl.BlockSpec((1, ts, gw), lambda b: (b, 0, 0)),
                   pl.BlockSpec((1, nheads, dk, dv), lambda b: (b, 0, 0, 0))),
        out_shape=(jax.ShapeDtypeStruct((nb, ts, gw), BF16),
                   jax.ShapeDtypeStruct((nb, nheads, dk, dv), F32)),
        scratch_shapes=[pltpu.VMEM((nheads, dv, dk), F32), pltpu.VMEM((ts, gw), F32)],
        compiler_params=_params("parallel"),
        name="hgrn_decode",
    )(proj3, proj3, proj3, proj3, lb, gain, beta, _tri(ts, chunk), s0)


def _finish_body(x_ref, att_ref, rec_ref, wo_ref, g_ref, wu_ref, wd_ref, o_ref, hn_ref):
    j = pl.program_id(1)

    @pl.when(j == 0)
    def _():
        aw = att_ref.shape[1]
        h = (x_ref[...]
             + jnp.dot(att_ref[...], wo_ref[0:aw, :], preferred_element_type=F32)
             + jnp.dot(rec_ref[...], wo_ref[aw:, :], preferred_element_type=F32))
        ms = jnp.mean(h * h, axis=-1, keepdims=True)
        hn_ref[...] = (h * lax.rsqrt(ms + EPS) * g_ref[...]).astype(BF16)
        o_ref[...] = h

    u = jnp.maximum(jnp.dot(hn_ref[...], wu_ref[...], preferred_element_type=F32), 0.0)
    o_ref[...] += jnp.dot((u * u).astype(BF16), wd_ref[...], preferred_element_type=F32)


def _finish(x, att, rec, wo_bf, ffn_g, wu_bf, wd_bf):
    m, d = x.shape
    aw, rw = att.shape[1], rec.shape[1]
    ff = wu_bf.shape[1]
    tm = _pick(m, 512)
    fc = _pick(ff, 512)
    return pl.pallas_call(
        _finish_body,
        grid=(m // tm, ff // fc),
        in_specs=[
            pl.BlockSpec((tm, d), lambda i, j: (i, 0)),
            pl.BlockSpec((tm, aw), lambda i, j: (i, 0)),
            pl.BlockSpec((tm, rw), lambda i, j: (i, 0)),
            pl.BlockSpec((aw + rw, d), lambda i, j: (0, 0)),
            pl.BlockSpec((1, d), lambda i, j: (0, 0)),
            pl.BlockSpec((d, fc), lambda i, j: (0, j)),
            pl.BlockSpec((fc, d), lambda i, j: (j, 0)),
        ],
        out_specs=pl.BlockSpec((tm, d), lambda i, j: (i, 0)),
        out_shape=jax.ShapeDtypeStruct((m, d), F32),
        scratch_shapes=[pltpu.VMEM((tm, d), BF16)],
        compiler_params=_params("parallel", "arbitrary"),
        name="finish",
    )(x, att, rec, wo_bf, ffn_g, wu_bf, wd_bf)


def _rope_tables(pos, head_dim):
    rot = head_dim // 4
    half = rot // 2
    inv = ROPE_THETA ** (-jnp.arange(half, dtype=F32) * 2.0 / rot)
    ang = pos.astype(F32)[:, None] * inv[None, :]
    cos, sin = jnp.cos(ang), jnp.sin(ang)
    ones = jnp.ones((pos.shape[0], head_dim - rot), F32)
    zeros = jnp.zeros((pos.shape[0], head_dim - rot), F32)
    zh = jnp.zeros_like(sin)
    c = jnp.concatenate([cos, cos, ones], axis=1)
    sa = jnp.concatenate([-sin, zh, zeros], axis=1)
    sb = jnp.concatenate([zh, sin, zeros], axis=1)
    reps = LANES // head_dim
    return tuple(jnp.tile(a, (1, reps)) for a in (c, sa, sb))


def kernel(x_prompt, x_sample, cache_k, cache_v, state_rec, page_table, attn_norm, w_in, q_norm, k_norm,
           lambda_q1, lambda_k1, lambda_q2, lambda_k2, att_out_norm, rec_lb_logits, rec_out_norm,
           beta_att, beta_rec, w_out, ffn_norm, w_up, w_down):
    bp, tp, d = x_prompt.shape
    bs, ts, _ = x_sample.shape
    depth = w_in.shape[0]
    n_pool, page, nheads, _, head_dim = cache_k.shape[1:]
    vd = cache_v.shape[-1]
    nrec, dk, dv = state_rec.shape[2:]
    npages = page_table.shape[1]
    past = npages * page
    aw = nheads * vd
    gw = nrec * dk
    assert bp == 1 and 2 * head_dim == vd == LANES and aw == gw and w_in.shape[2] == 7 * gw

    lam_inits = tuple(0.8 - 0.6 * math.exp(-0.3 * l) for l in range(depth))
    lam_all, lb_all = _prep(lambda_q1, lambda_k1, lambda_q2, lambda_k2, rec_lb_logits, lam_inits)

    lane_map = jnp.arange(MXU_TILE) // head_dim
    gmat = (lane_map[:, None] == lane_map[None, :]).astype(BF16)
    rope_p = _rope_tables(jnp.arange(tp), head_dim)
    rope_s = tuple(jnp.tile(a, (bs, 1)) for a in _rope_tables(past + jnp.arange(ts), head_dim))
    srow = jnp.arange(2 * nheads * 8)
    bmask = ((srow[:, None] // (nheads * 8) == (jnp.arange(aw)[None, :] // head_dim) % 2)
             & ((srow[:, None] // 8) % nheads == jnp.arange(aw)[None, :] // vd)).astype(F32)

    hp = x_prompt.reshape(tp, d)
    hs = x_sample.reshape(bs * ts, d)
    outs = [[] for _ in range(6)]
    for l in range(depth):
        w_in_bf = w_in[l].astype(BF16)
        w_out_bf = w_out[l].astype(BF16)
        w_up_bf = w_up[l].astype(BF16)
        w_down_bf = w_down[l].astype(BF16)
        qk_gain = jnp.stack([jnp.tile(q_norm[l], aw // head_dim),
                             jnp.tile(k_norm[l], aw // head_dim)]).reshape(2, 1, aw)
        norm_g = attn_norm[l].reshape(1, d)
        lam = lam_all[l:l + 1]
        lb = lb_all[l:l + 1]
        att_g = att_out_norm[l].reshape(1, vd)
        rec_g = rec_out_norm[l].reshape(1, dv)
        b_att = beta_att[l].reshape(1, aw)
        b_rec = beta_rec[l].reshape(1, gw)
        ffn_g = ffn_norm[l].reshape(1, d)
        out_scale = 1.0 - lam_inits[l]

        proj = _proj(hp, norm_g, w_in_bf, qk_gain, gmat, *rope_p, head_dim=head_dim)
        att = _attn_prompt(proj, lam, att_g, b_att, nheads=nheads, out_scale=out_scale)
        rec, s_new = _hgrn_prompt(proj, lb, rec_g, b_rec, nheads=nrec, dk=dk, dv=dv)
        hp = _finish(hp, att, rec, w_out_bf, ffn_g, w_up_bf, w_down_bf)
        outs[0].append(proj[:, aw:2 * aw].reshape(bp, tp, nheads, 2, head_dim))
        outs[1].append(proj[:, 2 * aw:3 * aw].reshape(bp, tp, nheads, vd))
        outs[2].append(s_new.reshape(bp, nrec, dk, dv).astype(state_rec.dtype))

        proj_s = _proj(hs, norm_g, w_in_bf, qk_gain, gmat, *rope_s, head_dim=head_dim)
        proj3 = proj_s.reshape(bs, ts, 7 * gw)
        cache_kt = jnp.transpose(cache_k[l], (0, 2, 3, 4, 1)).reshape(n_pool, aw, page)
        att_s = _attn_decode(page_table, lam, att_g, b_att, bmask, proj3, cache_kt, cache_v[l],
                             nheads=nheads, out_scale=out_scale)
        rec_s, s_new = _hgrn_decode(proj3, lb, rec_g, b_rec, state_rec[l], nheads=nrec, dk=dk, dv=dv)
        hs = _finish(hs, att_s.reshape(bs * ts, aw), rec_s.reshape(bs * ts, gw),
                     w_out_bf, ffn_g, w_up_bf, w_down_bf)
        outs[3].append(proj_s[:, aw:2 * aw].reshape(bs, ts, nheads, 2, head_dim))
        outs[4].append(proj_s[:, 2 * aw:3 * aw].reshape(bs, ts, nheads, vd))
        outs[5].append(s_new.astype(state_rec.dtype))

    return (hp.reshape(bp, tp, d), hs.reshape(bs, ts, d),
            jnp.stack(outs[0]), jnp.stack(outs[1]), jnp.stack(outs[2]),
            jnp.stack(outs[3]), jnp.stack(outs[4]), jnp.stack(outs[5]))
```

```python
import functools
import math

import jax
import jax.numpy as jnp
from jax import lax
from jax.experimental import pallas as pl
from jax.experimental.pallas import tpu as pltpu

F32 = jnp.float32
BF16 = jnp.bfloat16

EPS = 1e-6
ROPE_THETA = 500000.0
MASK_VALUE = -1e30
LANES = 128
MXU_TILE = 256
LOG2E = 1.4426950408889634
VMEM_LIMIT_BYTES = 56 * 1024 * 1024
REC_CHUNK = 64

_NT = (((1,), (1,)), ((), ()))
_TA = (((0,), (0,)), ((), ()))


def _params(*semantics):
    return pltpu.CompilerParams(dimension_semantics=semantics, vmem_limit_bytes=VMEM_LIMIT_BYTES)


def _pick(n, pref):
    if n <= pref:
        return n
    t = pref
    while n % t:
        t //= 2
    return t


def _split3(x):
    h1 = x.astype(BF16)
    r1 = x - h1.astype(F32)
    h2 = r1.astype(BF16)
    h3 = (r1 - h2.astype(F32)).astype(BF16)
    return h1, h2, h3


def _prep_body(lq1_ref, lk1_ref, lq2_ref, lk2_ref, logit_ref, lam_ref, lb_ref, *, lam_inits):
    for l, lam_init in enumerate(lam_inits):
        s1 = jnp.sum(lq1_ref[l:l + 1, :] * lk1_ref[l:l + 1, :], axis=1, keepdims=True)
        s2 = jnp.sum(lq2_ref[l:l + 1, :] * lk2_ref[l:l + 1, :], axis=1, keepdims=True)
        lam = jnp.exp(s1) - jnp.exp(s2) + lam_init
        lam_ref[l:l + 1, :] = jnp.broadcast_to(lam, (1, LANES))
    x = logit_ref[...]
    e = jnp.exp(x - jnp.max(x, axis=0, keepdims=True))
    sm = e / jnp.sum(e, axis=0, keepdims=True)
    acc = jnp.zeros((1, x.shape[1]), F32)
    for r in range(x.shape[0]):
        acc = acc + sm[r:r + 1, :]
        lb_ref[r:r + 1, :] = acc


def _prep(lq1, lk1, lq2, lk2, logits, lam_inits):
    depth = lq1.shape[0]
    return pl.pallas_call(
        functools.partial(_prep_body, lam_inits=lam_inits),
        out_shape=(jax.ShapeDtypeStruct((depth, LANES), F32),
                   jax.ShapeDtypeStruct(logits.shape, F32)),
        name="prep",
    )(lq1, lk1, lq2, lk2, logits)


def _proj_body(x_ref, g_ref, w_ref, qkg_ref, gmat_ref, cos_ref, sa_ref, sb_ref, o_ref, xn_ref,
               *, head_dim, rot_half):
    j = pl.program_id(1)

    @pl.when(j == 0)
    def _():
        x = x_ref[...]
        ms = jnp.mean(x * x, axis=-1, keepdims=True)
        xn_ref[...] = (x * lax.rsqrt(ms + EPS) * g_ref[...]).astype(BF16)

    y = jnp.dot(xn_ref[...], w_ref[...], preferred_element_type=F32)

    @pl.when(j < 2)
    def _():
        gm = gmat_ref[...]
        gt = gm.shape[0]
        gain = qkg_ref[0]
        c, sa, sb = cos_ref[...], sa_ref[...], sb_ref[...]
        for t in range(y.shape[1] // gt):
            yt = y[:, t * gt:(t + 1) * gt]
            y2 = yt * yt
            hi = y2.astype(BF16)
            lo = (y2 - hi.astype(F32)).astype(BF16)
            ss = (jnp.dot(hi, gm, preferred_element_type=F32)
                  + jnp.dot(lo, gm, preferred_element_type=F32))
            yn = yt * lax.rsqrt(ss * (1.0 / head_dim) + EPS) * gain[:, t * gt:(t + 1) * gt]
            for u in range(gt // LANES):
                yu = yn[:, u * LANES:(u + 1) * LANES]
                lane0 = t * gt + u * LANES
                o_ref[:, lane0:lane0 + LANES] = (
                    yu * c + pltpu.roll(yu, LANES - rot_half, 1) * sa + pltpu.roll(yu, rot_half, 1) * sb)

    @pl.when(j >= 2)
    def _():
        o_ref[...] = y


def _proj(x, norm_g, w_bf, qk_gain, gmat, cos_t, sa_t, sb_t, *, head_dim):
    m, d = x.shape
    gw = qk_gain.shape[2]
    gt = gmat.shape[0]
    ngroups = w_bf.shape[1] // gw
    tm = _pick(m, 512)
    assert gw % gt == 0 and gt % LANES == 0
    return pl.pallas_call(
        functools.partial(_proj_body, head_dim=head_dim, rot_half=head_dim // 8),
        grid=(m // tm, ngroups),
        in_specs=[
            pl.BlockSpec((tm, d), lambda i, j: (i, 0)),
            pl.BlockSpec((1, d), lambda i, j: (0, 0)),
            pl.BlockSpec((d, gw), lambda i, j: (0, j)),
            pl.BlockSpec((1, 1, gw), lambda i, j: (jnp.minimum(j, 1), 0, 0)),
            pl.BlockSpec((gt, gt), lambda i, j: (0, 0)),
            pl.BlockSpec((tm, LANES), lambda i, j: (i, 0)),
            pl.BlockSpec((tm, LANES), lambda i, j: (i, 0)),
            pl.BlockSpec((tm, LANES), lambda i, j: (i, 0)),
        ],
        out_specs=pl.BlockSpec((tm, gw), lambda i, j: (i, j)),
        out_shape=jax.ShapeDtypeStruct((m, ngroups * gw), F32),
        scratch_shapes=[pltpu.VMEM((tm, d), BF16)],
        compiler_params=_params("parallel", "arbitrary"),
        name="proj",
    )(x, norm_g, w_bf, qk_gain, gmat, cos_t, sa_t, sb_t)


def _attn_body(lam_ref, q_ref, k_ref, v_ref, g_ref, beta_ref, o_ref,
               kb_ref, vb_ref, qs_ref, m_ref, l_ref, acc_ref, *, tq, wide, out_scale):
    qi = pl.program_id(1)
    half = LANES // 2

    @pl.when(qi == 0)
    def _():
        kb_ref[...] = k_ref[...].astype(BF16)
        vb_ref[...] = v_ref[...].astype(BF16)

    q = q_ref[...] * (LOG2E / math.sqrt(half))
    lane = lax.broadcasted_iota(jnp.int32, (1, LANES), 1)
    qs_ref[0:tq, :] = jnp.where(lane < half, q, 0.0).astype(BF16)
    qs_ref[tq:2 * tq, :] = jnp.where(lane >= half, q, 0.0).astype(BF16)
    m_ref[...] = jnp.full(m_ref.shape, MASK_VALUE, F32)
    l_ref[...] = jnp.zeros(l_ref.shape, F32)
    acc_ref[...] = jnp.zeros(acc_ref.shape, F32)

    def block(off, width, masked):
        kblk = kb_ref[pl.ds(off, width), :]
        vblk = vb_ref[pl.ds(off, width), :]
        for c in range(2):
            rs = slice(c * tq, (c + 1) * tq)
            s = lax.dot_general(qs_ref[rs, :], kblk, _NT, preferred_element_type=F32)
            if masked:
                row = lax.broadcasted_iota(jnp.int32, s.shape, 0)
                col = lax.broadcasted_iota(jnp.int32, s.shape, 1)
                s = jnp.where(col <= row, s, MASK_VALUE)
            m_prev = m_ref[rs, :]
            m_next = jnp.maximum(m_prev, jnp.max(s, axis=1, keepdims=True))
            alpha = jnp.exp2(m_prev - m_next)
            p = jnp.exp2(s - jnp.tile(m_next, (1, width // LANES)))
            l_ref[rs, :] = alpha * l_ref[rs, :] + jnp.sum(p, axis=1, keepdims=True)
            acc_ref[rs, :] = alpha * acc_ref[rs, :] + jnp.dot(
                p.astype(BF16), vblk, preferred_element_type=F32)
            m_ref[rs, :] = m_next

    per_wide = wide // tq
    nwide = qi // per_wide

    def wide_body(j, carry):
        block(pl.multiple_of(j * wide, wide), wide, False)
        return carry

    def narrow_body(j, carry):
        block(pl.multiple_of((nwide * per_wide + j) * tq, tq), tq, False)
        return carry

    lax.fori_loop(0, nwide, wide_body, 0)
    lax.fori_loop(0, qi - nwide * per_wide, narrow_body, 0)
    block(pl.multiple_of(qi * tq, tq), tq, True)

    a = acc_ref[...] / l_ref[...]
    o = a[0:tq, :] - lam_ref[...] * a[tq:2 * tq, :]
    ms = jnp.mean(o * o, axis=-1, keepdims=True)
    o_ref[...] = (o * lax.rsqrt(ms + EPS) * g_ref[...] * out_scale * beta_ref[...]).astype(o_ref.dtype)


def _attn_prompt(proj, lam, gain, beta, *, nheads, out_scale):
    t = proj.shape[0]
    tq = _pick(t, 512)
    wide = _pick(t, 4 * tq)
    assert tq % LANES == 0 and wide % tq == 0
    return pl.pallas_call(
        functools.partial(_attn_body, tq=tq, wide=wide, out_scale=out_scale),
        grid=(nheads, t // tq),
        in_specs=[
            pl.BlockSpec((1, LANES), lambda h, i: (0, 0)),
            pl.BlockSpec((tq, LANES), lambda h, i: (i, h)),
            pl.BlockSpec((t, LANES), lambda h, i: (0, nheads + h)),
            pl.BlockSpec((t, LANES), lambda h, i: (0, 2 * nheads + h)),
            pl.BlockSpec((1, LANES), lambda h, i: (0, 0)),
            pl.BlockSpec((1, LANES), lambda h, i: (0, h)),
        ],
        out_specs=pl.BlockSpec((tq, LANES), lambda h, i: (i, h)),
        out_shape=jax.ShapeDtypeStruct((t, nheads * LANES), BF16),
        scratch_shapes=[
            pltpu.VMEM((t, LANES), BF16),
            pltpu.VMEM((t, LANES), BF16),
            pltpu.VMEM((2 * tq, LANES), BF16),
            pltpu.VMEM((2 * tq, LANES), F32),
            pltpu.VMEM((2 * tq, LANES), F32),
            pltpu.VMEM((2 * tq, LANES), F32),
        ],
        compiler_params=_params("parallel", "arbitrary"),
        name="attn_prompt",
    )(lam, proj, proj, proj, gain, beta)


def _attn_decode_body(pt_ref, lam_ref, g_ref, beta_ref, bmask_ref, q_ref, kn_ref, vn_ref, *rest,
                      npages, page, nheads, ts, out_scale):
    del pt_ref
    kp_refs = rest[:npages]
    vp_refs = rest[npages:2 * npages]
    o_ref, ks_ref, vs_ref, kn_scr = rest[2 * npages:]
    past = npages * page
    tail, w = kn_scr.shape
    nrow = bmask_ref.shape[0]
    half = LANES // 2

    for p in range(npages):
        ks_ref[:, p * page:(p + 1) * page] = kp_refs[p][...].astype(BF16)
        for h in range(nheads):
            vs_ref[p * page:(p + 1) * page, h * LANES:(h + 1) * LANES] = (
                vp_refs[p][pl.ds(h, page, stride=nheads), :].astype(BF16))
    kn_scr[...] = jnp.zeros((tail, w), BF16)
    kn_scr[0:ts, :] = kn_ref[0].astype(BF16)
    vs_ref[past:, :] = jnp.zeros((tail, w), BF16)
    vs_ref[past:past + ts, :] = vn_ref[0].astype(BF16)

    q = q_ref[0] * (LOG2E / math.sqrt(half))
    qt = (jnp.tile(q, (nrow // ts, 1)) * bmask_ref[...]).astype(BF16)
    s_past = jnp.dot(qt, ks_ref[...], preferred_element_type=F32)
    s_new = lax.dot_general(qt, kn_scr[...], _NT, preferred_element_type=F32)
    row = lax.broadcasted_iota(jnp.int32, s_new.shape, 0)
    col = lax.broadcasted_iota(jnp.int32, s_new.shape, 1)
    s_new = jnp.where(col <= lax.rem(row, ts), s_new, MASK_VALUE)
    m = jnp.maximum(jnp.max(s_past, axis=1, keepdims=True), jnp.max(s_new, axis=1, keepdims=True))
    p_past = jnp.exp2(s_past - m)
    p_new = jnp.exp2(s_new - m)
    l = jnp.sum(p_past, axis=1, keepdims=True) + jnp.sum(p_new, axis=1, keepdims=True)
    row1 = lax.broadcasted_iota(jnp.int32, (nrow, 1), 0)
    hr = nrow // 2
    wgt = jnp.where(row1 < hr, 1.0, -lam_ref[:, 0:1]) / l
    pw_past = p_past * wgt
    pw_new = p_new * wgt
    a_past = (pw_past[0:hr, :] + pw_past[hr:, :]).astype(BF16)
    a_new = (pw_new[0:hr, :] + pw_new[hr:, :]).astype(BF16)
    o2 = (jnp.dot(a_past, vs_ref[0:past, :], preferred_element_type=F32)
          + jnp.dot(a_new, vs_ref[past:, :], preferred_element_type=F32))
    for h in range(nheads):
        cs = slice(h * LANES, (h + 1) * LANES)
        o = o2[h * 8:(h + 1) * 8, cs]
        ms = jnp.mean(o * o, axis=-1, keepdims=True)
        o = o * lax.rsqrt(ms + EPS) * g_ref[...] * out_scale * beta_ref[:, cs]
        o_ref[0, :, cs] = o[0:ts, :].astype(o_ref.dtype)


def _attn_decode(page_table, lam, gain, beta, bmask, proj3, cache_kt, cache_v2, *, nheads, out_scale):
    nb, ts, _ = proj3.shape
    npages = page_table.shape[1]
    w, page = cache_kt.shape[1], cache_kt.shape[2]
    nrow = bmask.shape[0]
    tail = LANES
    assert 8 % ts == 0 and w == nheads * LANES and cache_v2.shape[1:] == (page * nheads, LANES)

    def k_spec(p):
        return pl.BlockSpec((None, w, page), lambda b, pt: (pt[b, p], 0, 0))

    def v_spec(p):
        return pl.BlockSpec((None, page * nheads, LANES), lambda b, pt: (pt[b, p], 0, 0))

    grid_spec = pltpu.PrefetchScalarGridSpec(
        num_scalar_prefetch=1,
        grid=(nb,),
        in_specs=[
            pl.BlockSpec((1, LANES), lambda b, pt: (0, 0)),
            pl.BlockSpec((1, LANES), lambda b, pt: (0, 0)),
            pl.BlockSpec((1, w), lambda b, pt: (0, 0)),
            pl.BlockSpec((nrow, w), lambda b, pt: (0, 0)),
            pl.BlockSpec((1, ts, w), lambda b, pt: (b, 0, 0)),
            pl.BlockSpec((1, ts, w), lambda b, pt: (b, 0, 1)),
            pl.BlockSpec((1, ts, w), lambda b, pt: (b, 0, 2)),
        ] + [k_spec(p) for p in range(npages)] + [v_spec(p) for p in range(npages)],
        out_specs=pl.BlockSpec((1, ts, w), lambda b, pt: (b, 0, 0)),
        scratch_shapes=[
            pltpu.VMEM((w, npages * page), BF16),
            pltpu.VMEM((npages * page + tail, w), BF16),
            pltpu.VMEM((tail, w), BF16),
        ],
    )
    return pl.pallas_call(
        functools.partial(_attn_decode_body, npages=npages, page=page, nheads=nheads, ts=ts,
                          out_scale=out_scale),
        grid_spec=grid_spec,
        out_shape=jax.ShapeDtypeStruct((nb, ts, w), BF16),
        compiler_params=_params("arbitrary"),
        name="attn_decode",
    )(page_table, lam, gain, beta, bmask, proj3, proj3, proj3,
      *([cache_kt] * npages), *([cache_v2] * npages))


def _hgrn_body(rq_ref, rf_ref, ri_ref, rg_ref, lb_ref, gn_ref, beta_ref, tri_ref, *rest,
               nheads, dk, chunk, batched):
    if batched:
        s0_ref, rec_ref, sout_ref, st_ref, o_scr = rest
        rq, rf, ri, rg = rq_ref[0], rf_ref[0], ri_ref[0], rg_ref[0]
        for h in range(nheads):
            st_ref[h] = s0_ref[0, h].T
    else:
        rec_ref, sout_ref, st_ref, o_scr = rest
        rq, rf, ri, rg = rq_ref[...], rf_ref[...], ri_ref[...], rg_ref[...]

        @pl.when(pl.program_id(0) == 0)
        def _():
            st_ref[...] = jnp.zeros(st_ref.shape, F32)

    rows = rq.shape[0]
    lb = lb_ref[...]
    q = rq * jax.nn.sigmoid(rq)
    fg = lb + (1.0 - lb) * jax.nn.sigmoid(rf)
    logf = jnp.log(fg)
    kk = 1.0 - fg
    tri = tri_ref[...]
    h1, h2, h3 = _split3(logf)
    b = (jnp.dot(tri, h1, preferred_element_type=F32)
         + jnp.dot(tri, h2, preferred_element_type=F32)
         + jnp.dot(tri, h3, preferred_element_type=F32))
    q_in = (q * jnp.exp(b)).astype(BF16)
    k_in = (kk * jnp.exp(-b)).astype(BF16)
    v_bf = ri.astype(BF16)
    tr = lax.broadcasted_iota(jnp.int32, (chunk, chunk), 0)
    tc = lax.broadcasted_iota(jnp.int32, (chunk, chunk), 1)
    causal = tc <= tr
    for c in range(rows // chunk):
        rs = slice(c * chunk, (c + 1) * chunk)
        b_last = b[(c + 1) * chunk - 1:(c + 1) * chunk, :]
        k_st = (kk[rs, :] * jnp.exp(b_last - b[rs, :])).astype(BF16)
        decay = jnp.exp(b_last)
        for h in range(nheads):
            cs = slice(h * dk, (h + 1) * dk)
            qc, kc, vc = q_in[rs, cs], k_in[rs, cs], v_bf[rs, cs]
            att = lax.dot_general(qc, kc, _NT, preferred_element_type=F32)
            att = jnp.where(causal, att, 0.0).astype(BF16)
            st = st_ref[h]
            o_scr[rs, cs] = (jnp.dot(att, vc, preferred_element_type=F32)
                             + lax.dot_general(qc, st.astype(BF16), _NT, preferred_element_type=F32))
            st_ref[h] = st * decay[:, cs] + lax.dot_general(vc, k_st[:, cs], _TA,
                                                            preferred_element_type=F32)
    gate = rg * jax.nn.sigmoid(rg)
    for h in range(nheads):
        cs = slice(h * dk, (h + 1) * dk)
        o = o_scr[:, cs]
        ms = jnp.mean(o * o, axis=-1, keepdims=True)
        o = (o * lax.rsqrt(ms + EPS) * gn_ref[...]) * gate[:, cs] * beta_ref[:, cs]
        if batched:
            rec_ref[0, :, cs] = o.astype(rec_ref.dtype)
        else:
            rec_ref[:, cs] = o.astype(rec_ref.dtype)

    if batched:
        for h in range(nheads):
            sout_ref[0, h] = st_ref[h].T
    else:
        @pl.when(pl.program_id(0) == pl.num_programs(0) - 1)
        def _():
            for h in range(nheads):
                sout_ref[h] = st_ref[h].T


def _tri(rows, chunk):
    r = jnp.arange(rows)
    return ((r[:, None] >= r[None, :]) & (r[:, None] // chunk == r[None, :] // chunk)).astype(BF16)


def _hgrn_prompt(proj, lb, gain, beta, *, nheads, dk, dv):
    t = proj.shape[0]
    gw = nheads * dk
    chunk = math.gcd(t, REC_CHUNK)
    rows = _pick(t, 4 * chunk)
    assert dk == dv == LANES
    specs = [pl.BlockSpec((rows, gw), functools.partial(lambda i, g: (i, g), g=3 + g)) for g in range(4)]
    return pl.pallas_call(
        functools.partial(_hgrn_body, nheads=nheads, dk=dk, chunk=chunk, batched=False),
        grid=(t // rows,),
        in_specs=specs + [
            pl.BlockSpec((1, gw), lambda i: (0, 0)),
            pl.BlockSpec((1, dv), lambda i: (0, 0)),
            pl.BlockSpec((1, gw), lambda i: (0, 0)),
            pl.BlockSpec((rows, rows), lambda i: (0, 0)),
        ],
        out_specs=(pl.BlockSpec((rows, gw), lambda i: (i, 0)),
                   pl.BlockSpec((nheads, dk, dv), lambda i: (0, 0, 0))),
        out_shape=(jax.ShapeDtypeStruct((t, gw), BF16),
                   jax.ShapeDtypeStruct((nheads, dk, dv), F32)),
        scratch_shapes=[pltpu.VMEM((nheads, dv, dk), F32), pltpu.VMEM((rows, gw), F32)],
        compiler_params=_params("arbitrary"),
        name="hgrn_prompt",
    )(proj, proj, proj, proj, lb, gain, beta, _tri(rows, chunk))


def _hgrn_decode(proj3, lb, gain, beta, s0, *, nheads, dk, dv):
    nb, ts, _ = proj3.shape
    gw = nheads * dk
    chunk = math.gcd(ts, REC_CHUNK)
    assert dk == dv == LANES and chunk == ts
    specs = [pl.BlockSpec((1, ts, gw), functools.partial(lambda b, g: (b, 0, g), g=3 + g)) for g in range(4)]
    return pl.pallas_call(
        functools.partial(_hgrn_body, nheads=nheads, dk=dk, chunk=chunk, batched=True),
        grid=(nb,),
        in_specs=specs + [
            pl.BlockSpec((1, gw), lambda b: (0, 0)),
            pl.BlockSpec((1, dv), lambda b: (0, 0)),
            pl.BlockSpec((1, gw), lambda b: (0, 0)),
            pl.BlockSpec((ts, ts), lambda b: (0, 0)),
            pl.BlockSpec((1, nheads, dk, dv), lambda b: (b, 0, 0, 0)),
        ],
        out_specs=(pl.BlockSpec((1, ts, gw), lambda b: (b, 0, 0)),
                   pl.BlockSpec((1, nheads, dk, dv), lambda b: (b, 0, 0, 0))),
        out_shape=(jax.ShapeDtypeStruct((nb, ts, gw), BF16),
                   jax.ShapeDtypeStruct((nb, nheads, dk, dv), F32)),
        scratch_shapes=[pltpu.VMEM((nheads, dv, dk), F32), pltpu.VMEM((ts, gw), F32)],
        compiler_params=_params("parallel"),
        name="hgrn_decode",
    )(proj3, proj3, proj3, proj3, lb, gain, beta, _tri(ts, chunk), s0)


def _finish_body(x_ref, att_ref, rec_ref, wo_ref, g_ref, wu_ref, wd_ref, o_ref, hn_ref):
    j = pl.program_id(1)

    @pl.when(j == 0)
    def _():
        aw = att_ref.shape[1]
        h = (x_ref[...]
             + jnp.dot(att_ref[...], wo_ref[0:aw, :], preferred_element_type=F32)
             + jnp.dot(rec_ref[...], wo_ref[aw:, :], preferred_element_type=F32))
        ms = jnp.mean(h * h, axis=-1, keepdims=True)
        hn_ref[...] = (h * lax.rsqrt(ms + EPS) * g_ref[...]).astype(BF16)
        o_ref[...] = h

    u = jnp.maximum(jnp.dot(hn_ref[...], wu_ref[...], preferred_element_type=F32), 0.0)
    o_ref[...] += jnp.dot((u * u).astype(BF16), wd_ref[...], preferred_element_type=F32)


def _finish(x, att, rec, wo_bf, ffn_g, wu_bf, wd_bf):
    m, d = x.shape
    aw, rw = att.shape[1], rec.shape[1]
    ff = wu_bf.shape[1]
    tm = _pick(m, 512)
    fc = _pick(ff, 512)
    return pl.pallas_call(
        _finish_body,
        grid=(m // tm, ff // fc),
        in_specs=[
            pl.BlockSpec((tm, d), lambda i, j: (i, 0)),
            pl.BlockSpec((tm, aw), lambda i, j: (i, 0)),
            pl.BlockSpec((tm, rw), lambda i, j: (i, 0)),
            pl.BlockSpec((aw + rw, d), lambda i, j: (0, 0)),
            pl.BlockSpec((1, d), lambda i, j: (0, 0)),
            pl.BlockSpec((d, fc), lambda i, j: (0, j)),
            pl.BlockSpec((fc, d), lambda i, j: (j, 0)),
        ],
        out_specs=pl.BlockSpec((tm, d), lambda i, j: (i, 0)),
        out_shape=jax.ShapeDtypeStruct((m, d), F32),
        scratch_shapes=[pltpu.VMEM((tm, d), BF16)],
        compiler_params=_params("parallel", "arbitrary"),
        name="finish",
    )(x, att, rec, wo_bf, ffn_g, wu_bf, wd_bf)


def _rope_tables(pos, head_dim):
    rot = head_dim // 4
    half = rot // 2
    inv = ROPE_THETA ** (-jnp.arange(half, dtype=F32) * 2.0 / rot)
    ang = pos.astype(F32)[:, None] * inv[None, :]
    cos, sin = jnp.cos(ang), jnp.sin(ang)
    ones = jnp.ones((pos.shape[0], head_dim - rot), F32)
    zeros = jnp.zeros((pos.shape[0], head_dim - rot), F32)
    zh = jnp.zeros_like(sin)
    c = jnp.concatenate([cos, cos, ones], axis=1)
    sa = jnp.concatenate([-sin, zh, zeros], axis=1)
    sb = jnp.concatenate([zh, sin, zeros], axis=1)
    reps = LANES // head_dim
    return tuple(jnp.tile(a, (1, reps)) for a in (c, sa, sb))


def kernel(x_prompt, x_sample, cache_k, cache_v, state_rec, page_table, attn_norm, w_in, q_norm, k_norm,
           lambda_q1, lambda_k1, lambda_q2, lambda_k2, att_out_norm, rec_lb_logits, rec_out_norm,
           beta_att, beta_rec, w_out, ffn_norm, w_up, w_down):
    bp, tp, d = x_prompt.shape
    bs, ts, _ = x_sample.shape
    depth = w_in.shape[0]
    n_pool, page, nheads, _, head_dim = cache_k.shape[1:]
    vd = cache_v.shape[-1]
    nrec, dk, dv = state_rec.shape[2:]
    npages = page_table.shape[1]
    past = npages * page
    aw = nheads * vd
    gw = nrec * dk
    assert bp == 1 and 2 * head_dim == vd == LANES and aw == gw and w_in.shape[2] == 7 * gw

    lam_inits = tuple(0.8 - 0.6 * math.exp(-0.3 * l) for l in range(depth))
    lam_all, lb_all = _prep(lambda_q1, lambda_k1, lambda_q2, lambda_k2, rec_lb_logits, lam_inits)

    lane_map = jnp.arange(MXU_TILE) // head_dim
    gmat = (lane_map[:, None] == lane_map[None, :]).astype(BF16)
    rope_p = _rope_tables(jnp.arange(tp), head_dim)
    rope_s = tuple(jnp.tile(a, (bs, 1)) for a in _rope_tables(past + jnp.arange(ts), head_dim))
    srow = jnp.arange(2 * nheads * 8)
    bmask = ((srow[:, None] // (nheads * 8) == (jnp.arange(aw)[None, :] // head_dim) % 2)
             & ((srow[:, None] // 8) % nheads == jnp.arange(aw)[None, :] // vd)).astype(F32)

    hp = x_prompt.reshape(tp, d)
    hs = x_sample.reshape(bs * ts, d)
    outs = [[] for _ in range(6)]
    for l in range(depth):
        w_in_bf = w_in[l].astype(BF16)
        w_out_bf = w_out[l].astype(BF16)
        w_up_bf = w_up[l].astype(BF16)
        w_down_bf = w_down[l].astype(BF16)
        qk_gain = jnp.stack([jnp.tile(q_norm[l], aw // head_dim),
                             jnp.tile(k_norm[l], aw // head_dim)]).reshape(2, 1, aw)
        norm_g = attn_norm[l].reshape(1, d)
        lam = lam_all[l:l + 1]
        lb = lb_all[l:l + 1]
        att_g = att_out_norm[l].reshape(1, vd)
        rec_g = rec_out_norm[l].reshape(1, dv)
        b_att = beta_att[l].reshape(1, aw)
        b_rec = beta_rec[l].reshape(1, gw)
        ffn_g = ffn_norm[l].reshape(1, d)
        out_scale = 1.0 - lam_inits[l]

        proj = _proj(hp, norm_g, w_in_bf, qk_gain, gmat, *rope_p, head_dim=head_dim)
        att = _attn_prompt(proj, lam, att_g, b_att, nheads=nheads, out_scale=out_scale)
        rec, s_new = _hgrn_prompt(proj, lb, rec_g, b_rec, nheads=nrec, dk=dk, dv=dv)
        hp = _finish(hp, att, rec, w_out_bf, ffn_g, w_up_bf, w_down_bf)
        outs[0].append(proj[:, aw:2 * aw].reshape(bp, tp, nheads, 2, head_dim))
        outs[1].append(proj[:, 2 * aw:3 * aw].reshape(bp, tp, nheads, vd))
        outs[2].append(s_new.reshape(bp, nrec, dk, dv).astype(state_rec.dtype))

        proj_s = _proj(hs, norm_g, w_in_bf, qk_gain, gmat, *rope_s, head_dim=head_dim)
        proj3 = proj_s.reshape(bs, ts, 7 * gw)
        cache_kt = jnp.transpose(cache_k[l], (0, 2, 3, 4, 1)).reshape(n_pool, aw, page)
        cache_v2 = cache_v[l].reshape(n_pool, page * nheads, vd)
        att_s = _attn_decode(page_table, lam, att_g, b_att, bmask, proj3, cache_kt, cache_v2,
                             nheads=nheads, out_scale=out_scale)
        rec_s, s_new = _hgrn_decode(proj3, lb, rec_g, b_rec, state_rec[l], nheads=nrec, dk=dk, dv=dv)
        hs = _finish(hs, att_s.reshape(bs * ts, aw), rec_s.reshape(bs * ts, gw),
                     w_out_bf, ffn_g, w_up_bf, w_down_bf)
        outs[3].append(proj_s[:, aw:2 * aw].reshape(bs, ts, nheads, 2, head_dim))
        outs[4].append(proj_s[:, 2 * aw:3 * aw].reshape(bs, ts, nheads, vd))
        outs[5].append(s_new.astype(state_rec.dtype))

    return (hp.reshape(bp, tp, d), hs.reshape(bs, ts, d),
            jnp.stack(outs[0]), jnp.stack(outs[1]), jnp.stack(outs[2]),
            jnp.stack(outs[3]), jnp.stack(outs[4]), jnp.stack(outs[5]))
```

```python
import functools
import math

import jax
import jax.numpy as jnp
from jax import lax
from jax.experimental import pallas as pl
from jax.experimental.pallas import tpu as pltpu

F32 = jnp.float32
BF16 = jnp.bfloat16

EPS = 1e-6
ROPE_THETA = 500000.0
MASK_VALUE = -1e30
LANES = 128
MXU_TILE = 256
LOG2E = 1.4426950408889634
VMEM_LIMIT_BYTES = 56 * 1024 * 1024
REC_CHUNK = 64

_NT = (((1,), (1,)), ((), ()))
_TA = (((0,), (0,)), ((), ()))


def _params(*semantics):
    return pltpu.CompilerParams(dimension_semantics=semantics, vmem_limit_bytes=VMEM_LIMIT_BYTES)


def _pick(n, pref):
    if n <= pref:
        return n
    t = pref
    while n % t:
        t //= 2
    return t


def _split3(x):
    h1 = x.astype(BF16)
    r1 = x - h1.astype(F32)
    h2 = r1.astype(BF16)
    h3 = (r1 - h2.astype(F32)).astype(BF16)
    return h1, h2, h3


def _prep_body(lq1_ref, lk1_ref, lq2_ref, lk2_ref, logit_ref, lam_ref, lb_ref, *, lam_inits):
    for l, lam_init in enumerate(lam_inits):
        s1 = jnp.sum(lq1_ref[l:l + 1, :] * lk1_ref[l:l + 1, :], axis=1, keepdims=True)
        s2 = jnp.sum(lq2_ref[l:l + 1, :] * lk2_ref[l:l + 1, :], axis=1, keepdims=True)
        lam = jnp.exp(s1) - jnp.exp(s2) + lam_init
        lam_ref[l:l + 1, :] = jnp.broadcast_to(lam, (1, LANES))
    x = logit_ref[...]
    e = jnp.exp(x - jnp.max(x, axis=0, keepdims=True))
    sm = e / jnp.sum(e, axis=0, keepdims=True)
    acc = jnp.zeros((1, x.shape[1]), F32)
    for r in range(x.shape[0]):
        acc = acc + sm[r:r + 1, :]
        lb_ref[r:r + 1, :] = acc


def _prep(lq1, lk1, lq2, lk2, logits, lam_inits):
    depth = lq1.shape[0]
    return pl.pallas_call(
        functools.partial(_prep_body, lam_inits=lam_inits),
        out_shape=(jax.ShapeDtypeStruct((depth, LANES), F32),
                   jax.ShapeDtypeStruct(logits.shape, F32)),
        name="prep",
    )(lq1, lk1, lq2, lk2, logits)


def _proj_body(x_ref, g_ref, w_ref, qkg_ref, gmat_ref, cos_ref, sa_ref, sb_ref, o_ref, xn_ref,
               *, head_dim, rot_half):
    j = pl.program_id(1)

    @pl.when(j == 0)
    def _():
        x = x_ref[...]
        ms = jnp.mean(x * x, axis=-1, keepdims=True)
        xn_ref[...] = (x * lax.rsqrt(ms + EPS) * g_ref[...]).astype(BF16)

    y = jnp.dot(xn_ref[...], w_ref[...], preferred_element_type=F32)

    @pl.when(j < 2)
    def _():
        gm = gmat_ref[...]
        gt = gm.shape[0]
        gain = qkg_ref[0]
        c, sa, sb = cos_ref[...], sa_ref[...], sb_ref[...]
        for t in range(y.shape[1] // gt):
            yt = y[:, t * gt:(t + 1) * gt]
            y2 = yt * yt
            hi = y2.astype(BF16)
            lo = (y2 - hi.astype(F32)).astype(BF16)
            ss = (jnp.dot(hi, gm, preferred_element_type=F32)
                  + jnp.dot(lo, gm, preferred_element_type=F32))
            yn = yt * lax.rsqrt(ss * (1.0 / head_dim) + EPS) * gain[:, t * gt:(t + 1) * gt]
            for u in range(gt // LANES):
                yu = yn[:, u * LANES:(u + 1) * LANES]
                lane0 = t * gt + u * LANES
                o_ref[:, lane0:lane0 + LANES] = (
                    yu * c + pltpu.roll(yu, LANES - rot_half, 1) * sa + pltpu.roll(yu, rot_half, 1) * sb)

    @pl.when(j >= 2)
    def _():
        o_ref[...] = y


def _proj(x, norm_g, w_bf, qk_gain, gmat, cos_t, sa_t, sb_t, *, head_dim):
    m, d = x.shape
    gw = qk_gain.shape[2]
    gt = gmat.shape[0]
    ngroups = w_bf.shape[1] // gw
    tm = _pick(m, 1024)
    assert gw % gt == 0 and gt % LANES == 0
    return pl.pallas_call(
        functools.partial(_proj_body, head_dim=head_dim, rot_half=head_dim // 8),
        grid=(m // tm, ngroups),
        in_specs=[
            pl.BlockSpec((tm, d), lambda i, j: (i, 0)),
            pl.BlockSpec((1, d), lambda i, j: (0, 0)),
            pl.BlockSpec((d, gw), lambda i, j: (0, j)),
            pl.BlockSpec((1, 1, gw), lambda i, j: (jnp.minimum(j, 1), 0, 0)),
            pl.BlockSpec((gt, gt), lambda i, j: (0, 0)),
            pl.BlockSpec((tm, LANES), lambda i, j: (i, 0)),
            pl.BlockSpec((tm, LANES), lambda i, j: (i, 0)),
            pl.BlockSpec((tm, LANES), lambda i, j: (i, 0)),
        ],
        out_specs=pl.BlockSpec((tm, gw), lambda i, j: (i, j)),
        out_shape=jax.ShapeDtypeStruct((m, ngroups * gw), F32),
        scratch_shapes=[pltpu.VMEM((tm, d), BF16)],
        compiler_params=_params("parallel", "arbitrary"),
        name="proj",
    )(x, norm_g, w_bf, qk_gain, gmat, cos_t, sa_t, sb_t)


def _attn_body(lam_ref, q_ref, k_ref, v_ref, g_ref, beta_ref, o_ref,
               kb_ref, vb_ref, qs_ref, m_ref, l_ref, acc_ref, *, tq, wide, out_scale):
    qi = pl.program_id(1)
    half = LANES // 2

    @pl.when(qi == 0)
    def _():
        kb_ref[...] = k_ref[...].astype(BF16)
        vb_ref[...] = v_ref[...].astype(BF16)

    q = q_ref[...] * (LOG2E / math.sqrt(half))
    lane = lax.broadcasted_iota(jnp.int32, (1, LANES), 1)
    qs_ref[0:tq, :] = jnp.where(lane < half, q, 0.0).astype(BF16)
    qs_ref[tq:2 * tq, :] = jnp.where(lane >= half, q, 0.0).astype(BF16)
    m_ref[...] = jnp.full(m_ref.shape, MASK_VALUE, F32)
    l_ref[...] = jnp.zeros(l_ref.shape, F32)
    acc_ref[...] = jnp.zeros(acc_ref.shape, F32)

    def block(off, width, masked):
        kblk = kb_ref[pl.ds(off, width), :]
        vblk = vb_ref[pl.ds(off, width), :]
        for c in range(2):
            rs = slice(c * tq, (c + 1) * tq)
            s = lax.dot_general(qs_ref[rs, :], kblk, _NT, preferred_element_type=F32)
            if masked:
                row = lax.broadcasted_iota(jnp.int32, s.shape, 0)
                col = lax.broadcasted_iota(jnp.int32, s.shape, 1)
                s = jnp.where(col <= row, s, MASK_VALUE)
            m_prev = m_ref[rs, :]
            m_next = jnp.maximum(m_prev, jnp.max(s, axis=1, keepdims=True))
            alpha = jnp.exp2(m_prev - m_next)
            p = jnp.exp2(s - jnp.tile(m_next, (1, width // LANES)))
            l_ref[rs, :] = alpha * l_ref[rs, :] + jnp.sum(p, axis=1, keepdims=True)
            acc_ref[rs, :] = alpha * acc_ref[rs, :] + jnp.dot(
                p.astype(BF16), vblk, preferred_element_type=F32)
            m_ref[rs, :] = m_next

    per_wide = wide // tq
    nwide = qi // per_wide

    def wide_body(j, carry):
        block(pl.multiple_of(j * wide, wide), wide, False)
        return carry

    def narrow_body(j, carry):
        block(pl.multiple_of((nwide * per_wide + j) * tq, tq), tq, False)
        return carry

    lax.fori_loop(0, nwide, wide_body, 0)
    lax.fori_loop(0, qi - nwide * per_wide, narrow_body, 0)
    block(pl.multiple_of(qi * tq, tq), tq, True)

    a = acc_ref[...] / l_ref[...]
    o = a[0:tq, :] - lam_ref[...] * a[tq:2 * tq, :]
    ms = jnp.mean(o * o, axis=-1, keepdims=True)
    o_ref[...] = (o * lax.rsqrt(ms + EPS) * g_ref[...] * out_scale * beta_ref[...]).astype(o_ref.dtype)


def _attn_prompt(proj, lam, gain, beta, *, nheads, out_scale):
    t = proj.shape[0]
    tq = _pick(t, 512)
    wide = _pick(t, 4 * tq)
    assert tq % LANES == 0 and wide % tq == 0
    return pl.pallas_call(
        functools.partial(_attn_body, tq=tq, wide=wide, out_scale=out_scale),
        grid=(nheads, t // tq),
        in_specs=[
            pl.BlockSpec((1, LANES), lambda h, i: (0, 0)),
            pl.BlockSpec((tq, LANES), lambda h, i: (i, h)),
            pl.BlockSpec((t, LANES), lambda h, i: (0, nheads + h)),
            pl.BlockSpec((t, LANES), lambda h, i: (0, 2 * nheads + h)),
            pl.BlockSpec((1, LANES), lambda h, i: (0, 0)),
            pl.BlockSpec((1, LANES), lambda h, i: (0, h)),
        ],
        out_specs=pl.BlockSpec((tq, LANES), lambda h, i: (i, h)),
        out_shape=jax.ShapeDtypeStruct((t, nheads * LANES), BF16),
        scratch_shapes=[
            pltpu.VMEM((t, LANES), BF16),
            pltpu.VMEM((t, LANES), BF16),
            pltpu.VMEM((2 * tq, LANES), BF16),
            pltpu.VMEM((2 * tq, LANES), F32),
            pltpu.VMEM((2 * tq, LANES), F32),
            pltpu.VMEM((2 * tq, LANES), F32),
        ],
        compiler_params=_params("parallel", "arbitrary"),
        name="attn_prompt",
    )(lam, proj, proj, proj, gain, beta)


def _attn_decode_body(pt_ref, lam_ref, g_ref, beta_ref, bmask_ref, q_ref, kn_ref, vn_ref, *rest,
                      npages, page, nheads, ts, out_scale):
    del pt_ref
    kp_refs = rest[:npages]
    vp_refs = rest[npages:2 * npages]
    o_ref, ks_ref, vs_ref, kn_scr = rest[2 * npages:]
    past = npages * page
    tail, w = kn_scr.shape
    nrow = bmask_ref.shape[0]
    half = LANES // 2

    for p in range(npages):
        ks_ref[:, p * page:(p + 1) * page] = kp_refs[p][...].astype(BF16)
        for h in range(nheads):
            vs_ref[p * page:(p + 1) * page, h * LANES:(h + 1) * LANES] = (
                vp_refs[p][pl.ds(h, page, stride=nheads), :].astype(BF16))
    kn_scr[...] = jnp.zeros((tail, w), BF16)
    kn_scr[0:ts, :] = kn_ref[0].astype(BF16)
    vs_ref[past:, :] = jnp.zeros((tail, w), BF16)
    vs_ref[past:past + ts, :] = vn_ref[0].astype(BF16)

    q = q_ref[0] * (LOG2E / math.sqrt(half))
    qt = (jnp.tile(q, (nrow // ts, 1)) * bmask_ref[...]).astype(BF16)
    s_past = jnp.dot(qt, ks_ref[...], preferred_element_type=F32)
    s_new = lax.dot_general(qt, kn_scr[...], _NT, preferred_element_type=F32)
    row = lax.broadcasted_iota(jnp.int32, s_new.shape, 0)
    col = lax.broadcasted_iota(jnp.int32, s_new.shape, 1)
    s_new = jnp.where(col <= lax.rem(row, ts), s_new, MASK_VALUE)
    m = jnp.maximum(jnp.max(s_past, axis=1, keepdims=True), jnp.max(s_new, axis=1, keepdims=True))
    p_past = jnp.exp2(s_past - m)
    p_new = jnp.exp2(s_new - m)
    l = jnp.sum(p_past, axis=1, keepdims=True) + jnp.sum(p_new, axis=1, keepdims=True)
    row1 = lax.broadcasted_iota(jnp.int32, (nrow, 1), 0)
    hr = nrow // 2
    wgt = jnp.where(row1 < hr, 1.0, -lam_ref[:, 0:1]) / l
    pw_past = p_past * wgt
    pw_new = p_new * wgt
    a_past = (pw_past[0:hr, :] + pw_past[hr:, :]).astype(BF16)
    a_new = (pw_new[0:hr, :] + pw_new[hr:, :]).astype(BF16)
    o2 = (jnp.dot(a_past, vs_ref[0:past, :], preferred_element_type=F32)
          + jnp.dot(a_new, vs_ref[past:, :], preferred_element_type=F32))
    for h in range(nheads):
        cs = slice(h * LANES, (h + 1) * LANES)
        o = o2[h * 8:(h + 1) * 8, cs]
        ms = jnp.mean(o * o, axis=-1, keepdims=True)
        o = o * lax.rsqrt(ms + EPS) * g_ref[...] * out_scale * beta_ref[:, cs]
        o_ref[0, :, cs] = o[0:ts, :].astype(o_ref.dtype)


def _attn_decode(page_table, lam, gain, beta, bmask, proj3, cache_kt, cache_v2, *, nheads, out_scale):
    nb, ts, _ = proj3.shape
    npages = page_table.shape[1]
    w, page = cache_kt.shape[1], cache_kt.shape[2]
    nrow = bmask.shape[0]
    tail = LANES
    assert 8 % ts == 0 and w == nheads * LANES and cache_v2.shape[1:] == (page * nheads, LANES)

    def k_spec(p):
        return pl.BlockSpec((None, w, page), lambda b, pt: (pt[b, p], 0, 0))

    def v_spec(p):
        return pl.BlockSpec((None, page * nheads, LANES), lambda b, pt: (pt[b, p], 0, 0))

    grid_spec = pltpu.PrefetchScalarGridSpec(
        num_scalar_prefetch=1,
        grid=(nb,),
        in_specs=[
            pl.BlockSpec((1, LANES), lambda b, pt: (0, 0)),
            pl.BlockSpec((1, LANES), lambda b, pt: (0, 0)),
            pl.BlockSpec((1, w), lambda b, pt: (0, 0)),
            pl.BlockSpec((nrow, w), lambda b, pt: (0, 0)),
            pl.BlockSpec((1, ts, w), lambda b, pt: (b, 0, 0)),
            pl.BlockSpec((1, ts, w), lambda b, pt: (b, 0, 1)),
            pl.BlockSpec((1, ts, w), lambda b, pt: (b, 0, 2)),
        ] + [k_spec(p) for p in range(npages)] + [v_spec(p) for p in range(npages)],
        out_specs=pl.BlockSpec((1, ts, w), lambda b, pt: (b, 0, 0)),
        scratch_shapes=[
            pltpu.VMEM((w, npages * page), BF16),
            pltpu.VMEM((npages * page + tail, w), BF16),
            pltpu.VMEM((tail, w), BF16),
        ],
    )
    return pl.pallas_call(
        functools.partial(_attn_decode_body, npages=npages, page=page, nheads=nheads, ts=ts,
                          out_scale=out_scale),
        grid_spec=grid_spec,
        out_shape=jax.ShapeDtypeStruct((nb, ts, w), BF16),
        compiler_params=_params("arbitrary"),
        name="attn_decode",
    )(page_table, lam, gain, beta, bmask, proj3, proj3, proj3,
      *([cache_kt] * npages), *([cache_v2] * npages))


def _hgrn_body(rq_ref, rf_ref, ri_ref, rg_ref, lb_ref, gn_ref, beta_ref, tri_ref, *rest,
               nheads, dk, chunk, batched):
    if batched:
        s0_ref, rec_ref, sout_ref, st_ref, o_scr = rest
        nseq = rq_ref.shape[0]
    else:
        rec_ref, sout_ref, st_ref, o_scr = rest
        nseq = 1

        @pl.when(pl.program_id(0) == 0)
        def _():
            st_ref[...] = jnp.zeros(st_ref.shape, F32)

    lb = lb_ref[...]
    tri = tri_ref[...]
    tr = lax.broadcasted_iota(jnp.int32, (chunk, chunk), 0)
    tc = lax.broadcasted_iota(jnp.int32, (chunk, chunk), 1)
    causal = tc <= tr
    for bi in range(nseq):
        sb = bi * nheads
        if batched:
            rq, rf, ri, rg = rq_ref[bi], rf_ref[bi], ri_ref[bi], rg_ref[bi]
            for h in range(nheads):
                st_ref[sb + h] = s0_ref[bi, h].T
        else:
            rq, rf, ri, rg = rq_ref[...], rf_ref[...], ri_ref[...], rg_ref[...]
        rows = rq.shape[0]
        q = rq * jax.nn.sigmoid(rq)
        fg = lb + (1.0 - lb) * jax.nn.sigmoid(rf)
        logf = jnp.log(fg)
        kk = 1.0 - fg
        h1, h2, h3 = _split3(logf)
        b = (jnp.dot(tri, h1, preferred_element_type=F32)
             + jnp.dot(tri, h2, preferred_element_type=F32)
             + jnp.dot(tri, h3, preferred_element_type=F32))
        q_in = (q * jnp.exp(b)).astype(BF16)
        k_in = (kk * jnp.exp(-b)).astype(BF16)
        v_bf = ri.astype(BF16)
        for c in range(rows // chunk):
            rs = slice(c * chunk, (c + 1) * chunk)
            b_last = b[(c + 1) * chunk - 1:(c + 1) * chunk, :]
            k_st = (kk[rs, :] * jnp.exp(b_last - b[rs, :])).astype(BF16)
            decay = jnp.exp(b_last)
            for h in range(nheads):
                cs = slice(h * dk, (h + 1) * dk)
                qc, kc, vc = q_in[rs, cs], k_in[rs, cs], v_bf[rs, cs]
                att = lax.dot_general(qc, kc, _NT, preferred_element_type=F32)
                att = jnp.where(causal, att, 0.0).astype(BF16)
                st = st_ref[sb + h]
                o_scr[bi, rs, cs] = (
                    jnp.dot(att, vc, preferred_element_type=F32)
                    + lax.dot_general(qc, st.astype(BF16), _NT, preferred_element_type=F32))
                st_ref[sb + h] = st * decay[:, cs] + lax.dot_general(
                    vc, k_st[:, cs], _TA, preferred_element_type=F32)
        gate = rg * jax.nn.sigmoid(rg)
        for h in range(nheads):
            cs = slice(h * dk, (h + 1) * dk)
            o = o_scr[bi, :, cs]
            ms = jnp.mean(o * o, axis=-1, keepdims=True)
            o = (o * lax.rsqrt(ms + EPS) * gn_ref[...]) * gate[:, cs] * beta_ref[:, cs]
            if batched:
                rec_ref[bi, :, cs] = o.astype(rec_ref.dtype)
                sout_ref[bi, h] = st_ref[sb + h].T
            else:
                rec_ref[:, cs] = o.astype(rec_ref.dtype)

    if not batched:
        @pl.when(pl.program_id(0) == pl.num_programs(0) - 1)
        def _():
            for h in range(nheads):
                sout_ref[h] = st_ref[h].T


def _tri(rows, chunk):
    r = jnp.arange(rows)
    return ((r[:, None] >= r[None, :]) & (r[:, None] // chunk == r[None, :] // chunk)).astype(BF16)


def _hgrn_prompt(proj, lb, gain, beta, *, nheads, dk, dv):
    t = proj.shape[0]
    gw = nheads * dk
    chunk = math.gcd(t, REC_CHUNK)
    rows = _pick(t, 4 * chunk)
    assert dk == dv == LANES
    specs = [pl.BlockSpec((rows, gw), functools.partial(lambda i, g: (i, g), g=3 + g)) for g in range(4)]
    return pl.pallas_call(
        functools.partial(_hgrn_body, nheads=nheads, dk=dk, chunk=chunk, batched=False),
        grid=(t // rows,),
        in_specs=specs + [
            pl.BlockSpec((1, gw), lambda i: (0, 0)),
            pl.BlockSpec((1, dv), lambda i: (0, 0)),
            pl.BlockSpec((1, gw), lambda i: (0, 0)),
            pl.BlockSpec((rows, rows), lambda i: (0, 0)),
        ],
        out_specs=(pl.BlockSpec((rows, gw), lambda i: (i, 0)),
                   pl.BlockSpec((nheads, dk, dv), lambda i: (0, 0, 0))),
        out_shape=(jax.ShapeDtypeStruct((t, gw), BF16),
                   jax.ShapeDtypeStruct((nheads, dk, dv), F32)),
        scratch_shapes=[pltpu.VMEM((nheads, dv, dk), F32), pltpu.VMEM((1, rows, gw), F32)],
        compiler_params=_params("arbitrary"),
        name="hgrn_prompt",
    )(proj, proj, proj, proj, lb, gain, beta, _tri(rows, chunk))


def _hgrn_decode(proj3, lb, gain, beta, s0, *, nheads, dk, dv):
    nb, ts, _ = proj3.shape
    gw = nheads * dk
    chunk = math.gcd(ts, REC_CHUNK)
    assert dk == dv == LANES and chunk == ts
    nseq = _pick(nb, 4)
    specs = [pl.BlockSpec((nseq, ts, gw), functools.partial(lambda b, g: (b, 0, g), g=3 + g)) for g in range(4)]
    return pl.pallas_call(
        functools.partial(_hgrn_body, nheads=nheads, dk=dk, chunk=chunk, batched=True),
        grid=(nb // nseq,),
        in_specs=specs + [
            pl.BlockSpec((1, gw), lambda b: (0, 0)),
            pl.BlockSpec((1, dv), lambda b: (0, 0)),
            pl.BlockSpec((1, gw), lambda b: (0, 0)),
            pl.BlockSpec((ts, ts), lambda b: (0, 0)),
            pl.BlockSpec((nseq, nheads, dk, dv), lambda b: (b, 0, 0, 0)),
        ],
        out_specs=(pl.BlockSpec((nseq, ts, gw), lambda b: (b, 0, 0)),
                   pl.BlockSpec((nseq, nheads, dk, dv), lambda b: (b, 0, 0, 0))),
        out_shape=(jax.ShapeDtypeStruct((nb, ts, gw), BF16),
                   jax.ShapeDtypeStruct((nb, nheads, dk, dv), F32)),
        scratch_shapes=[pltpu.VMEM((nseq * nheads, dv, dk), F32), pltpu.VMEM((nseq, ts, gw), F32)],
        compiler_params=_params("parallel"),
        name="hgrn_decode",
    )(proj3, proj3, proj3, proj3, lb, gain, beta, _tri(ts, chunk), s0)


def _finish_body(x_ref, att_ref, rec_ref, wo_ref, g_ref, wu_ref, wd_ref, o_ref, hn_ref):
    j = pl.program_id(1)

    @pl.when(j == 0)
    def _():
        aw = att_ref.shape[1]
        h = (x_ref[...]
             + jnp.dot(att_ref[...], wo_ref[0:aw, :], preferred_element_type=F32)
             + jnp.dot(rec_ref[...], wo_ref[aw:, :], preferred_element_type=F32))
        ms = jnp.mean(h * h, axis=-1, keepdims=True)
        hn_ref[...] = (h * lax.rsqrt(ms + EPS) * g_ref[...]).astype(BF16)
        o_ref[...] = h

    u = jnp.maximum(jnp.dot(hn_ref[...], wu_ref[...], preferred_element_type=F32), 0.0)
    o_ref[...] += jnp.dot((u * u).astype(BF16), wd_ref[...], preferred_element_type=F32)


def _finish(x, att, rec, wo_bf, ffn_g, wu_bf, wd_bf):
    m, d = x.shape
    aw, rw = att.shape[1], rec.shape[1]
    ff = wu_bf.shape[1]
    tm = _pick(m, 512)
    fc = _pick(ff, 1024)
    return pl.pallas_call(
        _finish_body,
        grid=(m // tm, ff // fc),
        in_specs=[
            pl.BlockSpec((tm, d), lambda i, j: (i, 0)),
            pl.BlockSpec((tm, aw), lambda i, j: (i, 0)),
            pl.BlockSpec((tm, rw), lambda i, j: (i, 0)),
            pl.BlockSpec((aw + rw, d), lambda i, j: (0, 0), pipeline_mode=pl.Buffered(1)),
            pl.BlockSpec((1, d), lambda i, j: (0, 0)),
            pl.BlockSpec((d, fc), lambda i, j: (0, j)),
            pl.BlockSpec((fc, d), lambda i, j: (j, 0)),
        ],
        out_specs=pl.BlockSpec((tm, d), lambda i, j: (i, 0)),
        out_shape=jax.ShapeDtypeStruct((m, d), F32),
        scratch_shapes=[pltpu.VMEM((tm, d), BF16)],
        compiler_params=_params("parallel", "arbitrary"),
        name="finish",
    )(x, att, rec, wo_bf, ffn_g, wu_bf, wd_bf)


def _rope_tables(pos, head_dim):
    rot = head_dim // 4
    half = rot // 2
    inv = ROPE_THETA ** (-jnp.arange(half, dtype=F32) * 2.0 / rot)
    ang = pos.astype(F32)[:, None] * inv[None, :]
    cos, sin = jnp.cos(ang), jnp.sin(ang)
    ones = jnp.ones((pos.shape[0], head_dim - rot), F32)
    zeros = jnp.zeros((pos.shape[0], head_dim - rot), F32)
    zh = jnp.zeros_like(sin)
    c = jnp.concatenate([cos, cos, ones], axis=1)
    sa = jnp.concatenate([-sin, zh, zeros], axis=1)
    sb = jnp.concatenate([zh, sin, zeros], axis=1)
    reps = LANES // head_dim
    return tuple(jnp.tile(a, (1, reps)) for a in (c, sa, sb))


def kernel(x_prompt, x_sample, cache_k, cache_v, state_rec, page_table, attn_norm, w_in, q_norm, k_norm,
           lambda_q1, lambda_k1, lambda_q2, lambda_k2, att_out_norm, rec_lb_logits, rec_out_norm,
           beta_att, beta_rec, w_out, ffn_norm, w_up, w_down):
    bp, tp, d = x_prompt.shape
    bs, ts, _ = x_sample.shape
    depth = w_in.shape[0]
    n_pool, page, nheads, _, head_dim = cache_k.shape[1:]
    vd = cache_v.shape[-1]
    nrec, dk, dv = state_rec.shape[2:]
    npages = page_table.shape[1]
    past = npages * page
    aw = nheads * vd
    gw = nrec * dk
    assert bp == 1 and 2 * head_dim == vd == LANES and aw == gw and w_in.shape[2] == 7 * gw

    lam_inits = tuple(0.8 - 0.6 * math.exp(-0.3 * l) for l in range(depth))
    lam_all, lb_all = _prep(lambda_q1, lambda_k1, lambda_q2, lambda_k2, rec_lb_logits, lam_inits)

    lane_map = jnp.arange(MXU_TILE) // head_dim
    gmat = (lane_map[:, None] == lane_map[None, :]).astype(BF16)
    rope_p = _rope_tables(jnp.arange(tp), head_dim)
    rope_s = tuple(jnp.tile(a, (bs, 1)) for a in _rope_tables(past + jnp.arange(ts), head_dim))
    srow = jnp.arange(2 * nheads * 8)
    bmask = ((srow[:, None] // (nheads * 8) == (jnp.arange(aw)[None, :] // head_dim) % 2)
             & ((srow[:, None] // 8) % nheads == jnp.arange(aw)[None, :] // vd)).astype(F32)

    hp = x_prompt.reshape(tp, d)
    hs = x_sample.reshape(bs * ts, d)
    outs = [[] for _ in range(6)]
    for l in range(depth):
        w_in_bf = w_in[l].astype(BF16)
        w_out_bf = w_out[l].astype(BF16)
        w_up_bf = w_up[l].astype(BF16)
        w_down_bf = w_down[l].astype(BF16)
        qk_gain = jnp.stack([jnp.tile(q_norm[l], aw // head_dim),
                             jnp.tile(k_norm[l], aw // head_dim)]).reshape(2, 1, aw)
        norm_g = attn_norm[l].reshape(1, d)
        lam = lam_all[l:l + 1]
        lb = lb_all[l:l + 1]
        att_g = att_out_norm[l].reshape(1, vd)
        rec_g = rec_out_norm[l].reshape(1, dv)
        b_att = beta_att[l].reshape(1, aw)
        b_rec = beta_rec[l].reshape(1, gw)
        ffn_g = ffn_norm[l].reshape(1, d)
        out_scale = 1.0 - lam_inits[l]

        proj = _proj(hp, norm_g, w_in_bf, qk_gain, gmat, *rope_p, head_dim=head_dim)
        att = _attn_prompt(proj, lam, att_g, b_att, nheads=nheads, out_scale=out_scale)
        rec, s_new = _hgrn_prompt(proj, lb, rec_g, b_rec, nheads=nrec, dk=dk, dv=dv)
        hp = _finish(hp, att, rec, w_out_bf, ffn_g, w_up_bf, w_down_bf)
        outs[0].append(proj[:, aw:2 * aw].reshape(bp, tp, nheads, 2, head_dim))
        outs[1].append(proj[:, 2 * aw:3 * aw].reshape(bp, tp, nheads, vd))
        outs[2].append(s_new.reshape(bp, nrec, dk, dv).astype(state_rec.dtype))

        proj_s = _proj(hs, norm_g, w_in_bf, qk_gain, gmat, *rope_s, head_dim=head_dim)
        proj3 = proj_s.reshape(bs, ts, 7 * gw)
        cache_kt = jnp.transpose(cache_k[l], (0, 2, 3, 4, 1)).reshape(n_pool, aw, page)
        cache_v2 = cache_v[l].reshape(n_pool, page * nheads, vd)
        att_s = _attn_decode(page_table, lam, att_g, b_att, bmask, proj3, cache_kt, cache_v2,
                             nheads=nheads, out_scale=out_scale)
        rec_s, s_new = _hgrn_decode(proj3, lb, rec_g, b_rec, state_rec[l], nheads=nrec, dk=dk, dv=dv)
        hs = _finish(hs, att_s.reshape(bs * ts, aw), rec_s.reshape(bs * ts, gw),
                     w_out_bf, ffn_g, w_up_bf, w_down_bf)
        outs[3].append(proj_s[:, aw:2 * aw].reshape(bs, ts, nheads, 2, head_dim))
        outs[4].append(proj_s[:, 2 * aw:3 * aw].reshape(bs, ts, nheads, vd))
        outs[5].append(s_new.astype(state_rec.dtype))

    return (hp.reshape(bp, tp, d), hs.reshape(bs, ts, d),
            jnp.stack(outs[0]), jnp.stack(outs[1]), jnp.stack(outs[2]),
            jnp.stack(outs[3]), jnp.stack(outs[4]), jnp.stack(outs[5]))
```

```python
import functools
import math

import jax
import jax.numpy as jnp
from jax import lax
from jax.experimental import pallas as pl
from jax.experimental.pallas import tpu as pltpu

F32 = jnp.float32
BF16 = jnp.bfloat16

EPS = 1e-6
ROPE_THETA = 500000.0
MASK_VALUE = -1e30
LANES = 128
MXU_TILE = 256
LOG2E = 1.4426950408889634
VMEM_LIMIT_BYTES = 56 * 1024 * 1024
REC_CHUNK = 64

_NT = (((1,), (1,)), ((), ()))
_TA = (((0,), (0,)), ((), ()))


def _params(*semantics):
    return pltpu.CompilerParams(dimension_semantics=semantics, vmem_limit_bytes=VMEM_LIMIT_BYTES)


def _pick(n, pref):
    if n <= pref:
        return n
    t = pref
    while n % t:
        t //= 2
    return t


def _split3(x):
    h1 = x.astype(BF16)
    r1 = x - h1.astype(F32)
    h2 = r1.astype(BF16)
    h3 = (r1 - h2.astype(F32)).astype(BF16)
    return h1, h2, h3


def _prep_body(lq1_ref, lk1_ref, lq2_ref, lk2_ref, logit_ref, lam_ref, lb_ref, *, lam_inits):
    for l, lam_init in enumerate(lam_inits):
        s1 = jnp.sum(lq1_ref[l:l + 1, :] * lk1_ref[l:l + 1, :], axis=1, keepdims=True)
        s2 = jnp.sum(lq2_ref[l:l + 1, :] * lk2_ref[l:l + 1, :], axis=1, keepdims=True)
        lam = jnp.exp(s1) - jnp.exp(s2) + lam_init
        lam_ref[l:l + 1, :] = jnp.broadcast_to(lam, (1, LANES))
    x = logit_ref[...]
    e = jnp.exp(x - jnp.max(x, axis=0, keepdims=True))
    sm = e / jnp.sum(e, axis=0, keepdims=True)
    acc = jnp.zeros((1, x.shape[1]), F32)
    for r in range(x.shape[0]):
        acc = acc + sm[r:r + 1, :]
        lb_ref[r:r + 1, :] = acc


def _prep(lq1, lk1, lq2, lk2, logits, lam_inits):
    depth = lq1.shape[0]
    return pl.pallas_call(
        functools.partial(_prep_body, lam_inits=lam_inits),
        out_shape=(jax.ShapeDtypeStruct((depth, LANES), F32),
                   jax.ShapeDtypeStruct(logits.shape, F32)),
        name="prep",
    )(lq1, lk1, lq2, lk2, logits)


def _proj_body(x_ref, g_ref, w_ref, qkg_ref, gmat_ref, cos_ref, sa_ref, sb_ref, o_ref, xn_ref,
               *, head_dim, rot_half):
    j = pl.program_id(1)

    @pl.when(j == 0)
    def _():
        x = x_ref[...]
        ms = jnp.mean(x * x, axis=-1, keepdims=True)
        xn_ref[...] = (x * lax.rsqrt(ms + EPS) * g_ref[...]).astype(BF16)

    y = jnp.dot(xn_ref[...], w_ref[...], preferred_element_type=F32)

    @pl.when(j < 2)
    def _():
        gm = gmat_ref[...]
        gt = gm.shape[0]
        gain = qkg_ref[0]
        c, sa, sb = cos_ref[...], sa_ref[...], sb_ref[...]
        for t in range(y.shape[1] // gt):
            yt = y[:, t * gt:(t + 1) * gt]
            y2 = yt * yt
            hi = y2.astype(BF16)
            lo = (y2 - hi.astype(F32)).astype(BF16)
            ss = (jnp.dot(hi, gm, preferred_element_type=F32)
                  + jnp.dot(lo, gm, preferred_element_type=F32))
            yn = yt * lax.rsqrt(ss * (1.0 / head_dim) + EPS) * gain[:, t * gt:(t + 1) * gt]
            for u in range(gt // LANES):
                yu = yn[:, u * LANES:(u + 1) * LANES]
                lane0 = t * gt + u * LANES
                o_ref[:, lane0:lane0 + LANES] = (
                    yu * c + pltpu.roll(yu, LANES - rot_half, 1) * sa + pltpu.roll(yu, rot_half, 1) * sb)

    @pl.when(j >= 2)
    def _():
        o_ref[...] = y


def _proj(x, norm_g, w_bf, qk_gain, gmat, cos_t, sa_t, sb_t, *, head_dim):
    m, d = x.shape
    gw = qk_gain.shape[2]
    gt = gmat.shape[0]
    ngroups = w_bf.shape[1] // gw
    tm = _pick(m, 1024)
    assert gw % gt == 0 and gt % LANES == 0
    return pl.pallas_call(
        functools.partial(_proj_body, head_dim=head_dim, rot_half=head_dim // 8),
        grid=(m // tm, ngroups),
        in_specs=[
            pl.BlockSpec((tm, d), lambda i, j: (i, 0)),
            pl.BlockSpec((1, d), lambda i, j: (0, 0)),
            pl.BlockSpec((d, gw), lambda i, j: (0, j)),
            pl.BlockSpec((1, 1, gw), lambda i, j: (jnp.minimum(j, 1), 0, 0)),
            pl.BlockSpec((gt, gt), lambda i, j: (0, 0)),
            pl.BlockSpec((tm, LANES), lambda i, j: (i, 0)),
            pl.BlockSpec((tm, LANES), lambda i, j: (i, 0)),
            pl.BlockSpec((tm, LANES), lambda i, j: (i, 0)),
        ],
        out_specs=pl.BlockSpec((tm, gw), lambda i, j: (i, j)),
        out_shape=jax.ShapeDtypeStruct((m, ngroups * gw), F32),
        scratch_shapes=[pltpu.VMEM((tm, d), BF16)],
        compiler_params=_params("parallel", "arbitrary"),
        name="proj",
    )(x, norm_g, w_bf, qk_gain, gmat, cos_t, sa_t, sb_t)


def _attn_body(lam_ref, q_ref, k_ref, v_ref, g_ref, beta_ref, o_ref, kt_ref, vo_ref,
               kb_ref, vb_ref, qs_ref, m_ref, l_ref, acc_ref, *, tq, wide, out_scale):
    qi = pl.program_id(1)
    half = LANES // 2

    @pl.when(qi == 0)
    def _():
        kb_ref[...] = k_ref[...].astype(BF16)
        vb_ref[...] = v_ref[...].astype(BF16)
        vo_ref[...] = v_ref[...]
        for i in range(k_ref.shape[0] // tq):
            kt_ref[:, i * tq:(i + 1) * tq] = k_ref[i * tq:(i + 1) * tq, :].T

    q = q_ref[...] * (LOG2E / math.sqrt(half))
    lane = lax.broadcasted_iota(jnp.int32, (1, LANES), 1)
    qs_ref[0:tq, :] = jnp.where(lane < half, q, 0.0).astype(BF16)
    qs_ref[tq:2 * tq, :] = jnp.where(lane >= half, q, 0.0).astype(BF16)
    m_ref[...] = jnp.full(m_ref.shape, MASK_VALUE, F32)
    l_ref[...] = jnp.zeros(l_ref.shape, F32)
    acc_ref[...] = jnp.zeros(acc_ref.shape, F32)

    def block(off, width, diag_at):
        kblk = kb_ref[pl.ds(off, width), :]
        vblk = vb_ref[pl.ds(off, width), :]
        for c in range(2):
            rs = slice(c * tq, (c + 1) * tq)
            s = lax.dot_general(qs_ref[rs, :], kblk, _NT, preferred_element_type=F32)
            if diag_at is not None:
                row = lax.broadcasted_iota(jnp.int32, (tq, tq), 0)
                col = lax.broadcasted_iota(jnp.int32, (tq, tq), 1)
                tail = jnp.where(col <= row, s[:, diag_at:], MASK_VALUE)
                s = tail if diag_at == 0 else jnp.concatenate([s[:, :diag_at], tail], axis=1)
            m_prev = m_ref[rs, :]
            m_next = jnp.maximum(m_prev, jnp.max(s, axis=1, keepdims=True))
            alpha = jnp.exp2(m_prev - m_next)
            p = jnp.exp2(s - jnp.tile(m_next, (1, width // LANES)))
            l_ref[rs, :] = alpha * l_ref[rs, :] + jnp.sum(p, axis=1, keepdims=True)
            acc_ref[rs, :] = alpha * acc_ref[rs, :] + jnp.dot(
                p.astype(BF16), vblk, preferred_element_type=F32)
            m_ref[rs, :] = m_next

    per_wide = wide // tq
    nwide = qi // per_wide
    rem = qi - nwide * per_wide

    def wide_body(j, carry):
        block(pl.multiple_of(j * wide, wide), wide, None)
        return carry

    lax.fori_loop(0, nwide, wide_body, 0)
    for r in range(per_wide):
        @pl.when(rem == r)
        def _(r=r):
            block(pl.multiple_of(nwide * wide, wide), (r + 1) * tq, r * tq)

    a = acc_ref[...] / l_ref[...]
    o = a[0:tq, :] - lam_ref[...] * a[tq:2 * tq, :]
    ms = jnp.mean(o * o, axis=-1, keepdims=True)
    o_ref[...] = (o * lax.rsqrt(ms + EPS) * g_ref[...] * out_scale * beta_ref[...]).astype(o_ref.dtype)


def _attn_prompt(proj, lam, gain, beta, *, nheads, out_scale):
    t = proj.shape[0]
    tq = _pick(t, 512)
    wide = _pick(t, 4 * tq)
    assert tq % LANES == 0 and wide % tq == 0
    return pl.pallas_call(
        functools.partial(_attn_body, tq=tq, wide=wide, out_scale=out_scale),
        grid=(nheads, t // tq),
        in_specs=[
            pl.BlockSpec((1, LANES), lambda h, i: (0, 0)),
            pl.BlockSpec((tq, LANES), lambda h, i: (i, h)),
            pl.BlockSpec((t, LANES), lambda h, i: (0, nheads + h)),
            pl.BlockSpec((t, LANES), lambda h, i: (0, 2 * nheads + h)),
            pl.BlockSpec((1, LANES), lambda h, i: (0, 0)),
            pl.BlockSpec((1, LANES), lambda h, i: (0, h)),
        ],
        out_specs=(pl.BlockSpec((tq, LANES), lambda h, i: (i, h)),
                   pl.BlockSpec((LANES, t), lambda h, i: (h, 0)),
                   pl.BlockSpec((t, LANES), lambda h, i: (0, h))),
        out_shape=(jax.ShapeDtypeStruct((t, nheads * LANES), BF16),
                   jax.ShapeDtypeStruct((nheads * LANES, t), F32),
                   jax.ShapeDtypeStruct((t, nheads * LANES), F32)),
        scratch_shapes=[
            pltpu.VMEM((t, LANES), BF16),
            pltpu.VMEM((t, LANES), BF16),
            pltpu.VMEM((2 * tq, LANES), BF16),
            pltpu.VMEM((2 * tq, LANES), F32),
            pltpu.VMEM((2 * tq, LANES), F32),
            pltpu.VMEM((2 * tq, LANES), F32),
        ],
        compiler_params=_params("parallel", "arbitrary"),
        name="attn_prompt",
    )(lam, proj, proj, proj, gain, beta)


def _attn_decode_body(pt_ref, lam_ref, g_ref, beta_ref, bmask_ref, q_ref, kn_ref, vn_ref, *rest,
                      npages, page, nheads, ts, out_scale):
    del pt_ref
    kp_refs = rest[:npages]
    vp_refs = rest[npages:2 * npages]
    o_ref, ks_ref, vs_ref, kn_scr = rest[2 * npages:]
    past = npages * page
    tail, w = kn_scr.shape
    nrow = bmask_ref.shape[0]
    half = LANES // 2

    for p in range(npages):
        ks_ref[:, p * page:(p + 1) * page] = kp_refs[p][...].astype(BF16)
        for h in range(nheads):
            vs_ref[p * page:(p + 1) * page, h * LANES:(h + 1) * LANES] = (
                vp_refs[p][pl.ds(h, page, stride=nheads), :].astype(BF16))
    kn_scr[...] = jnp.zeros((tail, w), BF16)
    kn_scr[0:ts, :] = kn_ref[0].astype(BF16)
    vs_ref[past:, :] = jnp.zeros((tail, w), BF16)
    vs_ref[past:past + ts, :] = vn_ref[0].astype(BF16)

    q = q_ref[0] * (LOG2E / math.sqrt(half))
    qt = (jnp.tile(q, (nrow // ts, 1)) * bmask_ref[...]).astype(BF16)
    s_past = jnp.dot(qt, ks_ref[...], preferred_element_type=F32)
    s_new = lax.dot_general(qt, kn_scr[...], _NT, preferred_element_type=F32)
    row = lax.broadcasted_iota(jnp.int32, s_new.shape, 0)
    col = lax.broadcasted_iota(jnp.int32, s_new.shape, 1)
    s_new = jnp.where(col <= lax.rem(row, ts), s_new, MASK_VALUE)
    m = jnp.maximum(jnp.max(s_past, axis=1, keepdims=True), jnp.max(s_new, axis=1, keepdims=True))
    p_past = jnp.exp2(s_past - m)
    p_new = jnp.exp2(s_new - m)
    l = jnp.sum(p_past, axis=1, keepdims=True) + jnp.sum(p_new, axis=1, keepdims=True)
    row1 = lax.broadcasted_iota(jnp.int32, (nrow, 1), 0)
    hr = nrow // 2
    wgt = jnp.where(row1 < hr, 1.0, -lam_ref[:, 0:1]) / l
    pw_past = p_past * wgt
    pw_new = p_new * wgt
    a_past = (pw_past[0:hr, :] + pw_past[hr:, :]).astype(BF16)
    a_new = (pw_new[0:hr, :] + pw_new[hr:, :]).astype(BF16)
    o2 = (jnp.dot(a_past, vs_ref[0:past, :], preferred_element_type=F32)
          + jnp.dot(a_new, vs_ref[past:, :], preferred_element_type=F32))
    for h in range(nheads):
        cs = slice(h * LANES, (h + 1) * LANES)
        o = o2[h * 8:(h + 1) * 8, cs]
        ms = jnp.mean(o * o, axis=-1, keepdims=True)
        o = o * lax.rsqrt(ms + EPS) * g_ref[...] * out_scale * beta_ref[:, cs]
        o_ref[0, :, cs] = o[0:ts, :].astype(o_ref.dtype)


def _attn_decode(page_table, lam, gain, beta, bmask, proj3, cache_kt, cache_v2, *, nheads, out_scale):
    nb, ts, _ = proj3.shape
    npages = page_table.shape[1]
    w, page = cache_kt.shape[1], cache_kt.shape[2]
    nrow = bmask.shape[0]
    tail = LANES
    assert 8 % ts == 0 and w == nheads * LANES and cache_v2.shape[1:] == (page * nheads, LANES)

    def k_spec(p):
        return pl.BlockSpec((None, w, page), lambda b, pt: (pt[b, p], 0, 0))

    def v_spec(p):
        return pl.BlockSpec((None, page * nheads, LANES), lambda b, pt: (pt[b, p], 0, 0))

    grid_spec = pltpu.PrefetchScalarGridSpec(
        num_scalar_prefetch=1,
        grid=(nb,),
        in_specs=[
            pl.BlockSpec((1, LANES), lambda b, pt: (0, 0)),
            pl.BlockSpec((1, LANES), lambda b, pt: (0, 0)),
            pl.BlockSpec((1, w), lambda b, pt: (0, 0)),
            pl.BlockSpec((nrow, w), lambda b, pt: (0, 0)),
            pl.BlockSpec((1, ts, w), lambda b, pt: (b, 0, 0)),
            pl.BlockSpec((1, ts, w), lambda b, pt: (b, 0, 1)),
            pl.BlockSpec((1, ts, w), lambda b, pt: (b, 0, 2)),
        ] + [k_spec(p) for p in range(npages)] + [v_spec(p) for p in range(npages)],
        out_specs=pl.BlockSpec((1, ts, w), lambda b, pt: (b, 0, 0)),
        scratch_shapes=[
            pltpu.VMEM((w, npages * page), BF16),
            pltpu.VMEM((npages * page + tail, w), BF16),
            pltpu.VMEM((tail, w), BF16),
        ],
    )
    return pl.pallas_call(
        functools.partial(_attn_decode_body, npages=npages, page=page, nheads=nheads, ts=ts,
                          out_scale=out_scale),
        grid_spec=grid_spec,
        out_shape=jax.ShapeDtypeStruct((nb, ts, w), BF16),
        compiler_params=_params("arbitrary"),
        name="attn_decode",
    )(page_table, lam, gain, beta, bmask, proj3, proj3, proj3,
      *([cache_kt] * npages), *([cache_v2] * npages))


def _hgrn_body(rq_ref, rf_ref, ri_ref, rg_ref, lb_ref, gn_ref, beta_ref, tri_ref, *rest,
               nheads, dk, chunk, batched):
    if batched:
        s0_ref, rec_ref, sout_ref, st_ref, o_scr = rest
        nseq = rq_ref.shape[0]
    else:
        rec_ref, sout_ref, st_ref, o_scr = rest
        nseq = 1

        @pl.when(pl.program_id(0) == 0)
        def _():
            st_ref[...] = jnp.zeros(st_ref.shape, F32)

    lb = lb_ref[...]
    tri = tri_ref[...]
    tr = lax.broadcasted_iota(jnp.int32, (chunk, chunk), 0)
    tc = lax.broadcasted_iota(jnp.int32, (chunk, chunk), 1)
    causal = tc <= tr
    for bi in range(nseq):
        sb = bi * nheads
        if batched:
            rq, rf, ri, rg = rq_ref[bi], rf_ref[bi], ri_ref[bi], rg_ref[bi]
            for h in range(nheads):
                st_ref[sb + h] = s0_ref[bi, h].T
        else:
            rq, rf, ri, rg = rq_ref[...], rf_ref[...], ri_ref[...], rg_ref[...]
        rows = rq.shape[0]
        q = rq * jax.nn.sigmoid(rq)
        fg = lb + (1.0 - lb) * jax.nn.sigmoid(rf)
        logf = jnp.log(fg)
        kk = 1.0 - fg
        h1, h2, h3 = _split3(logf)
        b = (jnp.dot(tri, h1, preferred_element_type=F32)
             + jnp.dot(tri, h2, preferred_element_type=F32)
             + jnp.dot(tri, h3, preferred_element_type=F32))
        q_in = (q * jnp.exp(b)).astype(BF16)
        k_in = (kk * jnp.exp(-b)).astype(BF16)
        v_bf = ri.astype(BF16)
        for c in range(rows // chunk):
            rs = slice(c * chunk, (c + 1) * chunk)
            b_last = b[(c + 1) * chunk - 1:(c + 1) * chunk, :]
            k_st = (kk[rs, :] * jnp.exp(b_last - b[rs, :])).astype(BF16)
            decay = jnp.exp(b_last)
            for h in range(nheads):
                cs = slice(h * dk, (h + 1) * dk)
                qc, kc, vc = q_in[rs, cs], k_in[rs, cs], v_bf[rs, cs]
                att = lax.dot_general(qc, kc, _NT, preferred_element_type=F32)
                att = jnp.where(causal, att, 0.0).astype(BF16)
                st = st_ref[sb + h]
                o_scr[bi, rs, cs] = (
                    jnp.dot(att, vc, preferred_element_type=F32)
                    + lax.dot_general(qc, st.astype(BF16), _NT, preferred_element_type=F32))
                st_ref[sb + h] = st * decay[:, cs] + lax.dot_general(
                    vc, k_st[:, cs], _TA, preferred_element_type=F32)
        gate = rg * jax.nn.sigmoid(rg)
        for h in range(nheads):
            cs = slice(h * dk, (h + 1) * dk)
            o = o_scr[bi, :, cs]
            ms = jnp.mean(o * o, axis=-1, keepdims=True)
            o = (o * lax.rsqrt(ms + EPS) * gn_ref[...]) * gate[:, cs] * beta_ref[:, cs]
            if batched:
                rec_ref[bi, :, cs] = o.astype(rec_ref.dtype)
                sout_ref[bi, h] = st_ref[sb + h].T
            else:
                rec_ref[:, cs] = o.astype(rec_ref.dtype)

    if not batched:
        @pl.when(pl.program_id(0) == pl.num_programs(0) - 1)
        def _():
            for h in range(nheads):
                sout_ref[h] = st_ref[h].T


def _tri(rows, chunk):
    r = jnp.arange(rows)
    return ((r[:, None] >= r[None, :]) & (r[:, None] // chunk == r[None, :] // chunk)).astype(BF16)


def _hgrn_prompt(proj, lb, gain, beta, *, nheads, dk, dv):
    t = proj.shape[0]
    gw = nheads * dk
    chunk = math.gcd(t, REC_CHUNK)
    rows = _pick(t, 4 * chunk)
    assert dk == dv == LANES
    specs = [pl.BlockSpec((rows, gw), functools.partial(lambda i, g: (i, g), g=3 + g)) for g in range(4)]
    return pl.pallas_call(
        functools.partial(_hgrn_body, nheads=nheads, dk=dk, chunk=chunk, batched=False),
        grid=(t // rows,),
        in_specs=specs + [
            pl.BlockSpec((1, gw), lambda i: (0, 0)),
            pl.BlockSpec((1, dv), lambda i: (0, 0)),
            pl.BlockSpec((1, gw), lambda i: (0, 0)),
            pl.BlockSpec((rows, rows), lambda i: (0, 0)),
        ],
        out_specs=(pl.BlockSpec((rows, gw), lambda i: (i, 0)),
                   pl.BlockSpec((nheads, dk, dv), lambda i: (0, 0, 0))),
        out_shape=(jax.ShapeDtypeStruct((t, gw), BF16),
                   jax.ShapeDtypeStruct((nheads, dk, dv), F32)),
        scratch_shapes=[pltpu.VMEM((nheads, dv, dk), F32), pltpu.VMEM((1, rows, gw), F32)],
        compiler_params=_params("arbitrary"),
        name="hgrn_prompt",
    )(proj, proj, proj, proj, lb, gain, beta, _tri(rows, chunk))


def _hgrn_decode(proj3, lb, gain, beta, s0, *, nheads, dk, dv):
    nb, ts, _ = proj3.shape
    gw = nheads * dk
    chunk = math.gcd(ts, REC_CHUNK)
    assert dk == dv == LANES and chunk == ts
    nseq = _pick(nb, 4)
    specs = [pl.BlockSpec((nseq, ts, gw), functools.partial(lambda b, g: (b, 0, g), g=3 + g)) for g in range(4)]
    return pl.pallas_call(
        functools.partial(_hgrn_body, nheads=nheads, dk=dk, chunk=chunk, batched=True),
        grid=(nb // nseq,),
        in_specs=specs + [
            pl.BlockSpec((1, gw), lambda b: (0, 0)),
            pl.BlockSpec((1, dv), lambda b: (0, 0)),
            pl.BlockSpec((1, gw), lambda b: (0, 0)),
            pl.BlockSpec((ts, ts), lambda b: (0, 0)),
            pl.BlockSpec((nseq, nheads, dk, dv), lambda b: (b, 0, 0, 0)),
        ],
        out_specs=(pl.BlockSpec((nseq, ts, gw), lambda b: (b, 0, 0)),
                   pl.BlockSpec((nseq, nheads, dk, dv), lambda b: (b, 0, 0, 0))),
        out_shape=(jax.ShapeDtypeStruct((nb, ts, gw), BF16),
                   jax.ShapeDtypeStruct((nb, nheads, dk, dv), F32)),
        scratch_shapes=[pltpu.VMEM((nseq * nheads, dv, dk), F32), pltpu.VMEM((nseq, ts, gw), F32)],
        compiler_params=_params("parallel"),
        name="hgrn_decode",
    )(proj3, proj3, proj3, proj3, lb, gain, beta, _tri(ts, chunk), s0)


def _finish_body(x_ref, att_ref, rec_ref, wo_ref, g_ref, wu_ref, wd_ref, o_ref, hn_ref):
    j = pl.program_id(1)

    @pl.when(j == 0)
    def _():
        aw = att_ref.shape[1]
        h = (x_ref[...]
             + jnp.dot(att_ref[...], wo_ref[0:aw, :], preferred_element_type=F32)
             + jnp.dot(rec_ref[...], wo_ref[aw:, :], preferred_element_type=F32))
        ms = jnp.mean(h * h, axis=-1, keepdims=True)
        hn_ref[...] = (h * lax.rsqrt(ms + EPS) * g_ref[...]).astype(BF16)
        o_ref[...] = h

    u = jnp.maximum(jnp.dot(hn_ref[...], wu_ref[...], preferred_element_type=F32), 0.0)
    o_ref[...] += jnp.dot((u * u).astype(BF16), wd_ref[...], preferred_element_type=F32)


def _finish(x, att, rec, wo_bf, ffn_g, wu_bf, wd_bf):
    m, d = x.shape
    aw, rw = att.shape[1], rec.shape[1]
    ff = wu_bf.shape[1]
    tm = _pick(m, 512)
    fc = _pick(ff, 1024)
    return pl.pallas_call(
        _finish_body,
        grid=(m // tm, ff // fc),
        in_specs=[
            pl.BlockSpec((tm, d), lambda i, j: (i, 0)),
            pl.BlockSpec((tm, aw), lambda i, j: (i, 0)),
            pl.BlockSpec((tm, rw), lambda i, j: (i, 0)),
            pl.BlockSpec((aw + rw, d), lambda i, j: (0, 0), pipeline_mode=pl.Buffered(1)),
            pl.BlockSpec((1, d), lambda i, j: (0, 0)),
            pl.BlockSpec((d, fc), lambda i, j: (0, j)),
            pl.BlockSpec((fc, d), lambda i, j: (j, 0)),
        ],
        out_specs=pl.BlockSpec((tm, d), lambda i, j: (i, 0)),
        out_shape=jax.ShapeDtypeStruct((m, d), F32),
        scratch_shapes=[pltpu.VMEM((tm, d), BF16)],
        compiler_params=_params("parallel", "arbitrary"),
        name="finish",
    )(x, att, rec, wo_bf, ffn_g, wu_bf, wd_bf)


def _rope_tables(pos, head_dim):
    rot = head_dim // 4
    half = rot // 2
    inv = ROPE_THETA ** (-jnp.arange(half, dtype=F32) * 2.0 / rot)
    ang = pos.astype(F32)[:, None] * inv[None, :]
    cos, sin = jnp.cos(ang), jnp.sin(ang)
    ones = jnp.ones((pos.shape[0], head_dim - rot), F32)
    zeros = jnp.zeros((pos.shape[0], head_dim - rot), F32)
    zh = jnp.zeros_like(sin)
    c = jnp.concatenate([cos, cos, ones], axis=1)
    sa = jnp.concatenate([-sin, zh, zeros], axis=1)
    sb = jnp.concatenate([zh, sin, zeros], axis=1)
    reps = LANES // head_dim
    return tuple(jnp.tile(a, (1, reps)) for a in (c, sa, sb))


def kernel(x_prompt, x_sample, cache_k, cache_v, state_rec, page_table, attn_norm, w_in, q_norm, k_norm,
           lambda_q1, lambda_k1, lambda_q2, lambda_k2, att_out_norm, rec_lb_logits, rec_out_norm,
           beta_att, beta_rec, w_out, ffn_norm, w_up, w_down):
    bp, tp, d = x_prompt.shape
    bs, ts, _ = x_sample.shape
    depth = w_in.shape[0]
    n_pool, page, nheads, _, head_dim = cache_k.shape[1:]
    vd = cache_v.shape[-1]
    nrec, dk, dv = state_rec.shape[2:]
    npages = page_table.shape[1]
    past = npages * page
    aw = nheads * vd
    gw = nrec * dk
    assert bp == 1 and 2 * head_dim == vd == LANES and aw == gw and w_in.shape[2] == 7 * gw

    lam_inits = tuple(0.8 - 0.6 * math.exp(-0.3 * l) for l in range(depth))
    lam_all, lb_all = _prep(lambda_q1, lambda_k1, lambda_q2, lambda_k2, rec_lb_logits, lam_inits)

    lane_map = jnp.arange(MXU_TILE) // head_dim
    gmat = (lane_map[:, None] == lane_map[None, :]).astype(BF16)
    rope_p = _rope_tables(jnp.arange(tp), head_dim)
    rope_s = tuple(jnp.tile(a, (bs, 1)) for a in _rope_tables(past + jnp.arange(ts), head_dim))
    srow = jnp.arange(2 * nheads * 8)
    bmask = ((srow[:, None] // (nheads * 8) == (jnp.arange(aw)[None, :] // head_dim) % 2)
             & ((srow[:, None] // 8) % nheads == jnp.arange(aw)[None, :] // vd)).astype(F32)

    hp = x_prompt.reshape(tp, d)
    hs = x_sample.reshape(bs * ts, d)
    outs = [[] for _ in range(6)]
    for l in range(depth):
        w_in_bf = w_in[l].astype(BF16)
        w_out_bf = w_out[l].astype(BF16)
        w_up_bf = w_up[l].astype(BF16)
        w_down_bf = w_down[l].astype(BF16)
        qk_gain = jnp.stack([jnp.tile(q_norm[l], aw // head_dim),
                             jnp.tile(k_norm[l], aw // head_dim)]).reshape(2, 1, aw)
        norm_g = attn_norm[l].reshape(1, d)
        lam = lam_all[l:l + 1]
        lb = lb_all[l:l + 1]
        att_g = att_out_norm[l].reshape(1, vd)
        rec_g = rec_out_norm[l].reshape(1, dv)
        b_att = beta_att[l].reshape(1, aw)
        b_rec = beta_rec[l].reshape(1, gw)
        ffn_g = ffn_norm[l].reshape(1, d)
        out_scale = 1.0 - lam_inits[l]

        proj = _proj(hp, norm_g, w_in_bf, qk_gain, gmat, *rope_p, head_dim=head_dim)
        att, k_t, v_p = _attn_prompt(proj, lam, att_g, b_att, nheads=nheads, out_scale=out_scale)
        rec, s_new = _hgrn_prompt(proj, lb, rec_g, b_rec, nheads=nrec, dk=dk, dv=dv)
        hp = _finish(hp, att, rec, w_out_bf, ffn_g, w_up_bf, w_down_bf)
        outs[0].append(jnp.transpose(k_t.reshape(nheads, 2, head_dim, tp), (3, 0, 1, 2))
                       .reshape(bp, tp, nheads, 2, head_dim))
        outs[1].append(v_p.reshape(bp, tp, nheads, vd))
        outs[2].append(s_new.reshape(bp, nrec, dk, dv).astype(state_rec.dtype))

        proj_s = _proj(hs, norm_g, w_in_bf, qk_gain, gmat, *rope_s, head_dim=head_dim)
        proj3 = proj_s.reshape(bs, ts, 7 * gw)
        cache_kt = jnp.transpose(cache_k[l], (0, 2, 3, 4, 1)).reshape(n_pool, aw, page)
        cache_v2 = cache_v[l].reshape(n_pool, page * nheads, vd)
        att_s = _attn_decode(page_table, lam, att_g, b_att, bmask, proj3, cache_kt, cache_v2,
                             nheads=nheads, out_scale=out_scale)
        rec_s, s_new = _hgrn_decode(proj3, lb, rec_g, b_rec, state_rec[l], nheads=nrec, dk=dk, dv=dv)
        hs = _finish(hs, att_s.reshape(bs * ts, aw), rec_s.reshape(bs * ts, gw),
                     w_out_bf, ffn_g, w_up_bf, w_down_bf)
        outs[3].append(proj_s[:, aw:2 * aw].reshape(bs, ts, nheads, 2, head_dim))
        outs[4].append(proj_s[:, 2 * aw:3 * aw].reshape(bs, ts, nheads, vd))
        outs[5].append(s_new.astype(state_rec.dtype))

    return (hp.reshape(bp, tp, d), hs.reshape(bs, ts, d),
            jnp.stack(outs[0]), jnp.stack(outs[1]), jnp.stack(outs[2]),
            jnp.stack(outs[3]), jnp.stack(outs[4]), jnp.stack(outs[5]))
```

```python
import functools
import math

import jax
import jax.numpy as jnp
from jax import lax
from jax.experimental import pallas as pl
from jax.experimental.pallas import tpu as pltpu

F32 = jnp.float32
BF16 = jnp.bfloat16

EPS = 1e-6
ROPE_THETA = 500000.0
MASK_VALUE = -1e30
LANES = 128
MXU_TILE = 256
LOG2E = 1.4426950408889634
VMEM_LIMIT_BYTES = 56 * 1024 * 1024
REC_CHUNK = 64

_NT = (((1,), (1,)), ((), ()))
_TA = (((0,), (0,)), ((), ()))


def _params(*semantics):
    return pltpu.CompilerParams(dimension_semantics=semantics, vmem_limit_bytes=VMEM_LIMIT_BYTES)


def _pick(n, pref):
    if n <= pref:
        return n
    t = pref
    while n % t:
        t //= 2
    return t


def _split3(x):
    h1 = x.astype(BF16)
    r1 = x - h1.astype(F32)
    h2 = r1.astype(BF16)
    h3 = (r1 - h2.astype(F32)).astype(BF16)
    return h1, h2, h3


def _prep_body(lq1_ref, lk1_ref, lq2_ref, lk2_ref, logit_ref, lam_ref, lb_ref, *, lam_inits):
    for l, lam_init in enumerate(lam_inits):
        s1 = jnp.sum(lq1_ref[l:l + 1, :] * lk1_ref[l:l + 1, :], axis=1, keepdims=True)
        s2 = jnp.sum(lq2_ref[l:l + 1, :] * lk2_ref[l:l + 1, :], axis=1, keepdims=True)
        lam = jnp.exp(s1) - jnp.exp(s2) + lam_init
        lam_ref[l:l + 1, :] = jnp.broadcast_to(lam, (1, LANES))
    x = logit_ref[...]
    e = jnp.exp(x - jnp.max(x, axis=0, keepdims=True))
    sm = e / jnp.sum(e, axis=0, keepdims=True)
    acc = jnp.zeros((1, x.shape[1]), F32)
    for r in range(x.shape[0]):
        acc = acc + sm[r:r + 1, :]
        lb_ref[r:r + 1, :] = acc


def _prep(lq1, lk1, lq2, lk2, logits, lam_inits):
    depth = lq1.shape[0]
    return pl.pallas_call(
        functools.partial(_prep_body, lam_inits=lam_inits),
        out_shape=(jax.ShapeDtypeStruct((depth, LANES), F32),
                   jax.ShapeDtypeStruct(logits.shape, F32)),
        name="prep",
    )(lq1, lk1, lq2, lk2, logits)


def _proj_body(x_ref, g_ref, w_ref, qkg_ref, gmat_ref, cos_ref, sa_ref, sb_ref, o_ref, xn_ref,
               *, head_dim, rot_half):
    j = pl.program_id(1)

    @pl.when(j == 0)
    def _():
        x = x_ref[...]
        ms = jnp.mean(x * x, axis=-1, keepdims=True)
        xn_ref[...] = (x * lax.rsqrt(ms + EPS) * g_ref[...]).astype(BF16)

    y = jnp.dot(xn_ref[...], w_ref[...], preferred_element_type=F32)

    @pl.when(j < 2)
    def _():
        gm = gmat_ref[...]
        gt = gm.shape[0]
        gain = qkg_ref[0]
        c, sa, sb = cos_ref[...], sa_ref[...], sb_ref[...]
        for t in range(y.shape[1] // gt):
            yt = y[:, t * gt:(t + 1) * gt]
            y2 = yt * yt
            hi = y2.astype(BF16)
            lo = (y2 - hi.astype(F32)).astype(BF16)
            ss = (jnp.dot(hi, gm, preferred_element_type=F32)
                  + jnp.dot(lo, gm, preferred_element_type=F32))
            yn = yt * lax.rsqrt(ss * (1.0 / head_dim) + EPS) * gain[:, t * gt:(t + 1) * gt]
            for u in range(gt // LANES):
                yu = yn[:, u * LANES:(u + 1) * LANES]
                lane0 = t * gt + u * LANES
                o_ref[:, lane0:lane0 + LANES] = (
                    yu * c + pltpu.roll(yu, LANES - rot_half, 1) * sa + pltpu.roll(yu, rot_half, 1) * sb)

    @pl.when(j >= 2)
    def _():
        o_ref[...] = y


def _proj(x, norm_g, w_bf, qk_gain, gmat, cos_t, sa_t, sb_t, *, head_dim):
    m, d = x.shape
    gw = qk_gain.shape[2]
    gt = gmat.shape[0]
    ngroups = w_bf.shape[1] // gw
    tm = _pick(m, 1024)
    assert gw % gt == 0 and gt % LANES == 0
    return pl.pallas_call(
        functools.partial(_proj_body, head_dim=head_dim, rot_half=head_dim // 8),
        grid=(m // tm, ngroups),
        in_specs=[
            pl.BlockSpec((tm, d), lambda i, j: (i, 0)),
            pl.BlockSpec((1, d), lambda i, j: (0, 0)),
            pl.BlockSpec((d, gw), lambda i, j: (0, j)),
            pl.BlockSpec((1, 1, gw), lambda i, j: (jnp.minimum(j, 1), 0, 0)),
            pl.BlockSpec((gt, gt), lambda i, j: (0, 0)),
            pl.BlockSpec((tm, LANES), lambda i, j: (i, 0)),
            pl.BlockSpec((tm, LANES), lambda i, j: (i, 0)),
            pl.BlockSpec((tm, LANES), lambda i, j: (i, 0)),
        ],
        out_specs=pl.BlockSpec((tm, gw), lambda i, j: (i, j)),
        out_shape=jax.ShapeDtypeStruct((m, ngroups * gw), F32),
        scratch_shapes=[pltpu.VMEM((tm, d), BF16)],
        compiler_params=_params("parallel", "arbitrary"),
        name="proj",
    )(x, norm_g, w_bf, qk_gain, gmat, cos_t, sa_t, sb_t)


def _attn_body(lam_ref, q_ref, k_ref, v_ref, g_ref, beta_ref, o_ref, kt_ref, vo_ref,
               kb_ref, vb_ref, qs_ref, m_ref, l_ref, acc_ref, *, tq, wide, out_scale):
    qi = pl.program_id(1)
    half = LANES // 2

    @pl.when(qi == 0)
    def _():
        kb_ref[...] = k_ref[...].astype(BF16)
        vb_ref[...] = v_ref[...].astype(BF16)
        vo_ref[...] = v_ref[...]
        for i in range(k_ref.shape[0] // tq):
            kt_ref[:, i * tq:(i + 1) * tq] = k_ref[i * tq:(i + 1) * tq, :].T

    q = q_ref[...] * (LOG2E / math.sqrt(half))
    lane = lax.broadcasted_iota(jnp.int32, (1, LANES), 1)
    qs_ref[0:tq, :] = jnp.where(lane < half, q, 0.0).astype(BF16)
    qs_ref[tq:2 * tq, :] = jnp.where(lane >= half, q, 0.0).astype(BF16)
    m_ref[...] = jnp.full(m_ref.shape, MASK_VALUE, F32)
    l_ref[...] = jnp.zeros(l_ref.shape, F32)
    acc_ref[...] = jnp.zeros(acc_ref.shape, F32)

    def block(off, width, diag_at):
        kblk = kb_ref[pl.ds(off, width), :]
        vblk = vb_ref[pl.ds(off, width), :]
        for c in range(2):
            rs = slice(c * tq, (c + 1) * tq)
            s = lax.dot_general(qs_ref[rs, :], kblk, _NT, preferred_element_type=F32)
            if diag_at is not None:
                row = lax.broadcasted_iota(jnp.int32, (tq, tq), 0)
                col = lax.broadcasted_iota(jnp.int32, (tq, tq), 1)
                tail = jnp.where(col <= row, s[:, diag_at:], MASK_VALUE)
                s = tail if diag_at == 0 else jnp.concatenate([s[:, :diag_at], tail], axis=1)
            m_prev = m_ref[rs, :]
            m_next = jnp.maximum(m_prev, jnp.max(s, axis=1, keepdims=True))
            alpha = jnp.exp2(m_prev - m_next)
            p = jnp.exp2(s - jnp.tile(m_next, (1, width // LANES)))
            l_ref[rs, :] = alpha * l_ref[rs, :] + jnp.sum(p, axis=1, keepdims=True)
            acc_ref[rs, :] = alpha * acc_ref[rs, :] + jnp.dot(
                p.astype(BF16), vblk, preferred_element_type=F32)
            m_ref[rs, :] = m_next

    per_wide = wide // tq
    nwide = qi // per_wide
    rem = qi - nwide * per_wide

    def wide_body(j, carry):
        block(pl.multiple_of(j * wide, wide), wide, None)
        return carry

    lax.fori_loop(0, nwide, wide_body, 0)
    for r in range(per_wide):
        @pl.when(rem == r)
        def _(r=r):
            block(pl.multiple_of(nwide * wide, wide), (r + 1) * tq, r * tq)

    a = acc_ref[...] / l_ref[...]
    o = a[0:tq, :] - lam_ref[...] * a[tq:2 * tq, :]
    ms = jnp.mean(o * o, axis=-1, keepdims=True)
    o_ref[...] = (o * lax.rsqrt(ms + EPS) * g_ref[...] * out_scale * beta_ref[...]).astype(o_ref.dtype)


def _attn_prompt(proj, lam, gain, beta, *, nheads, out_scale):
    t = proj.shape[0]
    tq = _pick(t, 512)
    wide = _pick(t, 4 * tq)
    assert tq % LANES == 0 and wide % tq == 0
    return pl.pallas_call(
        functools.partial(_attn_body, tq=tq, wide=wide, out_scale=out_scale),
        grid=(nheads, t // tq),
        in_specs=[
            pl.BlockSpec((1, LANES), lambda h, i: (0, 0)),
            pl.BlockSpec((tq, LANES), lambda h, i: (i, h)),
            pl.BlockSpec((t, LANES), lambda h, i: (0, nheads + h)),
            pl.BlockSpec((t, LANES), lambda h, i: (0, 2 * nheads + h)),
            pl.BlockSpec((1, LANES), lambda h, i: (0, 0)),
            pl.BlockSpec((1, LANES), lambda h, i: (0, h)),
        ],
        out_specs=(pl.BlockSpec((tq, LANES), lambda h, i: (i, h)),
                   pl.BlockSpec((LANES, t), lambda h, i: (h, 0)),
                   pl.BlockSpec((t, LANES), lambda h, i: (0, h))),
        out_shape=(jax.ShapeDtypeStruct((t, nheads * LANES), BF16),
                   jax.ShapeDtypeStruct((nheads * LANES, t), F32),
                   jax.ShapeDtypeStruct((t, nheads * LANES), F32)),
        scratch_shapes=[
            pltpu.VMEM((t, LANES), BF16),
            pltpu.VMEM((t, LANES), BF16),
            pltpu.VMEM((2 * tq, LANES), BF16),
            pltpu.VMEM((2 * tq, LANES), F32),
            pltpu.VMEM((2 * tq, LANES), F32),
            pltpu.VMEM((2 * tq, LANES), F32),
        ],
        compiler_params=_params("parallel", "arbitrary"),
        name="attn_prompt",
    )(lam, proj, proj, proj, gain, beta)


def _attn_decode_body(pt_ref, lam_ref, g_ref, beta_ref, bmask_ref, q_ref, kn_ref, vn_ref, *rest,
                      npages, page, nheads, ts, out_scale):
    del pt_ref
    kp_refs = rest[:npages]
    vp_refs = rest[npages:2 * npages]
    o_ref, ks_ref, vs_ref, kn_scr = rest[2 * npages:]
    past = npages * page
    tail, w = kn_scr.shape
    nrow = bmask_ref.shape[0]
    half = LANES // 2

    for p in range(npages):
        ks_ref[:, p * page:(p + 1) * page] = kp_refs[p][...].astype(BF16)
        for h in range(nheads):
            vs_ref[p * page:(p + 1) * page, h * LANES:(h + 1) * LANES] = (
                vp_refs[p][pl.ds(h, page, stride=nheads), :].astype(BF16))
    kn_scr[...] = jnp.zeros((tail, w), BF16)
    kn_scr[0:ts, :] = kn_ref[0].astype(BF16)
    vs_ref[past:, :] = jnp.zeros((tail, w), BF16)
    vs_ref[past:past + ts, :] = vn_ref[0].astype(BF16)

    q = q_ref[0] * (LOG2E / math.sqrt(half))
    qt = (jnp.tile(q, (nrow // ts, 1)) * bmask_ref[...]).astype(BF16)
    s_past = jnp.dot(qt, ks_ref[...], preferred_element_type=F32)
    s_new = lax.dot_general(qt, kn_scr[...], _NT, preferred_element_type=F32)
    row = lax.broadcasted_iota(jnp.int32, s_new.shape, 0)
    col = lax.broadcasted_iota(jnp.int32, s_new.shape, 1)
    s_new = jnp.where(col <= lax.rem(row, ts), s_new, MASK_VALUE)
    m = jnp.maximum(jnp.max(s_past, axis=1, keepdims=True), jnp.max(s_new, axis=1, keepdims=True))
    p_past = jnp.exp2(s_past - m)
    p_new = jnp.exp2(s_new - m)
    l = jnp.sum(p_past, axis=1, keepdims=True) + jnp.sum(p_new, axis=1, keepdims=True)
    row1 = lax.broadcasted_iota(jnp.int32, (nrow, 1), 0)
    hr = nrow // 2
    wgt = jnp.where(row1 < hr, 1.0, -lam_ref[:, 0:1]) / l
    pw_past = p_past * wgt
    pw_new = p_new * wgt
    a_past = (pw_past[0:hr, :] + pw_past[hr:, :]).astype(BF16)
    a_new = (pw_new[0:hr, :] + pw_new[hr:, :]).astype(BF16)
    o2 = (jnp.dot(a_past, vs_ref[0:past, :], preferred_element_type=F32)
          + jnp.dot(a_new, vs_ref[past:, :], preferred_element_type=F32))
    for h in range(nheads):
        cs = slice(h * LANES, (h + 1) * LANES)
        o = o2[h * 8:(h + 1) * 8, cs]
        ms = jnp.mean(o * o, axis=-1, keepdims=True)
        o = o * lax.rsqrt(ms + EPS) * g_ref[...] * out_scale * beta_ref[:, cs]
        o_ref[0, :, cs] = o[0:ts, :].astype(o_ref.dtype)


def _hgrn_body(rq_ref, rf_ref, ri_ref, rg_ref, lb_ref, gn_ref, beta_ref, tri_ref, *rest,
               nheads, dk, chunk, batched, last_step=None):
    if batched:
        s0_ref, rec_ref, sout_ref, st_ref, o_scr = rest
        nseq = rq_ref.shape[0]
    else:
        rec_ref, sout_ref, st_ref, o_scr = rest
        nseq = 1

        @pl.when(pl.program_id(0) == 0)
        def _():
            st_ref[...] = jnp.zeros(st_ref.shape, F32)

    lb = lb_ref[...]
    tri = tri_ref[...]
    tr = lax.broadcasted_iota(jnp.int32, (chunk, chunk), 0)
    tc = lax.broadcasted_iota(jnp.int32, (chunk, chunk), 1)
    causal = tc <= tr
    for bi in range(nseq):
        sb = bi * nheads
        if batched:
            rq, rf, ri, rg = rq_ref[bi], rf_ref[bi], ri_ref[bi], rg_ref[bi]
            for h in range(nheads):
                st_ref[sb + h] = s0_ref[bi, h].T
        else:
            rq, rf, ri, rg = rq_ref[...], rf_ref[...], ri_ref[...], rg_ref[...]
        rows = rq.shape[0]
        q = rq * jax.nn.sigmoid(rq)
        fg = lb + (1.0 - lb) * jax.nn.sigmoid(rf)
        logf = jnp.log(fg)
        kk = 1.0 - fg
        h1, h2, h3 = _split3(logf)
        b = (jnp.dot(tri, h1, preferred_element_type=F32)
             + jnp.dot(tri, h2, preferred_element_type=F32)
             + jnp.dot(tri, h3, preferred_element_type=F32))
        q_in = (q * jnp.exp(b)).astype(BF16)
        k_in = (kk * jnp.exp(-b)).astype(BF16)
        v_bf = ri.astype(BF16)
        for c in range(rows // chunk):
            rs = slice(c * chunk, (c + 1) * chunk)
            b_last = b[(c + 1) * chunk - 1:(c + 1) * chunk, :]
            k_st = (kk[rs, :] * jnp.exp(b_last - b[rs, :])).astype(BF16)
            decay = jnp.exp(b_last)
            for h in range(nheads):
                cs = slice(h * dk, (h + 1) * dk)
                qc, kc, vc = q_in[rs, cs], k_in[rs, cs], v_bf[rs, cs]
                att = lax.dot_general(qc, kc, _NT, preferred_element_type=F32)
                att = jnp.where(causal, att, 0.0).astype(BF16)
                st = st_ref[sb + h]
                o_scr[bi, rs, cs] = (
                    jnp.dot(att, vc, preferred_element_type=F32)
                    + lax.dot_general(qc, st.astype(BF16), _NT, preferred_element_type=F32))
                st_ref[sb + h] = st * decay[:, cs] + lax.dot_general(
                    vc, k_st[:, cs], _TA, preferred_element_type=F32)
        gate = rg * jax.nn.sigmoid(rg)
        for h in range(nheads):
            cs = slice(h * dk, (h + 1) * dk)
            o = o_scr[bi, :, cs]
            ms = jnp.mean(o * o, axis=-1, keepdims=True)
            o = (o * lax.rsqrt(ms + EPS) * gn_ref[...]) * gate[:, cs] * beta_ref[:, cs]
            if batched:
                rec_ref[bi, :, cs] = o.astype(rec_ref.dtype)
                sout_ref[bi, h] = st_ref[sb + h].T
            else:
                rec_ref[:, cs] = o.astype(rec_ref.dtype)

    if not batched:
        @pl.when(pl.program_id(0) == last_step)
        def _():
            for h in range(nheads):
                sout_ref[h] = st_ref[h].T


def _tri(rows, chunk):
    r = jnp.arange(rows)
    return ((r[:, None] >= r[None, :]) & (r[:, None] // chunk == r[None, :] // chunk)).astype(BF16)


def _hgrn_attn_body(pt_ref, *refs, n_rec_in, n_att_in, rec_steps, att_steps, rec_kw, att_kw):
    n_in = n_rec_in + n_att_in
    rec_in, att_in = refs[:n_rec_in], refs[n_rec_in:n_in]
    rec_ref, sout_ref, att_ref = refs[n_in:n_in + 3]
    st_ref, o_scr, ks_ref, vs_ref, kn_scr = refs[n_in + 3:]
    nsteps = max(rec_steps, att_steps)
    step = pl.program_id(0)

    def run_rec():
        _hgrn_body(*rec_in, rec_ref, sout_ref, st_ref, o_scr, last_step=rec_steps - 1, **rec_kw)

    def run_att():
        _attn_decode_body(pt_ref, *att_in, att_ref, ks_ref, vs_ref, kn_scr, **att_kw)

    if rec_steps < nsteps:
        pl.when(step < rec_steps)(run_rec)
    else:
        run_rec()
    if att_steps < nsteps:
        pl.when(step < att_steps)(run_att)
    else:
        run_att()


def _hgrn_prompt_attn_decode(proj, lb, rec_gain, rec_beta, page_table, lam, att_gain, att_beta, bmask,
                             proj3, cache_kt, cache_v2, *, nrec, dk, dv, nheads, out_scale):
    t = proj.shape[0]
    gw = nrec * dk
    chunk = math.gcd(t, REC_CHUNK)
    nb, ts, _ = proj3.shape
    npages = page_table.shape[1]
    w, page = cache_kt.shape[1], cache_kt.shape[2]
    nrow = bmask.shape[0]
    tail = LANES
    assert dk == dv == LANES
    assert 8 % ts == 0 and w == nheads * LANES and cache_v2.shape[1:] == (page * nheads, LANES)
    rows = chunk * _pick(t // chunk, max(1, min(4, (t // chunk) // nb)))
    rec_steps, att_steps = t // rows, nb
    nsteps = max(rec_steps, att_steps)

    def rstep(i):
        return jnp.minimum(i, rec_steps - 1) if rec_steps < nsteps else i

    def bstep(i):
        return jnp.minimum(i, att_steps - 1) if att_steps < nsteps else i

    def k_spec(p):
        return pl.BlockSpec((None, w, page), lambda i, pt: (pt[bstep(i), p], 0, 0))

    def v_spec(p):
        return pl.BlockSpec((None, page * nheads, LANES), lambda i, pt: (pt[bstep(i), p], 0, 0))

    rec_specs = [pl.BlockSpec((rows, gw), functools.partial(lambda i, pt, g: (rstep(i), g), g=3 + g))
                 for g in range(4)] + [
        pl.BlockSpec((1, gw), lambda i, pt: (0, 0)),
        pl.BlockSpec((1, dv), lambda i, pt: (0, 0)),
        pl.BlockSpec((1, gw), lambda i, pt: (0, 0)),
        pl.BlockSpec((rows, rows), lambda i, pt: (0, 0)),
    ]
    att_specs = [
        pl.BlockSpec((1, LANES), lambda i, pt: (0, 0)),
        pl.BlockSpec((1, LANES), lambda i, pt: (0, 0)),
        pl.BlockSpec((1, w), lambda i, pt: (0, 0)),
        pl.BlockSpec((nrow, w), lambda i, pt: (0, 0)),
        pl.BlockSpec((1, ts, w), lambda i, pt: (bstep(i), 0, 0)),
        pl.BlockSpec((1, ts, w), lambda i, pt: (bstep(i), 0, 1)),
        pl.BlockSpec((1, ts, w), lambda i, pt: (bstep(i), 0, 2)),
    ] + [k_spec(p) for p in range(npages)] + [v_spec(p) for p in range(npages)]
    grid_spec = pltpu.PrefetchScalarGridSpec(
        num_scalar_prefetch=1,
        grid=(nsteps,),
        in_specs=rec_specs + att_specs,
        out_specs=(pl.BlockSpec((rows, gw), lambda i, pt: (rstep(i), 0)),
                   pl.BlockSpec((nrec, dk, dv), lambda i, pt: (0, 0, 0)),
                   pl.BlockSpec((1, ts, w), lambda i, pt: (bstep(i), 0, 0))),
        scratch_shapes=[
            pltpu.VMEM((nrec, dv, dk), F32),
            pltpu.VMEM((1, rows, gw), F32),
            pltpu.VMEM((w, npages * page), BF16),
            pltpu.VMEM((npages * page + tail, w), BF16),
            pltpu.VMEM((tail, w), BF16),
        ],
    )
    return pl.pallas_call(
        functools.partial(
            _hgrn_attn_body, n_rec_in=len(rec_specs), n_att_in=len(att_specs),
            rec_steps=rec_steps, att_steps=att_steps,
            rec_kw=dict(nheads=nrec, dk=dk, chunk=chunk, batched=False),
            att_kw=dict(npages=npages, page=page, nheads=nheads, ts=ts, out_scale=out_scale)),
        grid_spec=grid_spec,
        out_shape=(jax.ShapeDtypeStruct((t, gw), BF16),
                   jax.ShapeDtypeStruct((nrec, dk, dv), F32),
                   jax.ShapeDtypeStruct((nb, ts, w), BF16)),
        compiler_params=_params("arbitrary"),
        name="hgrn_prompt_attn_decode",
    )(page_table, proj, proj, proj, proj, lb, rec_gain, rec_beta, _tri(rows, chunk),
      lam, att_gain, att_beta, bmask, proj3, proj3, proj3,
      *([cache_kt] * npages), *([cache_v2] * npages))


def _hgrn_decode(proj3, lb, gain, beta, s0, *, nheads, dk, dv):
    nb, ts, _ = proj3.shape
    gw = nheads * dk
    chunk = math.gcd(ts, REC_CHUNK)
    assert dk == dv == LANES and chunk == ts
    nseq = _pick(nb, 4)
    specs = [pl.BlockSpec((nseq, ts, gw), functools.partial(lambda b, g: (b, 0, g), g=3 + g)) for g in range(4)]
    return pl.pallas_call(
        functools.partial(_hgrn_body, nheads=nheads, dk=dk, chunk=chunk, batched=True),
        grid=(nb // nseq,),
        in_specs=specs + [
            pl.BlockSpec((1, gw), lambda b: (0, 0)),
            pl.BlockSpec((1, dv), lambda b: (0, 0)),
            pl.BlockSpec((1, gw), lambda b: (0, 0)),
            pl.BlockSpec((ts, ts), lambda b: (0, 0)),
            pl.BlockSpec((nseq, nheads, dk, dv), lambda b: (b, 0, 0, 0)),
        ],
        out_specs=(pl.BlockSpec((nseq, ts, gw), lambda b: (b, 0, 0)),
                   pl.BlockSpec((nseq, nheads, dk, dv), lambda b: (b, 0, 0, 0))),
        out_shape=(jax.ShapeDtypeStruct((nb, ts, gw), BF16),
                   jax.ShapeDtypeStruct((nb, nheads, dk, dv), F32)),
        scratch_shapes=[pltpu.VMEM((nseq * nheads, dv, dk), F32), pltpu.VMEM((nseq, ts, gw), F32)],
        compiler_params=_params("parallel"),
        name="hgrn_decode",
    )(proj3, proj3, proj3, proj3, lb, gain, beta, _tri(ts, chunk), s0)


def _finish_body(x_ref, att_ref, rec_ref, wo_ref, g_ref, wu_ref, wd_ref, o_ref, hn_ref):
    j = pl.program_id(1)

    @pl.when(j == 0)
    def _():
        aw = att_ref.shape[1]
        h = (x_ref[...]
             + jnp.dot(att_ref[...], wo_ref[0:aw, :], preferred_element_type=F32)
             + jnp.dot(rec_ref[...], wo_ref[aw:, :], preferred_element_type=F32))
        ms = jnp.mean(h * h, axis=-1, keepdims=True)
        hn_ref[...] = (h * lax.rsqrt(ms + EPS) * g_ref[...]).astype(BF16)
        o_ref[...] = h

    u = jnp.maximum(jnp.dot(hn_ref[...], wu_ref[...], preferred_element_type=F32), 0.0)
    o_ref[...] += jnp.dot((u * u).astype(BF16), wd_ref[...], preferred_element_type=F32)


def _finish(x, att, rec, wo_bf, ffn_g, wu_bf, wd_bf):
    m, d = x.shape
    aw, rw = att.shape[1], rec.shape[1]
    ff = wu_bf.shape[1]
    tm = _pick(m, 512)
    fc = _pick(ff, 1024)
    return pl.pallas_call(
        _finish_body,
        grid=(m // tm, ff // fc),
        in_specs=[
            pl.BlockSpec((tm, d), lambda i, j: (i, 0)),
            pl.BlockSpec((tm, aw), lambda i, j: (i, 0)),
            pl.BlockSpec((tm, rw), lambda i, j: (i, 0)),
            pl.BlockSpec((aw + rw, d), lambda i, j: (0, 0), pipeline_mode=pl.Buffered(1)),
            pl.BlockSpec((1, d), lambda i, j: (0, 0)),
            pl.BlockSpec((d, fc), lambda i, j: (0, j)),
            pl.BlockSpec((fc, d), lambda i, j: (j, 0)),
        ],
        out_specs=pl.BlockSpec((tm, d), lambda i, j: (i, 0)),
        out_shape=jax.ShapeDtypeStruct((m, d), F32),
        scratch_shapes=[pltpu.VMEM((tm, d), BF16)],
        compiler_params=_params("parallel", "arbitrary"),
        name="finish",
    )(x, att, rec, wo_bf, ffn_g, wu_bf, wd_bf)


def _rope_tables(pos, head_dim):
    rot = head_dim // 4
    half = rot // 2
    inv = ROPE_THETA ** (-jnp.arange(half, dtype=F32) * 2.0 / rot)
    ang = pos.astype(F32)[:, None] * inv[None, :]
    cos, sin = jnp.cos(ang), jnp.sin(ang)
    ones = jnp.ones((pos.shape[0], head_dim - rot), F32)
    zeros = jnp.zeros((pos.shape[0], head_dim - rot), F32)
    zh = jnp.zeros_like(sin)
    c = jnp.concatenate([cos, cos, ones], axis=1)
    sa = jnp.concatenate([-sin, zh, zeros], axis=1)
    sb = jnp.concatenate([zh, sin, zeros], axis=1)
    reps = LANES // head_dim
    return tuple(jnp.tile(a, (1, reps)) for a in (c, sa, sb))


def kernel(x_prompt, x_sample, cache_k, cache_v, state_rec, page_table, attn_norm, w_in, q_norm, k_norm,
           lambda_q1, lambda_k1, lambda_q2, lambda_k2, att_out_norm, rec_lb_logits, rec_out_norm,
           beta_att, beta_rec, w_out, ffn_norm, w_up, w_down):
    bp, tp, d = x_prompt.shape
    bs, ts, _ = x_sample.shape
    depth = w_in.shape[0]
    n_pool, page, nheads, _, head_dim = cache_k.shape[1:]
    vd = cache_v.shape[-1]
    nrec, dk, dv = state_rec.shape[2:]
    npages = page_table.shape[1]
    past = npages * page
    aw = nheads * vd
    gw = nrec * dk
    assert bp == 1 and 2 * head_dim == vd == LANES and aw == gw and w_in.shape[2] == 7 * gw

    lam_inits = tuple(0.8 - 0.6 * math.exp(-0.3 * l) for l in range(depth))
    lam_all, lb_all = _prep(lambda_q1, lambda_k1, lambda_q2, lambda_k2, rec_lb_logits, lam_inits)

    lane_map = jnp.arange(MXU_TILE) // head_dim
    gmat = (lane_map[:, None] == lane_map[None, :]).astype(BF16)
    rope_p = _rope_tables(jnp.arange(tp), head_dim)
    rope_s = tuple(jnp.tile(a, (bs, 1)) for a in _rope_tables(past + jnp.arange(ts), head_dim))
    srow = jnp.arange(2 * nheads * 8)
    bmask = ((srow[:, None] // (nheads * 8) == (jnp.arange(aw)[None, :] // head_dim) % 2)
             & ((srow[:, None] // 8) % nheads == jnp.arange(aw)[None, :] // vd)).astype(F32)

    hp = x_prompt.reshape(tp, d)
    hs = x_sample.reshape(bs * ts, d)
    outs = [[] for _ in range(6)]
    for l in range(depth):
        w_in_bf = w_in[l].astype(BF16)
        w_out_bf = w_out[l].astype(BF16)
        w_up_bf = w_up[l].astype(BF16)
        w_down_bf = w_down[l].astype(BF16)
        qk_gain = jnp.stack([jnp.tile(q_norm[l], aw // head_dim),
                             jnp.tile(k_norm[l], aw // head_dim)]).reshape(2, 1, aw)
        norm_g = attn_norm[l].reshape(1, d)
        lam = lam_all[l:l + 1]
        lb = lb_all[l:l + 1]
        att_g = att_out_norm[l].reshape(1, vd)
        rec_g = rec_out_norm[l].reshape(1, dv)
        b_att = beta_att[l].reshape(1, aw)
        b_rec = beta_rec[l].reshape(1, gw)
        ffn_g = ffn_norm[l].reshape(1, d)
        out_scale = 1.0 - lam_inits[l]

        proj = _proj(hp, norm_g, w_in_bf, qk_gain, gmat, *rope_p, head_dim=head_dim)
        proj_s = _proj(hs, norm_g, w_in_bf, qk_gain, gmat, *rope_s, head_dim=head_dim)
        proj3 = proj_s.reshape(bs, ts, 7 * gw)
        cache_kt = jnp.transpose(cache_k[l], (0, 2, 3, 4, 1)).reshape(n_pool, aw, page)
        cache_v2 = cache_v[l].reshape(n_pool, page * nheads, vd)
        rec, s_new, att_s = _hgrn_prompt_attn_decode(
            proj, lb, rec_g, b_rec, page_table, lam, att_g, b_att, bmask, proj3, cache_kt, cache_v2,
            nrec=nrec, dk=dk, dv=dv, nheads=nheads, out_scale=out_scale)

        att, k_t, v_p = _attn_prompt(proj, lam, att_g, b_att, nheads=nheads, out_scale=out_scale)
        hp = _finish(hp, att, rec, w_out_bf, ffn_g, w_up_bf, w_down_bf)
        outs[0].append(jnp.transpose(k_t.reshape(nheads, 2, head_dim, tp), (3, 0, 1, 2))
                       .reshape(bp, tp, nheads, 2, head_dim))
        outs[1].append(v_p.reshape(bp, tp, nheads, vd))
        outs[2].append(s_new.reshape(bp, nrec, dk, dv).astype(state_rec.dtype))

        rec_s, s_new = _hgrn_decode(proj3, lb, rec_g, b_rec, state_rec[l], nheads=nrec, dk=dk, dv=dv)
        hs = _finish(hs, att_s.reshape(bs * ts, aw), rec_s.reshape(bs * ts, gw),
                     w_out_bf, ffn_g, w_up_bf, w_down_bf)
        outs[3].append(proj_s[:, aw:2 * aw].reshape(bs, ts, nheads, 2, head_dim))
        outs[4].append(proj_s[:, 2 * aw:3 * aw].reshape(bs, ts, nheads, vd))
        outs[5].append(s_new.astype(state_rec.dtype))

    return (hp.reshape(bp, tp, d), hs.reshape(bs, ts, d),
            jnp.stack(outs[0]), jnp.stack(outs[1]), jnp.stack(outs[2]),
            jnp.stack(outs[3]), jnp.stack(outs[4]), jnp.stack(outs[5]))
```

```python
import functools
import math

import jax
import jax.numpy as jnp
from jax import lax
from jax.experimental import pallas as pl
from jax.experimental.pallas import tpu as pltpu

F32 = jnp.float32
BF16 = jnp.bfloat16

EPS = 1e-6
ROPE_THETA = 500000.0
MASK_VALUE = -1e30
LANES = 128
MXU_TILE = 256
LOG2E = 1.4426950408889634
MAX_FIXED_OFFSET = 60.0
SCORE_BOUND_SLACK = 1.0 + 2.0 ** -6
VMEM_LIMIT_BYTES = 56 * 1024 * 1024
REC_CHUNK = 64

_NT = (((1,), (1,)), ((), ()))
_TA = (((0,), (0,)), ((), ()))


def _params(*semantics):
    return pltpu.CompilerParams(dimension_semantics=semantics, vmem_limit_bytes=VMEM_LIMIT_BYTES)


def _pick(n, pref):
    if n <= pref:
        return n
    t = pref
    while n % t:
        t //= 2
    return t


def _split3(x):
    h1 = x.astype(BF16)
    r1 = x - h1.astype(F32)
    h2 = r1.astype(BF16)
    h3 = (r1 - h2.astype(F32)).astype(BF16)
    return h1, h2, h3


def _prep_body(lq1_ref, lk1_ref, lq2_ref, lk2_ref, logit_ref, lam_ref, lb_ref, *, lam_inits):
    for l, lam_init in enumerate(lam_inits):
        s1 = jnp.sum(lq1_ref[l:l + 1, :] * lk1_ref[l:l + 1, :], axis=1, keepdims=True)
        s2 = jnp.sum(lq2_ref[l:l + 1, :] * lk2_ref[l:l + 1, :], axis=1, keepdims=True)
        lam = jnp.exp(s1) - jnp.exp(s2) + lam_init
        lam_ref[l:l + 1, :] = jnp.broadcast_to(lam, (1, LANES))
    x = logit_ref[...]
    e = jnp.exp(x - jnp.max(x, axis=0, keepdims=True))
    sm = e / jnp.sum(e, axis=0, keepdims=True)
    acc = jnp.zeros((1, x.shape[1]), F32)
    for r in range(x.shape[0]):
        acc = acc + sm[r:r + 1, :]
        lb_ref[r:r + 1, :] = acc


def _prep(lq1, lk1, lq2, lk2, logits, lam_inits):
    depth = lq1.shape[0]
    return pl.pallas_call(
        functools.partial(_prep_body, lam_inits=lam_inits),
        out_shape=(jax.ShapeDtypeStruct((depth, LANES), F32),
                   jax.ShapeDtypeStruct(logits.shape, F32)),
        name="prep",
    )(lq1, lk1, lq2, lk2, logits)


def _proj_body(x_ref, g_ref, w_ref, qkg_ref, gmat_ref, cos_ref, sa_ref, sb_ref, o_ref, xn_ref,
               *, head_dim, rot_half):
    j = pl.program_id(1)

    @pl.when(j == 0)
    def _():
        x = x_ref[...]
        ms = jnp.mean(x * x, axis=-1, keepdims=True)
        xn_ref[...] = (x * lax.rsqrt(ms + EPS) * g_ref[...]).astype(BF16)

    @pl.when(j < 2)
    def _():
        y = jnp.dot(xn_ref[...], w_ref[...], preferred_element_type=F32)
        gm = gmat_ref[...]
        gt = gm.shape[0]
        gain = qkg_ref[0]
        c, sa, sb = cos_ref[...], sa_ref[...], sb_ref[...]
        for t in range(y.shape[1] // gt):
            yt = y[:, t * gt:(t + 1) * gt]
            y2 = yt * yt
            hi = y2.astype(BF16)
            lo = (y2 - hi.astype(F32)).astype(BF16)
            ss = (jnp.dot(hi, gm, preferred_element_type=F32)
                  + jnp.dot(lo, gm, preferred_element_type=F32))
            yn = yt * lax.rsqrt(ss * (1.0 / head_dim) + EPS) * gain[:, t * gt:(t + 1) * gt]
            for u in range(gt // LANES):
                yu = yn[:, u * LANES:(u + 1) * LANES]
                lane0 = t * gt + u * LANES
                o_ref[:, lane0:lane0 + LANES] = (
                    yu * c + pltpu.roll(yu, LANES - rot_half, 1) * sa + pltpu.roll(yu, rot_half, 1) * sb)

    @pl.when(j >= 2)
    def _():
        o_ref[...] = jnp.dot(xn_ref[...], w_ref[...], preferred_element_type=F32)


def _proj(x, norm_g, w_bf, qk_gain, gmat, cos_t, sa_t, sb_t, *, head_dim):
    m, d = x.shape
    gw = qk_gain.shape[2]
    gt = gmat.shape[0]
    ngroups = w_bf.shape[1] // gw
    tm = _pick(m, 1024)
    assert gw % gt == 0 and gt % LANES == 0
    return pl.pallas_call(
        functools.partial(_proj_body, head_dim=head_dim, rot_half=head_dim // 8),
        grid=(m // tm, ngroups),
        in_specs=[
            pl.BlockSpec((tm, d), lambda i, j: (i, 0)),
            pl.BlockSpec((1, d), lambda i, j: (0, 0)),
            pl.BlockSpec((d, gw), lambda i, j: (0, j)),
            pl.BlockSpec((1, 1, gw), lambda i, j: (jnp.minimum(j, 1), 0, 0)),
            pl.BlockSpec((gt, gt), lambda i, j: (0, 0)),
            pl.BlockSpec((tm, LANES), lambda i, j: (i, 0)),
            pl.BlockSpec((tm, LANES), lambda i, j: (i, 0)),
            pl.BlockSpec((tm, LANES), lambda i, j: (i, 0)),
        ],
        out_specs=pl.BlockSpec((tm, gw), lambda i, j: (i, j)),
        out_shape=jax.ShapeDtypeStruct((m, ngroups * gw), F32),
        scratch_shapes=[pltpu.VMEM((tm, d), BF16)],
        compiler_params=_params("parallel", "arbitrary"),
        name="proj",
    )(x, norm_g, w_bf, qk_gain, gmat, cos_t, sa_t, sb_t)


def _attn_body(lam_ref, q_ref, k_ref, v_ref, g_ref, beta_ref, o_ref, kt_ref, vo_ref,
               kb_ref, vb_ref, qs_ref, m_ref, l_ref, acc_ref, kmax_ref, *, tq, wide, out_scale):
    qi = pl.program_id(1)
    half = LANES // 2
    lane = lax.broadcasted_iota(jnp.int32, (1, LANES), 1)

    @pl.when(qi == 0)
    def _():
        kb = k_ref[...].astype(BF16)
        kb_ref[...] = kb
        vb_ref[...] = v_ref[...].astype(BF16)
        vo_ref[...] = v_ref[...]
        for i in range(k_ref.shape[0] // tq):
            kt_ref[:, i * tq:(i + 1) * tq] = k_ref[i * tq:(i + 1) * tq, :].T
        k2 = kb.astype(F32)
        k2 = k2 * k2
        for c, sel in enumerate((lane < half, lane >= half)):
            n2 = jnp.sum(jnp.where(sel, k2, 0.0), axis=1, keepdims=True)
            kmax_ref[c:c + 1, :] = jnp.broadcast_to(
                jnp.sqrt(jnp.max(n2, axis=0, keepdims=True)), (1, LANES))

    q = q_ref[...] * (LOG2E / math.sqrt(half))
    qs = (jnp.where(lane < half, q, 0.0).astype(BF16), jnp.where(lane >= half, q, 0.0).astype(BF16))
    bounds = []
    for c in range(2):
        qs_ref[c * tq:(c + 1) * tq, :] = qs[c]
        qf = qs[c].astype(F32)
        qn = jnp.sqrt(jnp.sum(qf * qf, axis=1, keepdims=True))
        bounds.append(qn * kmax_ref[c:c + 1, :] * SCORE_BOUND_SLACK)
    fast = jnp.max(jnp.maximum(bounds[0], bounds[1])) <= MAX_FIXED_OFFSET
    for c in range(2):
        m_ref[c * tq:(c + 1) * tq, :] = jnp.where(fast, bounds[c], MASK_VALUE)
    l_ref[...] = jnp.zeros(l_ref.shape, F32)
    acc_ref[...] = jnp.zeros(acc_ref.shape, F32)

    def block(off, width, diag_at, fixed_offset):
        kblk = kb_ref[pl.ds(off, width), :]
        vblk = vb_ref[pl.ds(off, width), :]
        for c in range(2):
            rs = slice(c * tq, (c + 1) * tq)
            s = lax.dot_general(qs_ref[rs, :], kblk, _NT, preferred_element_type=F32)
            if diag_at is not None:
                row = lax.broadcasted_iota(jnp.int32, (tq, tq), 0)
                col = lax.broadcasted_iota(jnp.int32, (tq, tq), 1)
                tail = jnp.where(col <= row, s[:, diag_at:], MASK_VALUE)
                s = tail if diag_at == 0 else jnp.concatenate([s[:, :diag_at], tail], axis=1)
            m_prev = m_ref[rs, :]
            if fixed_offset:
                p = jnp.exp2(s - jnp.tile(m_prev, (1, width // LANES)))
                l_ref[rs, :] = l_ref[rs, :] + jnp.sum(p, axis=1, keepdims=True)
                acc_ref[rs, :] = acc_ref[rs, :] + jnp.dot(
                    p.astype(BF16), vblk, preferred_element_type=F32)
            else:
                m_next = jnp.maximum(m_prev, jnp.max(s, axis=1, keepdims=True))
                alpha = jnp.exp2(m_prev - m_next)
                p = jnp.exp2(s - jnp.tile(m_next, (1, width // LANES)))
                l_ref[rs, :] = alpha * l_ref[rs, :] + jnp.sum(p, axis=1, keepdims=True)
                acc_ref[rs, :] = alpha * acc_ref[rs, :] + jnp.dot(
                    p.astype(BF16), vblk, preferred_element_type=F32)
                m_ref[rs, :] = m_next

    per_wide = wide // tq
    nwide = qi // per_wide
    rem = qi - nwide * per_wide

    def all_blocks(fixed_offset):
        def wide_body(j, carry):
            block(pl.multiple_of(j * wide, wide), wide, None, fixed_offset)
            return carry

        lax.fori_loop(0, nwide, wide_body, 0)
        for r in range(per_wide):
            @pl.when(rem == r)
            def _(r=r):
                block(pl.multiple_of(nwide * wide, wide), (r + 1) * tq, r * tq, fixed_offset)

    pl.when(fast)(functools.partial(all_blocks, True))
    pl.when(jnp.logical_not(fast))(functools.partial(all_blocks, False))

    a = acc_ref[...] / l_ref[...]
    o = a[0:tq, :] - lam_ref[...] * a[tq:2 * tq, :]
    ms = jnp.mean(o * o, axis=-1, keepdims=True)
    o_ref[...] = (o * lax.rsqrt(ms + EPS) * g_ref[...] * out_scale * beta_ref[...]).astype(o_ref.dtype)


def _attn_prompt(proj, lam, gain, beta, *, nheads, out_scale):
    t = proj.shape[0]
    tq = _pick(t, 512)
    wide = _pick(t, 4 * tq)
    assert tq % LANES == 0 and wide % tq == 0
    return pl.pallas_call(
        functools.partial(_attn_body, tq=tq, wide=wide, out_scale=out_scale),
        grid=(nheads, t // tq),
        in_specs=[
            pl.BlockSpec((1, LANES), lambda h, i: (0, 0)),
            pl.BlockSpec((tq, LANES), lambda h, i: (i, h)),
            pl.BlockSpec((t, LANES), lambda h, i: (0, nheads + h)),
            pl.BlockSpec((t, LANES), lambda h, i: (0, 2 * nheads + h)),
            pl.BlockSpec((1, LANES), lambda h, i: (0, 0)),
            pl.BlockSpec((1, LANES), lambda h, i: (0, h)),
        ],
        out_specs=(pl.BlockSpec((tq, LANES), lambda h, i: (i, h)),
                   pl.BlockSpec((LANES, t), lambda h, i: (h, 0)),
                   pl.BlockSpec((t, LANES), lambda h, i: (0, h))),
        out_shape=(jax.ShapeDtypeStruct((t, nheads * LANES), BF16),
                   jax.ShapeDtypeStruct((nheads * LANES, t), F32),
                   jax.ShapeDtypeStruct((t, nheads * LANES), F32)),
        scratch_shapes=[
            pltpu.VMEM((t, LANES), BF16),
            pltpu.VMEM((t, LANES), BF16),
            pltpu.VMEM((2 * tq, LANES), BF16),
            pltpu.VMEM((2 * tq, LANES), F32),
            pltpu.VMEM((2 * tq, LANES), F32),
            pltpu.VMEM((2 * tq, LANES), F32),
            pltpu.VMEM((2, LANES), F32),
        ],
        compiler_params=_params("parallel", "arbitrary"),
        name="attn_prompt",
    )(lam, proj, proj, proj, gain, beta)


def _attn_decode_body(pt_ref, lam_ref, g_ref, beta_ref, bmask_ref, q_ref, kn_ref, vn_ref, *rest,
                      npages, page, nheads, ts, out_scale):
    del pt_ref
    kp_refs = rest[:npages]
    vp_refs = rest[npages:2 * npages]
    o_ref, ks_ref, vs_ref, kn_scr = rest[2 * npages:]
    past = npages * page
    tail, w = kn_scr.shape
    nrow = bmask_ref.shape[0]
    half = LANES // 2

    for p in range(npages):
        ks_ref[:, p * page:(p + 1) * page] = kp_refs[p][...].astype(BF16)
        for h in range(nheads):
            vs_ref[p * page:(p + 1) * page, h * LANES:(h + 1) * LANES] = (
                vp_refs[p][pl.ds(h, page, stride=nheads), :].astype(BF16))
    kn_scr[...] = jnp.zeros((tail, w), BF16)
    kn_scr[0:ts, :] = kn_ref[0].astype(BF16)
    vs_ref[past:, :] = jnp.zeros((tail, w), BF16)
    vs_ref[past:past + ts, :] = vn_ref[0].astype(BF16)

    q = q_ref[0] * (LOG2E / math.sqrt(half))
    qt = (jnp.tile(q, (nrow // ts, 1)) * bmask_ref[...]).astype(BF16)
    s_past = jnp.dot(qt, ks_ref[...], preferred_element_type=F32)
    s_new = lax.dot_general(qt, kn_scr[...], _NT, preferred_element_type=F32)
    row = lax.broadcasted_iota(jnp.int32, s_new.shape, 0)
    col = lax.broadcasted_iota(jnp.int32, s_new.shape, 1)
    s_new = jnp.where(col <= lax.rem(row, ts), s_new, MASK_VALUE)
    m = jnp.maximum(jnp.max(s_past, axis=1, keepdims=True), jnp.max(s_new, axis=1, keepdims=True))
    p_past = jnp.exp2(s_past - m)
    p_new = jnp.exp2(s_new - m)
    l = jnp.sum(p_past, axis=1, keepdims=True) + jnp.sum(p_new, axis=1, keepdims=True)
    row1 = lax.broadcasted_iota(jnp.int32, (nrow, 1), 0)
    hr = nrow // 2
    wgt = jnp.where(row1 < hr, 1.0, -lam_ref[:, 0:1]) / l
    pw_past = p_past * wgt
    pw_new = p_new * wgt
    a_past = (pw_past[0:hr, :] + pw_past[hr:, :]).astype(BF16)
    a_new = (pw_new[0:hr, :] + pw_new[hr:, :]).astype(BF16)
    o2 = (jnp.dot(a_past, vs_ref[0:past, :], preferred_element_type=F32)
          + jnp.dot(a_new, vs_ref[past:, :], preferred_element_type=F32))
    for h in range(nheads):
        cs = slice(h * LANES, (h + 1) * LANES)
        o = o2[h * 8:(h + 1) * 8, cs]
        ms = jnp.mean(o * o, axis=-1, keepdims=True)
        o = o * lax.rsqrt(ms + EPS) * g_ref[...] * out_scale * beta_ref[:, cs]
        o_ref[0, :, cs] = o[0:ts, :].astype(o_ref.dtype)


def _hgrn_body(rq_ref, rf_ref, ri_ref, rg_ref, lb_ref, gn_ref, beta_ref, tri_ref, *rest,
               nheads, dk, chunk, batched):
    if batched:
        s0_ref, rec_ref, sout_ref, st_ref, o_scr = rest
        nseq = rq_ref.shape[0]
    else:
        rec_ref, st_ref, o_scr = rest
        nseq = 1

    lb = lb_ref[...]
    tri = tri_ref[...]
    tr = lax.broadcasted_iota(jnp.int32, (chunk, chunk), 0)
    tc = lax.broadcasted_iota(jnp.int32, (chunk, chunk), 1)
    causal = tc <= tr
    for bi in range(nseq):
        sb = bi * nheads
        if batched:
            rq, rf, ri, rg = rq_ref[bi], rf_ref[bi], ri_ref[bi], rg_ref[bi]
            for h in range(nheads):
                st_ref[sb + h] = s0_ref[bi, h].T
        else:
            rq, rf, ri, rg = rq_ref[...], rf_ref[...], ri_ref[...], rg_ref[...]
        rows = rq.shape[0]
        q = rq * jax.nn.sigmoid(rq)
        fg = lb + (1.0 - lb) * jax.nn.sigmoid(rf)
        logf = jnp.log(fg)
        kk = 1.0 - fg
        h1, h2, h3 = _split3(logf)
        b = (jnp.dot(tri, h1, preferred_element_type=F32)
             + jnp.dot(tri, h2, preferred_element_type=F32)
             + jnp.dot(tri, h3, preferred_element_type=F32))
        q_in = (q * jnp.exp(b)).astype(BF16)
        k_in = (kk * jnp.exp(-b)).astype(BF16)
        v_bf = ri.astype(BF16)
        for c in range(rows // chunk):
            rs = slice(c * chunk, (c + 1) * chunk)
            b_last = b[(c + 1) * chunk - 1:(c + 1) * chunk, :]
            k_st = (kk[rs, :] * jnp.exp(b_last - b[rs, :])).astype(BF16)
            decay = jnp.exp(b_last)
            for h in range(nheads):
                cs = slice(h * dk, (h + 1) * dk)
                qc, kc, vc = q_in[rs, cs], k_in[rs, cs], v_bf[rs, cs]
                att = lax.dot_general(qc, kc, _NT, preferred_element_type=F32)
                att = jnp.where(causal, att, 0.0).astype(BF16)
                st = st_ref[sb + h]
                o_scr[bi, rs, cs] = (
                    jnp.dot(att, vc, preferred_element_type=F32)
                    + lax.dot_general(qc, st.astype(BF16), _NT, preferred_element_type=F32))
                st_ref[sb + h] = st * decay[:, cs] + lax.dot_general(
                    vc, k_st[:, cs], _TA, preferred_element_type=F32)
        gate = rg * jax.nn.sigmoid(rg)
        for h in range(nheads):
            cs = slice(h * dk, (h + 1) * dk)
            o = o_scr[bi, :, cs]
            ms = jnp.mean(o * o, axis=-1, keepdims=True)
            o = (o * lax.rsqrt(ms + EPS) * gn_ref[...]) * gate[:, cs] * beta_ref[:, cs]
            if batched:
                rec_ref[bi, :, cs] = o.astype(rec_ref.dtype)
                sout_ref[bi, h] = st_ref[sb + h].T
            else:
                rec_ref[:, cs] = o.astype(rec_ref.dtype)


def _tri(rows, chunk):
    r = jnp.arange(rows)
    return ((r[:, None] >= r[None, :]) & (r[:, None] // chunk == r[None, :] // chunk)).astype(BF16)


def _hgrn_attn_body(pt_ref, *refs, n_rec_in, n_att_in, rec_steps, att_steps, rec_kw, att_kw):
    n_in = n_rec_in + n_att_in
    rec_in, att_in = refs[:n_rec_in], refs[n_rec_in:n_in]
    rec_ref, sout_ref, att_ref = refs[n_in:n_in + 3]
    st_ref, o_scr, ks_ref, vs_ref, kn_scr = refs[n_in + 3:]
    nsteps = max(rec_steps, att_steps)
    step = pl.program_id(0)

    @pl.when(step == 0)
    def _():
        st_ref[...] = jnp.zeros(st_ref.shape, F32)

    def run_rec():
        _hgrn_body(*rec_in, rec_ref, st_ref, o_scr, **rec_kw)

    def run_att():
        _attn_decode_body(pt_ref, *att_in, att_ref, ks_ref, vs_ref, kn_scr, **att_kw)

    if rec_steps < nsteps:
        pl.when(step < rec_steps)(run_rec)
    else:
        run_rec()
    if att_steps < nsteps:
        pl.when(step < att_steps)(run_att)
    else:
        run_att()

    @pl.when(step == rec_steps - 1)
    def _():
        for h in range(sout_ref.shape[0]):
            sout_ref[h] = st_ref[h].T


def _hgrn_prompt_attn_decode(proj, lb, rec_gain, rec_beta, page_table, lam, att_gain, att_beta, bmask,
                             proj3, cache_kt, cache_v2, *, nrec, dk, dv, nheads, out_scale):
    t = proj.shape[0]
    gw = nrec * dk
    chunk = math.gcd(t, REC_CHUNK)
    nb, ts, _ = proj3.shape
    npages = page_table.shape[1]
    w, page = cache_kt.shape[1], cache_kt.shape[2]
    nrow = bmask.shape[0]
    tail = LANES
    assert dk == dv == LANES
    assert 8 % ts == 0 and w == nheads * LANES and cache_v2.shape[1:] == (page * nheads, LANES)
    rows = chunk * _pick(t // chunk, max(1, min(4, (t // chunk) // nb)))
    rec_steps, att_steps = t // rows, nb
    nsteps = max(rec_steps, att_steps)

    def rstep(i):
        return jnp.minimum(i, rec_steps - 1) if rec_steps < nsteps else i

    def bstep(i):
        return jnp.minimum(i, att_steps - 1) if att_steps < nsteps else i

    def k_spec(p):
        return pl.BlockSpec((None, w, page), lambda i, pt: (pt[bstep(i), p], 0, 0))

    def v_spec(p):
        return pl.BlockSpec((None, page * nheads, LANES), lambda i, pt: (pt[bstep(i), p], 0, 0))

    rec_specs = [pl.BlockSpec((rows, gw), functools.partial(lambda i, pt, g: (rstep(i), g), g=3 + g))
                 for g in range(4)] + [
        pl.BlockSpec((1, gw), lambda i, pt: (0, 0)),
        pl.BlockSpec((1, dv), lambda i, pt: (0, 0)),
        pl.BlockSpec((1, gw), lambda i, pt: (0, 0)),
        pl.BlockSpec((rows, rows), lambda i, pt: (0, 0)),
    ]
    att_specs = [
        pl.BlockSpec((1, LANES), lambda i, pt: (0, 0)),
        pl.BlockSpec((1, LANES), lambda i, pt: (0, 0)),
        pl.BlockSpec((1, w), lambda i, pt: (0, 0)),
        pl.BlockSpec((nrow, w), lambda i, pt: (0, 0)),
        pl.BlockSpec((1, ts, w), lambda i, pt: (bstep(i), 0, 0)),
        pl.BlockSpec((1, ts, w), lambda i, pt: (bstep(i), 0, 1)),
        pl.BlockSpec((1, ts, w), lambda i, pt: (bstep(i), 0, 2)),
    ] + [k_spec(p) for p in range(npages)] + [v_spec(p) for p in range(npages)]
    grid_spec = pltpu.PrefetchScalarGridSpec(
        num_scalar_prefetch=1,
        grid=(nsteps,),
        in_specs=rec_specs + att_specs,
        out_specs=(pl.BlockSpec((rows, gw), lambda i, pt: (rstep(i), 0)),
                   pl.BlockSpec((nrec, dk, dv), lambda i, pt: (0, 0, 0)),
                   pl.BlockSpec((1, ts, w), lambda i, pt: (bstep(i), 0, 0))),
        scratch_shapes=[
            pltpu.VMEM((nrec, dv, dk), F32),
            pltpu.VMEM((1, rows, gw), F32),
            pltpu.VMEM((w, npages * page), BF16),
            pltpu.VMEM((npages * page + tail, w), BF16),
            pltpu.VMEM((tail, w), BF16),
        ],
    )
    return pl.pallas_call(
        functools.partial(
            _hgrn_attn_body, n_rec_in=len(rec_specs), n_att_in=len(att_specs),
            rec_steps=rec_steps, att_steps=att_steps,
            rec_kw=dict(nheads=nrec, dk=dk, chunk=chunk, batched=False),
            att_kw=dict(npages=npages, page=page, nheads=nheads, ts=ts, out_scale=out_scale)),
        grid_spec=grid_spec,
        out_shape=(jax.ShapeDtypeStruct((t, gw), BF16),
                   jax.ShapeDtypeStruct((nrec, dk, dv), F32),
                   jax.ShapeDtypeStruct((nb, ts, w), BF16)),
        compiler_params=_params("arbitrary"),
        name="hgrn_prompt_attn_decode",
    )(page_table, proj, proj, proj, proj, lb, rec_gain, rec_beta, _tri(rows, chunk),
      lam, att_gain, att_beta, bmask, proj3, proj3, proj3,
      *([cache_kt] * npages), *([cache_v2] * npages))


def _hgrn_decode(proj3, lb, gain, beta, s0, *, nheads, dk, dv):
    nb, ts, _ = proj3.shape
    gw = nheads * dk
    chunk = math.gcd(ts, REC_CHUNK)
    assert dk == dv == LANES and chunk == ts
    nseq = _pick(nb, 4)
    specs = [pl.BlockSpec((nseq, ts, gw), functools.partial(lambda b, g: (b, 0, g), g=3 + g)) for g in range(4)]
    return pl.pallas_call(
        functools.partial(_hgrn_body, nheads=nheads, dk=dk, chunk=chunk, batched=True),
        grid=(nb // nseq,),
        in_specs=specs + [
            pl.BlockSpec((1, gw), lambda b: (0, 0)),
            pl.BlockSpec((1, dv), lambda b: (0, 0)),
            pl.BlockSpec((1, gw), lambda b: (0, 0)),
            pl.BlockSpec((ts, ts), lambda b: (0, 0)),
            pl.BlockSpec((nseq, nheads, dk, dv), lambda b: (b, 0, 0, 0)),
        ],
        out_specs=(pl.BlockSpec((nseq, ts, gw), lambda b: (b, 0, 0)),
                   pl.BlockSpec((nseq, nheads, dk, dv), lambda b: (b, 0, 0, 0))),
        out_shape=(jax.ShapeDtypeStruct((nb, ts, gw), BF16),
                   jax.ShapeDtypeStruct((nb, nheads, dk, dv), F32)),
        scratch_shapes=[pltpu.VMEM((nseq * nheads, dv, dk), F32), pltpu.VMEM((nseq, ts, gw), F32)],
        compiler_params=_params("parallel"),
        name="hgrn_decode",
    )(proj3, proj3, proj3, proj3, lb, gain, beta, _tri(ts, chunk), s0)


def _finish_body(x_ref, att_ref, rec_ref, wo_ref, g_ref, wu_ref, wd_ref, o_ref, hn_ref):
    j = pl.program_id(1)

    @pl.when(j == 0)
    def _():
        aw = att_ref.shape[1]
        h = (x_ref[...]
             + jnp.dot(att_ref[...], wo_ref[0:aw, :], preferred_element_type=F32)
             + jnp.dot(rec_ref[...], wo_ref[aw:, :], preferred_element_type=F32))
        ms = jnp.mean(h * h, axis=-1, keepdims=True)
        hn_ref[...] = (h * lax.rsqrt(ms + EPS) * g_ref[...]).astype(BF16)
        o_ref[...] = h

    u = jnp.maximum(jnp.dot(hn_ref[...], wu_ref[...], preferred_element_type=F32), 0.0)
    o_ref[...] += jnp.dot((u * u).astype(BF16), wd_ref[...], preferred_element_type=F32)


def _finish(x, att, rec, wo_bf, ffn_g, wu_bf, wd_bf):
    m, d = x.shape
    aw, rw = att.shape[1], rec.shape[1]
    ff = wu_bf.shape[1]
    tm = _pick(m, 512)
    fc = _pick(ff, 1024)
    return pl.pallas_call(
        _finish_body,
        grid=(m // tm, ff // fc),
        in_specs=[
            pl.BlockSpec((tm, d), lambda i, j: (i, 0)),
            pl.BlockSpec((tm, aw), lambda i, j: (i, 0)),
            pl.BlockSpec((tm, rw), lambda i, j: (i, 0)),
            pl.BlockSpec((aw + rw, d), lambda i, j: (0, 0), pipeline_mode=pl.Buffered(1)),
            pl.BlockSpec((1, d), lambda i, j: (0, 0)),
            pl.BlockSpec((d, fc), lambda i, j: (0, j)),
            pl.BlockSpec((fc, d), lambda i, j: (j, 0)),
        ],
        out_specs=pl.BlockSpec((tm, d), lambda i, j: (i, 0)),
        out_shape=jax.ShapeDtypeStruct((m, d), F32),
        scratch_shapes=[pltpu.VMEM((tm, d), BF16)],
        compiler_params=_params("parallel", "arbitrary"),
        name="finish",
    )(x, att, rec, wo_bf, ffn_g, wu_bf, wd_bf)


def _rope_tables(pos, head_dim):
    rot = head_dim // 4
    half = rot // 2
    inv = ROPE_THETA ** (-jnp.arange(half, dtype=F32) * 2.0 / rot)
    ang = pos.astype(F32)[:, None] * inv[None, :]
    cos, sin = jnp.cos(ang), jnp.sin(ang)
    ones = jnp.ones((pos.shape[0], head_dim - rot), F32)
    zeros = jnp.zeros((pos.shape[0], head_dim - rot), F32)
    zh = jnp.zeros_like(sin)
    c = jnp.concatenate([cos, cos, ones], axis=1)
    sa = jnp.concatenate([-sin, zh, zeros], axis=1)
    sb = jnp.concatenate([zh, sin, zeros], axis=1)
    reps = LANES // head_dim
    return tuple(jnp.tile(a, (1, reps)) for a in (c, sa, sb))


def kernel(x_prompt, x_sample, cache_k, cache_v, state_rec, page_table, attn_norm, w_in, q_norm, k_norm,
           lambda_q1, lambda_k1, lambda_q2, lambda_k2, att_out_norm, rec_lb_logits, rec_out_norm,
           beta_att, beta_rec, w_out, ffn_norm, w_up, w_down):
    bp, tp, d = x_prompt.shape
    bs, ts, _ = x_sample.shape
    depth = w_in.shape[0]
    n_pool, page, nheads, _, head_dim = cache_k.shape[1:]
    vd = cache_v.shape[-1]
    nrec, dk, dv = state_rec.shape[2:]
    npages = page_table.shape[1]
    past = npages * page
    aw = nheads * vd
    gw = nrec * dk
    assert bp == 1 and 2 * head_dim == vd == LANES and aw == gw and w_in.shape[2] == 7 * gw

    lam_inits = tuple(0.8 - 0.6 * math.exp(-0.3 * l) for l in range(depth))
    lam_all, lb_all = _prep(lambda_q1, lambda_k1, lambda_q2, lambda_k2, rec_lb_logits, lam_inits)

    lane_map = jnp.arange(MXU_TILE) // head_dim
    gmat = (lane_map[:, None] == lane_map[None, :]).astype(BF16)
    rope_p = _rope_tables(jnp.arange(tp), head_dim)
    rope_s = tuple(jnp.tile(a, (bs, 1)) for a in _rope_tables(past + jnp.arange(ts), head_dim))
    srow = jnp.arange(2 * nheads * 8)
    bmask = ((srow[:, None] // (nheads * 8) == (jnp.arange(aw)[None, :] // head_dim) % 2)
             & ((srow[:, None] // 8) % nheads == jnp.arange(aw)[None, :] // vd)).astype(F32)

    hp = x_prompt.reshape(tp, d)
    hs = x_sample.reshape(bs * ts, d)
    outs = [[] for _ in range(6)]
    for l in range(depth):
        w_in_bf = w_in[l].astype(BF16)
        w_out_bf = w_out[l].astype(BF16)
        w_up_bf = w_up[l].astype(BF16)
        w_down_bf = w_down[l].astype(BF16)
        qk_gain = jnp.stack([jnp.tile(q_norm[l], aw // head_dim),
                             jnp.tile(k_norm[l], aw // head_dim)]).reshape(2, 1, aw)
        norm_g = attn_norm[l].reshape(1, d)
        lam = lam_all[l:l + 1]
        lb = lb_all[l:l + 1]
        att_g = att_out_norm[l].reshape(1, vd)
        rec_g = rec_out_norm[l].reshape(1, dv)
        b_att = beta_att[l].reshape(1, aw)
        b_rec = beta_rec[l].reshape(1, gw)
        ffn_g = ffn_norm[l].reshape(1, d)
        out_scale = 1.0 - lam_inits[l]

        proj = _proj(hp, norm_g, w_in_bf, qk_gain, gmat, *rope_p, head_dim=head_dim)
        proj_s = _proj(hs, norm_g, w_in_bf, qk_gain, gmat, *rope_s, head_dim=head_dim)
        proj3 = proj_s.reshape(bs, ts, 7 * gw)
        cache_kt = jnp.transpose(cache_k[l], (0, 2, 3, 4, 1)).reshape(n_pool, aw, page)
        cache_v2 = cache_v[l].reshape(n_pool, page * nheads, vd)
        rec, s_new, att_s = _hgrn_prompt_attn_decode(
            proj, lb, rec_g, b_rec, page_table, lam, att_g, b_att, bmask, proj3, cache_kt, cache_v2,
            nrec=nrec, dk=dk, dv=dv, nheads=nheads, out_scale=out_scale)

        att, k_t, v_p = _attn_prompt(proj, lam, att_g, b_att, nheads=nheads, out_scale=out_scale)
        hp = _finish(hp, att, rec, w_out_bf, ffn_g, w_up_bf, w_down_bf)
        outs[0].append(jnp.transpose(k_t.reshape(nheads, 2, head_dim, tp), (3, 0, 1, 2))
                       .reshape(bp, tp, nheads, 2, head_dim))
        outs[1].append(v_p.reshape(bp, tp, nheads, vd))
        outs[2].append(s_new.reshape(bp, nrec, dk, dv).astype(state_rec.dtype))

        rec_s, s_new = _hgrn_decode(proj3, lb, rec_g, b_rec, state_rec[l], nheads=nrec, dk=dk, dv=dv)
        hs = _finish(hs, att_s.reshape(bs * ts, aw), rec_s.reshape(bs * ts, gw),
                     w_out_bf, ffn_g, w_up_bf, w_down_bf)
        outs[3].append(proj_s[:, aw:2 * aw].reshape(bs, ts, nheads, 2, head_dim))
        outs[4].append(proj_s[:, 2 * aw:3 * aw].reshape(bs, ts, nheads, vd))
        outs[5].append(s_new.astype(state_rec.dtype))

    return (hp.reshape(bp, tp, d), hs.reshape(bs, ts, d),
            jnp.stack(outs[0]), jnp.stack(outs[1]), jnp.stack(outs[2]),
            jnp.stack(outs[3]), jnp.stack(outs[4]), jnp.stack(outs[5]))
```

```python
import functools
import math

import jax
import jax.numpy as jnp
from jax import lax
from jax.experimental import pallas as pl
from jax.experimental.pallas import tpu as pltpu

F32 = jnp.float32
BF16 = jnp.bfloat16

EPS = 1e-6
ROPE_THETA = 500000.0
MASK_VALUE = -1e30
LANES = 128
MXU_TILE = 256
LOG2E = 1.4426950408889634
MAX_FIXED_OFFSET = 60.0
SCORE_BOUND_SLACK = 1.0 + 2.0 ** -6
VMEM_LIMIT_BYTES = 56 * 1024 * 1024
REC_CHUNK = 64

_NT = (((1,), (1,)), ((), ()))
_TA = (((0,), (0,)), ((), ()))


def _params(*semantics):
    return pltpu.CompilerParams(dimension_semantics=semantics, vmem_limit_bytes=VMEM_LIMIT_BYTES)


def _pick(n, pref):
    if n <= pref:
        return n
    t = pref
    while n % t:
        t //= 2
    return t


def _split3(x):
    h1 = x.astype(BF16)
    r1 = x - h1.astype(F32)
    h2 = r1.astype(BF16)
    h3 = (r1 - h2.astype(F32)).astype(BF16)
    return h1, h2, h3


def _prep_body(lq1_ref, lk1_ref, lq2_ref, lk2_ref, logit_ref, lam_ref, lb_ref, *, lam_inits):
    for l, lam_init in enumerate(lam_inits):
        s1 = jnp.sum(lq1_ref[l:l + 1, :] * lk1_ref[l:l + 1, :], axis=1, keepdims=True)
        s2 = jnp.sum(lq2_ref[l:l + 1, :] * lk2_ref[l:l + 1, :], axis=1, keepdims=True)
        lam = jnp.exp(s1) - jnp.exp(s2) + lam_init
        lam_ref[l:l + 1, :] = jnp.broadcast_to(lam, (1, LANES))
    x = logit_ref[...]
    e = jnp.exp(x - jnp.max(x, axis=0, keepdims=True))
    sm = e / jnp.sum(e, axis=0, keepdims=True)
    acc = jnp.zeros((1, x.shape[1]), F32)
    for r in range(x.shape[0]):
        acc = acc + sm[r:r + 1, :]
        lb_ref[r:r + 1, :] = acc


def _prep(lq1, lk1, lq2, lk2, logits, lam_inits):
    depth = lq1.shape[0]
    return pl.pallas_call(
        functools.partial(_prep_body, lam_inits=lam_inits),
        out_shape=(jax.ShapeDtypeStruct((depth, LANES), F32),
                   jax.ShapeDtypeStruct(logits.shape, F32)),
        name="prep",
    )(lq1, lk1, lq2, lk2, logits)


def _proj_body(x_ref, g_ref, w_ref, qkg_ref, gmat_ref, cos_ref, sa_ref, sb_ref, o_ref, xn_ref,
               *, head_dim, rot_half):
    j = pl.program_id(1)

    @pl.when(j == 0)
    def _():
        x = x_ref[...]
        ms = jnp.mean(x * x, axis=-1, keepdims=True)
        xn_ref[...] = (x * lax.rsqrt(ms + EPS) * g_ref[...]).astype(BF16)

    @pl.when(j < 2)
    def _():
        y = jnp.dot(xn_ref[...], w_ref[...], preferred_element_type=F32)
        gm = gmat_ref[...]
        gt = gm.shape[0]
        gain = qkg_ref[0]
        c, sa, sb = cos_ref[...], sa_ref[...], sb_ref[...]
        for t in range(y.shape[1] // gt):
            yt = y[:, t * gt:(t + 1) * gt]
            y2 = yt * yt
            hi = y2.astype(BF16)
            lo = (y2 - hi.astype(F32)).astype(BF16)
            ss = (jnp.dot(hi, gm, preferred_element_type=F32)
                  + jnp.dot(lo, gm, preferred_element_type=F32))
            yn = yt * lax.rsqrt(ss * (1.0 / head_dim) + EPS) * gain[:, t * gt:(t + 1) * gt]
            for u in range(gt // LANES):
                yu = yn[:, u * LANES:(u + 1) * LANES]
                lane0 = t * gt + u * LANES
                o_ref[:, lane0:lane0 + LANES] = (
                    yu * c + pltpu.roll(yu, LANES - rot_half, 1) * sa + pltpu.roll(yu, rot_half, 1) * sb)

    @pl.when(j >= 2)
    def _():
        o_ref[...] = jnp.dot(xn_ref[...], w_ref[...], preferred_element_type=F32)


def _proj(x, norm_g, w_bf, qk_gain, gmat, cos_t, sa_t, sb_t, *, head_dim):
    m, d = x.shape
    gw = qk_gain.shape[2]
    gt = gmat.shape[0]
    ngroups = w_bf.shape[1] // gw
    tm = _pick(m, 1024)
    assert gw % gt == 0 and gt % LANES == 0
    return pl.pallas_call(
        functools.partial(_proj_body, head_dim=head_dim, rot_half=head_dim // 8),
        grid=(m // tm, ngroups),
        in_specs=[
            pl.BlockSpec((tm, d), lambda i, j: (i, 0)),
            pl.BlockSpec((1, d), lambda i, j: (0, 0)),
            pl.BlockSpec((d, gw), lambda i, j: (0, j)),
            pl.BlockSpec((1, 1, gw), lambda i, j: (jnp.minimum(j, 1), 0, 0)),
            pl.BlockSpec((gt, gt), lambda i, j: (0, 0)),
            pl.BlockSpec((tm, LANES), lambda i, j: (i, 0)),
            pl.BlockSpec((tm, LANES), lambda i, j: (i, 0)),
            pl.BlockSpec((tm, LANES), lambda i, j: (i, 0)),
        ],
        out_specs=pl.BlockSpec((tm, gw), lambda i, j: (i, j)),
        out_shape=jax.ShapeDtypeStruct((m, ngroups * gw), F32),
        scratch_shapes=[pltpu.VMEM((tm, d), BF16)],
        compiler_params=_params("parallel", "arbitrary"),
        name="proj",
    )(x, norm_g, w_bf, qk_gain, gmat, cos_t, sa_t, sb_t)


def _attn_body(lam_ref, q_ref, k_ref, v_ref, g_ref, beta_ref, o_ref, kt_ref, vo_ref,
               kb_ref, vb_ref, qs_ref, m_ref, l_ref, acc_ref, kmax_ref, *, tq, wide, out_scale):
    qi = pl.program_id(1)
    half = LANES // 2
    lane = lax.broadcasted_iota(jnp.int32, (1, LANES), 1)

    @pl.when(qi == 0)
    def _():
        kb = k_ref[...].astype(BF16)
        kb_ref[...] = kb
        vb_ref[...] = v_ref[...].astype(BF16)
        vo_ref[...] = v_ref[...]
        for i in range(k_ref.shape[0] // tq):
            kt_ref[:, i * tq:(i + 1) * tq] = k_ref[i * tq:(i + 1) * tq, :].T
        k2 = kb.astype(F32)
        k2 = k2 * k2
        for c, sel in enumerate((lane < half, lane >= half)):
            n2 = jnp.sum(jnp.where(sel, k2, 0.0), axis=1, keepdims=True)
            kmax_ref[c:c + 1, :] = jnp.broadcast_to(
                jnp.sqrt(jnp.max(n2, axis=0, keepdims=True)), (1, LANES))

    q = q_ref[...] * (LOG2E / math.sqrt(half))
    qs = (jnp.where(lane < half, q, 0.0).astype(BF16), jnp.where(lane >= half, q, 0.0).astype(BF16))
    bounds = []
    for c in range(2):
        qs_ref[c * tq:(c + 1) * tq, :] = qs[c]
        qf = qs[c].astype(F32)
        qn = jnp.sqrt(jnp.sum(qf * qf, axis=1, keepdims=True))
        bounds.append(qn * kmax_ref[c:c + 1, :] * SCORE_BOUND_SLACK)
    fast = jnp.max(jnp.maximum(bounds[0], bounds[1])) <= MAX_FIXED_OFFSET
    for c in range(2):
        m_ref[c * tq:(c + 1) * tq, :] = jnp.where(fast, bounds[c], MASK_VALUE)
    l_ref[...] = jnp.zeros(l_ref.shape, F32)
    acc_ref[...] = jnp.zeros(acc_ref.shape, F32)

    def block(off, width, diag_at, fixed_offset):
        kblk = kb_ref[pl.ds(off, width), :]
        vblk = vb_ref[pl.ds(off, width), :]
        for c in range(2):
            rs = slice(c * tq, (c + 1) * tq)
            s = lax.dot_general(qs_ref[rs, :], kblk, _NT, preferred_element_type=F32)
            if diag_at is not None:
                row = lax.broadcasted_iota(jnp.int32, (tq, tq), 0)
                col = lax.broadcasted_iota(jnp.int32, (tq, tq), 1)
                tail = jnp.where(col <= row, s[:, diag_at:], MASK_VALUE)
                s = tail if diag_at == 0 else jnp.concatenate([s[:, :diag_at], tail], axis=1)
            m_prev = m_ref[rs, :]
            if fixed_offset:
                p = jnp.exp2(s - jnp.tile(m_prev, (1, width // LANES)))
                l_ref[rs, :] = l_ref[rs, :] + jnp.sum(p, axis=1, keepdims=True)
                acc_ref[rs, :] = acc_ref[rs, :] + jnp.dot(
                    p.astype(BF16), vblk, preferred_element_type=F32)
            else:
                m_next = jnp.maximum(m_prev, jnp.max(s, axis=1, keepdims=True))
                alpha = jnp.exp2(m_prev - m_next)
                p = jnp.exp2(s - jnp.tile(m_next, (1, width // LANES)))
                l_ref[rs, :] = alpha * l_ref[rs, :] + jnp.sum(p, axis=1, keepdims=True)
                acc_ref[rs, :] = alpha * acc_ref[rs, :] + jnp.dot(
                    p.astype(BF16), vblk, preferred_element_type=F32)
                m_ref[rs, :] = m_next

    per_wide = wide // tq
    nwide = qi // per_wide
    rem = qi - nwide * per_wide

    def all_blocks(fixed_offset):
        def wide_body(j, carry):
            block(pl.multiple_of(j * wide, wide), wide, None, fixed_offset)
            return carry

        lax.fori_loop(0, nwide, wide_body, 0)
        for r in range(per_wide):
            @pl.when(rem == r)
            def _(r=r):
                block(pl.multiple_of(nwide * wide, wide), (r + 1) * tq, r * tq, fixed_offset)

    pl.when(fast)(functools.partial(all_blocks, True))
    pl.when(jnp.logical_not(fast))(functools.partial(all_blocks, False))

    a = acc_ref[...] / l_ref[...]
    o = a[0:tq, :] - lam_ref[...] * a[tq:2 * tq, :]
    ms = jnp.mean(o * o, axis=-1, keepdims=True)
    o_ref[...] = (o * lax.rsqrt(ms + EPS) * g_ref[...] * out_scale * beta_ref[...]).astype(o_ref.dtype)


def _attn_prompt(proj, lam, gain, beta, *, nheads, out_scale):
    t = proj.shape[0]
    tq = _pick(t, 512)
    wide = _pick(t, 4 * tq)
    assert tq % LANES == 0 and wide % tq == 0
    return pl.pallas_call(
        functools.partial(_attn_body, tq=tq, wide=wide, out_scale=out_scale),
        grid=(nheads, t // tq),
        in_specs=[
            pl.BlockSpec((1, LANES), lambda h, i: (0, 0)),
            pl.BlockSpec((tq, LANES), lambda h, i: (i, h)),
            pl.BlockSpec((t, LANES), lambda h, i: (0, nheads + h)),
            pl.BlockSpec((t, LANES), lambda h, i: (0, 2 * nheads + h)),
            pl.BlockSpec((1, LANES), lambda h, i: (0, 0)),
            pl.BlockSpec((1, LANES), lambda h, i: (0, h)),
        ],
        out_specs=(pl.BlockSpec((tq, LANES), lambda h, i: (i, h)),
                   pl.BlockSpec((LANES, t), lambda h, i: (h, 0)),
                   pl.BlockSpec((t, LANES), lambda h, i: (0, h))),
        out_shape=(jax.ShapeDtypeStruct((t, nheads * LANES), BF16),
                   jax.ShapeDtypeStruct((nheads * LANES, t), F32),
                   jax.ShapeDtypeStruct((t, nheads * LANES), F32)),
        scratch_shapes=[
            pltpu.VMEM((t, LANES), BF16),
            pltpu.VMEM((t, LANES), BF16),
            pltpu.VMEM((2 * tq, LANES), BF16),
            pltpu.VMEM((2 * tq, LANES), F32),
            pltpu.VMEM((2 * tq, LANES), F32),
            pltpu.VMEM((2 * tq, LANES), F32),
            pltpu.VMEM((2, LANES), F32),
        ],
        compiler_params=_params("parallel", "arbitrary"),
        name="attn_prompt",
    )(lam, proj, proj, proj, gain, beta)


def _attn_decode_body(pt_ref, lam_ref, g_ref, beta_ref, bmask_ref, q_ref, kn_ref, vn_ref, *rest,
                      npages, page, nheads, ts, out_scale):
    del pt_ref
    kp_refs = rest[:npages]
    vp_refs = rest[npages:2 * npages]
    o_ref, ks_ref, vs_ref, kn_scr = rest[2 * npages:]
    past = npages * page
    tail, w = kn_scr.shape
    nrow = bmask_ref.shape[0]
    half = LANES // 2

    for p in range(npages):
        ks_ref[:, p * page:(p + 1) * page] = kp_refs[p][...].astype(BF16)
        for h in range(nheads):
            vs_ref[p * page:(p + 1) * page, h * LANES:(h + 1) * LANES] = (
                vp_refs[p][pl.ds(h, page, stride=nheads), :].astype(BF16))
    kn_scr[...] = jnp.zeros((tail, w), BF16)
    kn_scr[0:ts, :] = kn_ref[0].astype(BF16)
    vs_ref[past:, :] = jnp.zeros((tail, w), BF16)
    vs_ref[past:past + ts, :] = vn_ref[0].astype(BF16)

    q = q_ref[0] * (LOG2E / math.sqrt(half))
    qt = (jnp.tile(q, (nrow // ts, 1)) * bmask_ref[...]).astype(BF16)
    s_past = jnp.dot(qt, ks_ref[...], preferred_element_type=F32)
    s_new = lax.dot_general(qt, kn_scr[...], _NT, preferred_element_type=F32)
    row = lax.broadcasted_iota(jnp.int32, s_new.shape, 0)
    col = lax.broadcasted_iota(jnp.int32, s_new.shape, 1)
    s_new = jnp.where(col <= lax.rem(row, ts), s_new, MASK_VALUE)
    m = jnp.maximum(jnp.max(s_past, axis=1, keepdims=True), jnp.max(s_new, axis=1, keepdims=True))
    p_past = jnp.exp2(s_past - m)
    p_new = jnp.exp2(s_new - m)
    l = jnp.sum(p_past, axis=1, keepdims=True) + jnp.sum(p_new, axis=1, keepdims=True)
    row1 = lax.broadcasted_iota(jnp.int32, (nrow, 1), 0)
    hr = nrow // 2
    wgt = jnp.where(row1 < hr, 1.0, -lam_ref[:, 0:1]) / l
    pw_past = p_past * wgt
    pw_new = p_new * wgt
    a_past = (pw_past[0:hr, :] + pw_past[hr:, :]).astype(BF16)
    a_new = (pw_new[0:hr, :] + pw_new[hr:, :]).astype(BF16)
    o2 = (jnp.dot(a_past, vs_ref[0:past, :], preferred_element_type=F32)
          + jnp.dot(a_new, vs_ref[past:, :], preferred_element_type=F32))
    for h in range(nheads):
        cs = slice(h * LANES, (h + 1) * LANES)
        o = o2[h * 8:(h + 1) * 8, cs]
        ms = jnp.mean(o * o, axis=-1, keepdims=True)
        o = o * lax.rsqrt(ms + EPS) * g_ref[...] * out_scale * beta_ref[:, cs]
        o_ref[0, :, cs] = o[0:ts, :].astype(o_ref.dtype)


def _hgrn_body(rq_ref, rf_ref, ri_ref, rg_ref, lb_ref, gn_ref, beta_ref, tri_ref, *rest,
               nheads, dk, chunk, batched):
    if batched:
        s0_ref, rec_ref, sout_ref, st_ref, o_scr = rest
        nseq = rq_ref.shape[0]
    else:
        rec_ref, st_ref, o_scr = rest
        nseq = 1

    lb = lb_ref[...]
    tri = tri_ref[...]
    tr = lax.broadcasted_iota(jnp.int32, (chunk, chunk), 0)
    tc = lax.broadcasted_iota(jnp.int32, (chunk, chunk), 1)
    causal = tc <= tr
    for bi in range(nseq):
        sb = bi * nheads
        if batched:
            rq, rf, ri, rg = rq_ref[bi], rf_ref[bi], ri_ref[bi], rg_ref[bi]
            for h in range(nheads):
                st_ref[sb + h] = s0_ref[bi, h].T
        else:
            rq, rf, ri, rg = rq_ref[...], rf_ref[...], ri_ref[...], rg_ref[...]
        rows = rq.shape[0]
        q = rq * jax.nn.sigmoid(rq)
        fg = lb + (1.0 - lb) * jax.nn.sigmoid(rf)
        logf = jnp.log(fg)
        kk = 1.0 - fg
        h1, h2, h3 = _split3(logf)
        b = (jnp.dot(tri, h1, preferred_element_type=F32)
             + jnp.dot(tri, h2, preferred_element_type=F32)
             + jnp.dot(tri, h3, preferred_element_type=F32))
        q_in = (q * jnp.exp(b)).astype(BF16)
        k_in = (kk * jnp.exp(-b)).astype(BF16)
        v_bf = ri.astype(BF16)
        for c in range(rows // chunk):
            rs = slice(c * chunk, (c + 1) * chunk)
            b_last = b[(c + 1) * chunk - 1:(c + 1) * chunk, :]
            k_st = (kk[rs, :] * jnp.exp(b_last - b[rs, :])).astype(BF16)
            decay = jnp.exp(b_last)
            for h in range(nheads):
                cs = slice(h * dk, (h + 1) * dk)
                qc, kc, vc = q_in[rs, cs], k_in[rs, cs], v_bf[rs, cs]
                att = lax.dot_general(qc, kc, _NT, preferred_element_type=F32)
                att = jnp.where(causal, att, 0.0).astype(BF16)
                st = st_ref[sb + h]
                o_scr[bi, rs, cs] = (
                    jnp.dot(att, vc, preferred_element_type=F32)
                    + lax.dot_general(qc, st.astype(BF16), _NT, preferred_element_type=F32))
                st_ref[sb + h] = st * decay[:, cs] + lax.dot_general(
                    vc, k_st[:, cs], _TA, preferred_element_type=F32)
        gate = rg * jax.nn.sigmoid(rg)
        for h in range(nheads):
            cs = slice(h * dk, (h + 1) * dk)
            o = o_scr[bi, :, cs]
            ms = jnp.mean(o * o, axis=-1, keepdims=True)
            o = (o * lax.rsqrt(ms + EPS) * gn_ref[...]) * gate[:, cs] * beta_ref[:, cs]
            if batched:
                rec_ref[bi, :, cs] = o.astype(rec_ref.dtype)
                sout_ref[bi, h] = st_ref[sb + h].T
            else:
                rec_ref[:, cs] = o.astype(rec_ref.dtype)


def _tri(rows, chunk):
    r = jnp.arange(rows)
    return ((r[:, None] >= r[None, :]) & (r[:, None] // chunk == r[None, :] // chunk)).astype(BF16)


def _hgrn_attn_body(pt_ref, *refs, n_rec_in, n_att_in, n_srec_in, rec_steps, att_steps,
                    rec_kw, att_kw, srec_kw):
    n_in = n_rec_in + n_att_in + n_srec_in
    rec_in, att_in = refs[:n_rec_in], refs[n_rec_in:n_rec_in + n_att_in]
    srec_rows, (srec_tri, srec_s0) = refs[n_in - n_srec_in:n_in - 2], refs[n_in - 2:n_in]
    rec_ref, sout_ref, att_ref, srec_ref, ssout_ref = refs[n_in:n_in + 5]
    st_ref, o_scr, ks_ref, vs_ref, kn_scr, sst_ref, so_scr = refs[n_in + 5:]
    nsteps = max(rec_steps, att_steps)
    step = pl.program_id(0)

    @pl.when(step == 0)
    def _():
        st_ref[...] = jnp.zeros(st_ref.shape, F32)

    def run_rec():
        _hgrn_body(*rec_in, rec_ref, st_ref, o_scr, **rec_kw)

    def run_att():
        _attn_decode_body(pt_ref, *att_in, att_ref, ks_ref, vs_ref, kn_scr, **att_kw)
        _hgrn_body(*srec_rows, *rec_in[4:7], srec_tri, srec_s0, srec_ref, ssout_ref, sst_ref, so_scr,
                   **srec_kw)

    if rec_steps < nsteps:
        pl.when(step < rec_steps)(run_rec)
    else:
        run_rec()
    if att_steps < nsteps:
        pl.when(step < att_steps)(run_att)
    else:
        run_att()

    @pl.when(step == rec_steps - 1)
    def _():
        for h in range(sout_ref.shape[0]):
            sout_ref[h] = st_ref[h].T


def _hgrn_attn_fused(proj, lb, rec_gain, rec_beta, page_table, lam, att_gain, att_beta, bmask,
                     proj3, cache_kt, cache_v2, s0, *, nrec, dk, dv, nheads, out_scale):
    t = proj.shape[0]
    gw = nrec * dk
    chunk = math.gcd(t, REC_CHUNK)
    nb, ts, _ = proj3.shape
    schunk = math.gcd(ts, REC_CHUNK)
    npages = page_table.shape[1]
    w, page = cache_kt.shape[1], cache_kt.shape[2]
    nrow = bmask.shape[0]
    tail = LANES
    assert dk == dv == LANES and schunk == ts
    assert 8 % ts == 0 and w == nheads * LANES and cache_v2.shape[1:] == (page * nheads, LANES)
    rows = chunk * _pick(t // chunk, max(1, min(4, (t // chunk) // nb)))
    rec_steps, att_steps = t // rows, nb
    nsteps = max(rec_steps, att_steps)

    def rstep(i):
        return jnp.minimum(i, rec_steps - 1) if rec_steps < nsteps else i

    def bstep(i):
        return jnp.minimum(i, att_steps - 1) if att_steps < nsteps else i

    def k_spec(p):
        return pl.BlockSpec((None, w, page), lambda i, pt: (pt[bstep(i), p], 0, 0))

    def v_spec(p):
        return pl.BlockSpec((None, page * nheads, LANES), lambda i, pt: (pt[bstep(i), p], 0, 0))

    rec_specs = [pl.BlockSpec((rows, gw), functools.partial(lambda i, pt, g: (rstep(i), g), g=3 + g))
                 for g in range(4)] + [
        pl.BlockSpec((1, gw), lambda i, pt: (0, 0)),
        pl.BlockSpec((1, dv), lambda i, pt: (0, 0)),
        pl.BlockSpec((1, gw), lambda i, pt: (0, 0)),
        pl.BlockSpec((rows, rows), lambda i, pt: (0, 0)),
    ]
    att_specs = [
        pl.BlockSpec((1, LANES), lambda i, pt: (0, 0)),
        pl.BlockSpec((1, LANES), lambda i, pt: (0, 0)),
        pl.BlockSpec((1, w), lambda i, pt: (0, 0)),
        pl.BlockSpec((nrow, w), lambda i, pt: (0, 0)),
        pl.BlockSpec((1, ts, w), lambda i, pt: (bstep(i), 0, 0)),
        pl.BlockSpec((1, ts, w), lambda i, pt: (bstep(i), 0, 1)),
        pl.BlockSpec((1, ts, w), lambda i, pt: (bstep(i), 0, 2)),
    ] + [k_spec(p) for p in range(npages)] + [v_spec(p) for p in range(npages)]
    srec_specs = [pl.BlockSpec((1, ts, gw), functools.partial(lambda i, pt, g: (bstep(i), 0, g), g=3 + g))
                  for g in range(4)] + [
        pl.BlockSpec((ts, ts), lambda i, pt: (0, 0)),
        pl.BlockSpec((1, nrec, dk, dv), lambda i, pt: (bstep(i), 0, 0, 0)),
    ]
    grid_spec = pltpu.PrefetchScalarGridSpec(
        num_scalar_prefetch=1,
        grid=(nsteps,),
        in_specs=rec_specs + att_specs + srec_specs,
        out_specs=(pl.BlockSpec((rows, gw), lambda i, pt: (rstep(i), 0)),
                   pl.BlockSpec((nrec, dk, dv), lambda i, pt: (0, 0, 0)),
                   pl.BlockSpec((1, ts, w), lambda i, pt: (bstep(i), 0, 0)),
                   pl.BlockSpec((1, ts, gw), lambda i, pt: (bstep(i), 0, 0)),
                   pl.BlockSpec((1, nrec, dk, dv), lambda i, pt: (bstep(i), 0, 0, 0))),
        scratch_shapes=[
            pltpu.VMEM((nrec, dv, dk), F32),
            pltpu.VMEM((1, rows, gw), F32),
            pltpu.VMEM((w, npages * page), BF16),
            pltpu.VMEM((npages * page + tail, w), BF16),
            pltpu.VMEM((tail, w), BF16),
            pltpu.VMEM((nrec, dv, dk), F32),
            pltpu.VMEM((1, ts, gw), F32),
        ],
    )
    return pl.pallas_call(
        functools.partial(
            _hgrn_attn_body, n_rec_in=len(rec_specs), n_att_in=len(att_specs),
            n_srec_in=len(srec_specs), rec_steps=rec_steps, att_steps=att_steps,
            rec_kw=dict(nheads=nrec, dk=dk, chunk=chunk, batched=False),
            att_kw=dict(npages=npages, page=page, nheads=nheads, ts=ts, out_scale=out_scale),
            srec_kw=dict(nheads=nrec, dk=dk, chunk=schunk, batched=True)),
        grid_spec=grid_spec,
        out_shape=(jax.ShapeDtypeStruct((t, gw), BF16),
                   jax.ShapeDtypeStruct((nrec, dk, dv), F32),
                   jax.ShapeDtypeStruct((nb, ts, w), BF16),
                   jax.ShapeDtypeStruct((nb, ts, gw), BF16),
                   jax.ShapeDtypeStruct((nb, nrec, dk, dv), F32)),
        compiler_params=_params("arbitrary"),
        name="hgrn_attn_fused",
    )(page_table, proj, proj, proj, proj, lb, rec_gain, rec_beta, _tri(rows, chunk),
      lam, att_gain, att_beta, bmask, proj3, proj3, proj3,
      *([cache_kt] * npages), *([cache_v2] * npages),
      proj3, proj3, proj3, proj3, _tri(ts, schunk), s0)


def _finish_body(x_ref, att_ref, rec_ref, wo_ref, g_ref, wu_ref, wd_ref, o_ref, hn_ref):
    j = pl.program_id(1)

    @pl.when(j == 0)
    def _():
        aw = att_ref.shape[1]
        h = (x_ref[...]
             + jnp.dot(att_ref[...], wo_ref[0:aw, :], preferred_element_type=F32)
             + jnp.dot(rec_ref[...], wo_ref[aw:, :], preferred_element_type=F32))
        ms = jnp.mean(h * h, axis=-1, keepdims=True)
        hn_ref[...] = (h * lax.rsqrt(ms + EPS) * g_ref[...]).astype(BF16)
        o_ref[...] = h

    u = jnp.maximum(jnp.dot(hn_ref[...], wu_ref[...], preferred_element_type=F32), 0.0)
    o_ref[...] += jnp.dot((u * u).astype(BF16), wd_ref[...], preferred_element_type=F32)


def _finish(x, att, rec, wo_bf, ffn_g, wu_bf, wd_bf):
    m, d = x.shape
    aw, rw = att.shape[1], rec.shape[1]
    ff = wu_bf.shape[1]
    tm = _pick(m, 512)
    fc = _pick(ff, 1024)
    return pl.pallas_call(
        _finish_body,
        grid=(m // tm, ff // fc),
        in_specs=[
            pl.BlockSpec((tm, d), lambda i, j: (i, 0)),
            pl.BlockSpec((tm, aw), lambda i, j: (i, 0)),
            pl.BlockSpec((tm, rw), lambda i, j: (i, 0)),
            pl.BlockSpec((aw + rw, d), lambda i, j: (0, 0), pipeline_mode=pl.Buffered(1)),
            pl.BlockSpec((1, d), lambda i, j: (0, 0)),
            pl.BlockSpec((d, fc), lambda i, j: (0, j)),
            pl.BlockSpec((fc, d), lambda i, j: (j, 0)),
        ],
        out_specs=pl.BlockSpec((tm, d), lambda i, j: (i, 0)),
        out_shape=jax.ShapeDtypeStruct((m, d), F32),
        scratch_shapes=[pltpu.VMEM((tm, d), BF16)],
        compiler_params=_params("parallel", "arbitrary"),
        name="finish",
    )(x, att, rec, wo_bf, ffn_g, wu_bf, wd_bf)


def _rope_tables(pos, head_dim):
    rot = head_dim // 4
    half = rot // 2
    inv = ROPE_THETA ** (-jnp.arange(half, dtype=F32) * 2.0 / rot)
    ang = pos.astype(F32)[:, None] * inv[None, :]
    cos, sin = jnp.cos(ang), jnp.sin(ang)
    ones = jnp.ones((pos.shape[0], head_dim - rot), F32)
    zeros = jnp.zeros((pos.shape[0], head_dim - rot), F32)
    zh = jnp.zeros_like(sin)
    c = jnp.concatenate([cos, cos, ones], axis=1)
    sa = jnp.concatenate([-sin, zh, zeros], axis=1)
    sb = jnp.concatenate([zh, sin, zeros], axis=1)
    reps = LANES // head_dim
    return tuple(jnp.tile(a, (1, reps)) for a in (c, sa, sb))


def kernel(x_prompt, x_sample, cache_k, cache_v, state_rec, page_table, attn_norm, w_in, q_norm, k_norm,
           lambda_q1, lambda_k1, lambda_q2, lambda_k2, att_out_norm, rec_lb_logits, rec_out_norm,
           beta_att, beta_rec, w_out, ffn_norm, w_up, w_down):
    bp, tp, d = x_prompt.shape
    bs, ts, _ = x_sample.shape
    depth = w_in.shape[0]
    n_pool, page, nheads, _, head_dim = cache_k.shape[1:]
    vd = cache_v.shape[-1]
    nrec, dk, dv = state_rec.shape[2:]
    npages = page_table.shape[1]
    past = npages * page
    aw = nheads * vd
    gw = nrec * dk
    assert bp == 1 and 2 * head_dim == vd == LANES and aw == gw and w_in.shape[2] == 7 * gw

    lam_inits = tuple(0.8 - 0.6 * math.exp(-0.3 * l) for l in range(depth))
    lam_all, lb_all = _prep(lambda_q1, lambda_k1, lambda_q2, lambda_k2, rec_lb_logits, lam_inits)

    lane_map = jnp.arange(MXU_TILE) // head_dim
    gmat = (lane_map[:, None] == lane_map[None, :]).astype(BF16)
    rope_p = _rope_tables(jnp.arange(tp), head_dim)
    rope_s = tuple(jnp.tile(a, (bs, 1)) for a in _rope_tables(past + jnp.arange(ts), head_dim))
    srow = jnp.arange(2 * nheads * 8)
    bmask = ((srow[:, None] // (nheads * 8) == (jnp.arange(aw)[None, :] // head_dim) % 2)
             & ((srow[:, None] // 8) % nheads == jnp.arange(aw)[None, :] // vd)).astype(F32)

    hp = x_prompt.reshape(tp, d)
    hs = x_sample.reshape(bs * ts, d)
    outs = [[] for _ in range(6)]
    for l in range(depth):
        w_in_bf = w_in[l].astype(BF16)
        w_out_bf = w_out[l].astype(BF16)
        w_up_bf = w_up[l].astype(BF16)
        w_down_bf = w_down[l].astype(BF16)
        qk_gain = jnp.stack([jnp.tile(q_norm[l], aw // head_dim),
                             jnp.tile(k_norm[l], aw // head_dim)]).reshape(2, 1, aw)
        norm_g = attn_norm[l].reshape(1, d)
        lam = lam_all[l:l + 1]
        lb = lb_all[l:l + 1]
        att_g = att_out_norm[l].reshape(1, vd)
        rec_g = rec_out_norm[l].reshape(1, dv)
        b_att = beta_att[l].reshape(1, aw)
        b_rec = beta_rec[l].reshape(1, gw)
        ffn_g = ffn_norm[l].reshape(1, d)
        out_scale = 1.0 - lam_inits[l]

        proj = _proj(hp, norm_g, w_in_bf, qk_gain, gmat, *rope_p, head_dim=head_dim)
        proj_s = _proj(hs, norm_g, w_in_bf, qk_gain, gmat, *rope_s, head_dim=head_dim)
        proj3 = proj_s.reshape(bs, ts, 7 * gw)
        cache_kt = jnp.transpose(cache_k[l], (0, 2, 3, 4, 1)).reshape(n_pool, aw, page)
        cache_v2 = cache_v[l].reshape(n_pool, page * nheads, vd)
        rec, s_new, att_s, rec_s, s_new_s = _hgrn_attn_fused(
            proj, lb, rec_g, b_rec, page_table, lam, att_g, b_att, bmask, proj3, cache_kt, cache_v2,
            state_rec[l], nrec=nrec, dk=dk, dv=dv, nheads=nheads, out_scale=out_scale)

        att, k_t, v_p = _attn_prompt(proj, lam, att_g, b_att, nheads=nheads, out_scale=out_scale)
        hp = _finish(hp, att, rec, w_out_bf, ffn_g, w_up_bf, w_down_bf)
        outs[0].append(jnp.transpose(k_t.reshape(nheads, 2, head_dim, tp), (3, 0, 1, 2))
                       .reshape(bp, tp, nheads, 2, head_dim))
        outs[1].append(v_p.reshape(bp, tp, nheads, vd))
        outs[2].append(s_new.reshape(bp, nrec, dk, dv).astype(state_rec.dtype))

        hs = _finish(hs, att_s.reshape(bs * ts, aw), rec_s.reshape(bs * ts, gw),
                     w_out_bf, ffn_g, w_up_bf, w_down_bf)
        outs[3].append(proj_s[:, aw:2 * aw].reshape(bs, ts, nheads, 2, head_dim))
        outs[4].append(proj_s[:, 2 * aw:3 * aw].reshape(bs, ts, nheads, vd))
        outs[5].append(s_new_s.astype(state_rec.dtype))

    return (hp.reshape(bp, tp, d), hs.reshape(bs, ts, d),
            jnp.stack(outs[0]), jnp.stack(outs[1]), jnp.stack(outs[2]),
            jnp.stack(outs[3]), jnp.stack(outs[4]), jnp.stack(outs[5]))
```

```python
import functools
import math

import jax
import jax.numpy as jnp
import numpy as np
from jax import lax
from jax.experimental import pallas as pl
from jax.experimental.pallas import tpu as pltpu

F32 = jnp.float32
BF16 = jnp.bfloat16

EPS = 1e-6
ROPE_THETA = 500000.0
MASK_VALUE = -1e30
LANES = 128
MXU_TILE = 256
LOG2E = 1.4426950408889634
MAX_FIXED_OFFSET = 60.0
SCORE_BOUND_SLACK = 1.0 + 2.0 ** -6
VMEM_LIMIT_BYTES = 56 * 1024 * 1024
REC_CHUNK = 64

_NT = (((1,), (1,)), ((), ()))
_TA = (((0,), (0,)), ((), ()))


def _params(*semantics):
    return pltpu.CompilerParams(dimension_semantics=semantics, vmem_limit_bytes=VMEM_LIMIT_BYTES)


def _pick(n, pref):
    if n <= pref:
        return n
    t = pref
    while n % t:
        t //= 2
    return t


def _split3(x):
    h1 = x.astype(BF16)
    r1 = x - h1.astype(F32)
    h2 = r1.astype(BF16)
    h3 = (r1 - h2.astype(F32)).astype(BF16)
    return h1, h2, h3


def _prep_body(lq1_ref, lk1_ref, lq2_ref, lk2_ref, logit_ref, lam_ref, lb_ref, *, lam_inits):
    for l, lam_init in enumerate(lam_inits):
        s1 = jnp.sum(lq1_ref[l:l + 1, :] * lk1_ref[l:l + 1, :], axis=1, keepdims=True)
        s2 = jnp.sum(lq2_ref[l:l + 1, :] * lk2_ref[l:l + 1, :], axis=1, keepdims=True)
        lam = jnp.exp(s1) - jnp.exp(s2) + lam_init
        lam_ref[l:l + 1, :] = jnp.broadcast_to(lam, (1, LANES))
    x = logit_ref[...]
    e = jnp.exp(x - jnp.max(x, axis=0, keepdims=True))
    sm = e / jnp.sum(e, axis=0, keepdims=True)
    acc = jnp.zeros((1, x.shape[1]), F32)
    for r in range(x.shape[0]):
        acc = acc + sm[r:r + 1, :]
        lb_ref[r:r + 1, :] = acc


def _prep(lq1, lk1, lq2, lk2, logits, lam_inits):
    depth = lq1.shape[0]
    return pl.pallas_call(
        functools.partial(_prep_body, lam_inits=lam_inits),
        out_shape=(jax.ShapeDtypeStruct((depth, LANES), F32),
                   jax.ShapeDtypeStruct(logits.shape, F32)),
        name="prep",
    )(lq1, lk1, lq2, lk2, logits)


def _rows2d(x):
    return x.reshape(-1, x.shape[-1]) if x.ndim == 3 else x


def _store_rows(ref, cols, val):
    if len(ref.shape) == 3:
        ref[:, :, cols] = val.reshape(ref.shape[0], ref.shape[1], val.shape[-1])
    else:
        ref[:, cols] = val


def _proj_body(x_ref, g_ref, w_ref, qkg_ref, gmat_ref, cos_ref, sa_ref, sb_ref, o_ref, xn_ref,
               *, head_dim, rot_half):
    j = pl.program_id(1)

    @pl.when(j == 0)
    def _():
        x = _rows2d(x_ref[...])
        ms = jnp.mean(x * x, axis=-1, keepdims=True)
        xn_ref[...] = (x * lax.rsqrt(ms + EPS) * g_ref[...]).astype(BF16)

    @pl.when(j < 2)
    def _():
        y = jnp.dot(xn_ref[...], w_ref[...], preferred_element_type=F32)
        gm = gmat_ref[...]
        gt = gm.shape[0]
        gain = qkg_ref[0]
        c, sa, sb = cos_ref[...], sa_ref[...], sb_ref[...]
        for t in range(y.shape[1] // gt):
            yt = y[:, t * gt:(t + 1) * gt]
            y2 = yt * yt
            hi = y2.astype(BF16)
            lo = (y2 - hi.astype(F32)).astype(BF16)
            ss = (jnp.dot(hi, gm, preferred_element_type=F32)
                  + jnp.dot(lo, gm, preferred_element_type=F32))
            yn = yt * lax.rsqrt(ss * (1.0 / head_dim) + EPS) * gain[:, t * gt:(t + 1) * gt]
            for u in range(gt // LANES):
                yu = yn[:, u * LANES:(u + 1) * LANES]
                lane0 = t * gt + u * LANES
                _store_rows(o_ref, slice(lane0, lane0 + LANES),
                            yu * c + pltpu.roll(yu, LANES - rot_half, 1) * sa
                            + pltpu.roll(yu, rot_half, 1) * sb)

    @pl.when(j >= 2)
    def _():
        _store_rows(o_ref, slice(None), jnp.dot(xn_ref[...], w_ref[...], preferred_element_type=F32))


def _proj(x, norm_g, w_bf, qk_gain, gmat, cos_t, sa_t, sb_t, *, head_dim):
    d = x.shape[-1]
    m = math.prod(x.shape[:-1])
    gw = qk_gain.shape[2]
    gt = gmat.shape[0]
    ngroups = w_bf.shape[1] // gw
    tm = _pick(m, 1024)
    assert gw % gt == 0 and gt % LANES == 0
    if x.ndim == 3:
        ts = x.shape[1]
        assert tm % ts == 0
        x_spec = pl.BlockSpec((tm // ts, ts, d), lambda i, j: (i, 0, 0))
        o_spec = pl.BlockSpec((tm // ts, ts, gw), lambda i, j: (i, 0, j))
        o_shape = (x.shape[0], ts, ngroups * gw)
    else:
        x_spec = pl.BlockSpec((tm, d), lambda i, j: (i, 0))
        o_spec = pl.BlockSpec((tm, gw), lambda i, j: (i, j))
        o_shape = (m, ngroups * gw)
    return pl.pallas_call(
        functools.partial(_proj_body, head_dim=head_dim, rot_half=head_dim // 8),
        grid=(m // tm, ngroups),
        in_specs=[
            x_spec,
            pl.BlockSpec((1, d), lambda i, j: (0, 0)),
            pl.BlockSpec((d, gw), lambda i, j: (0, j)),
            pl.BlockSpec((1, 1, gw), lambda i, j: (jnp.minimum(j, 1), 0, 0)),
            pl.BlockSpec((gt, gt), lambda i, j: (0, 0)),
            pl.BlockSpec((tm, LANES), lambda i, j: (i, 0)),
            pl.BlockSpec((tm, LANES), lambda i, j: (i, 0)),
            pl.BlockSpec((tm, LANES), lambda i, j: (i, 0)),
        ],
        out_specs=o_spec,
        out_shape=jax.ShapeDtypeStruct(o_shape, F32),
        scratch_shapes=[pltpu.VMEM((tm, d), BF16)],
        compiler_params=_params("parallel", "arbitrary"),
        name="proj",
    )(x, norm_g, w_bf, qk_gain, gmat, cos_t, sa_t, sb_t)


def _attn_body(lam_ref, q_ref, k_ref, v_ref, g_ref, beta_ref, o_ref, kt_ref, vo_ref,
               kb_ref, vb_ref, qs_ref, m_ref, l_ref, acc_ref, kmax_ref, *, tq, wide, out_scale):
    qi = pl.program_id(1)
    half = LANES // 2
    lane = lax.broadcasted_iota(jnp.int32, (1, LANES), 1)

    @pl.when(qi == 0)
    def _():
        kb = k_ref[...].astype(BF16)
        kb_ref[...] = kb
        vb_ref[...] = v_ref[...].astype(BF16)
        vo_ref[...] = v_ref[...]
        for i in range(k_ref.shape[0] // tq):
            kt_ref[:, i * tq:(i + 1) * tq] = k_ref[i * tq:(i + 1) * tq, :].T
        k2 = kb.astype(F32)
        k2 = k2 * k2
        for c, sel in enumerate((lane < half, lane >= half)):
            n2 = jnp.sum(jnp.where(sel, k2, 0.0), axis=1, keepdims=True)
            kmax_ref[c:c + 1, :] = jnp.broadcast_to(
                jnp.sqrt(jnp.max(n2, axis=0, keepdims=True)), (1, LANES))

    q = q_ref[...] * (LOG2E / math.sqrt(half))
    qs = (jnp.where(lane < half, q, 0.0).astype(BF16), jnp.where(lane >= half, q, 0.0).astype(BF16))
    bounds = []
    for c in range(2):
        qs_ref[c * tq:(c + 1) * tq, :] = qs[c]
        qf = qs[c].astype(F32)
        qn = jnp.sqrt(jnp.sum(qf * qf, axis=1, keepdims=True))
        bounds.append(qn * kmax_ref[c:c + 1, :] * SCORE_BOUND_SLACK)
    fast = jnp.max(jnp.maximum(bounds[0], bounds[1])) <= MAX_FIXED_OFFSET
    for c in range(2):
        m_ref[c * tq:(c + 1) * tq, :] = jnp.where(fast, bounds[c], MASK_VALUE)
    l_ref[...] = jnp.zeros(l_ref.shape, F32)
    acc_ref[...] = jnp.zeros(acc_ref.shape, F32)

    def block(off, width, diag_at, fixed_offset):
        kblk = kb_ref[pl.ds(off, width), :]
        vblk = vb_ref[pl.ds(off, width), :]
        for c in range(2):
            rs = slice(c * tq, (c + 1) * tq)
            s = lax.dot_general(qs_ref[rs, :], kblk, _NT, preferred_element_type=F32)
            if diag_at is not None:
                row = lax.broadcasted_iota(jnp.int32, (tq, tq), 0)
                col = lax.broadcasted_iota(jnp.int32, (tq, tq), 1)
                tail = jnp.where(col <= row, s[:, diag_at:], MASK_VALUE)
                s = tail if diag_at == 0 else jnp.concatenate([s[:, :diag_at], tail], axis=1)
            m_prev = m_ref[rs, :]
            if fixed_offset:
                p = jnp.exp2(s - jnp.tile(m_prev, (1, width // LANES)))
                l_ref[rs, :] = l_ref[rs, :] + jnp.sum(p, axis=1, keepdims=True)
                acc_ref[rs, :] = acc_ref[rs, :] + jnp.dot(
                    p.astype(BF16), vblk, preferred_element_type=F32)
            else:
                m_next = jnp.maximum(m_prev, jnp.max(s, axis=1, keepdims=True))
                alpha = jnp.exp2(m_prev - m_next)
                p = jnp.exp2(s - jnp.tile(m_next, (1, width // LANES)))
                l_ref[rs, :] = alpha * l_ref[rs, :] + jnp.sum(p, axis=1, keepdims=True)
                acc_ref[rs, :] = alpha * acc_ref[rs, :] + jnp.dot(
                    p.astype(BF16), vblk, preferred_element_type=F32)
                m_ref[rs, :] = m_next

    per_wide = wide // tq
    nwide = qi // per_wide
    rem = qi - nwide * per_wide

    def all_blocks(fixed_offset):
        def wide_body(j, carry):
            block(pl.multiple_of(j * wide, wide), wide, None, fixed_offset)
            return carry

        lax.fori_loop(0, nwide, wide_body, 0)
        for r in range(per_wide):
            @pl.when(rem == r)
            def _(r=r):
                block(pl.multiple_of(nwide * wide, wide), (r + 1) * tq, r * tq, fixed_offset)

    pl.when(fast)(functools.partial(all_blocks, True))
    pl.when(jnp.logical_not(fast))(functools.partial(all_blocks, False))

    a = acc_ref[...] / l_ref[...]
    o = a[0:tq, :] - lam_ref[...] * a[tq:2 * tq, :]
    ms = jnp.mean(o * o, axis=-1, keepdims=True)
    o_ref[...] = (o * lax.rsqrt(ms + EPS) * g_ref[...] * out_scale * beta_ref[...]).astype(o_ref.dtype)


def _attn_prompt(proj, lam, gain, beta, *, nheads, out_scale):
    t = proj.shape[0]
    tq = _pick(t, 512)
    wide = _pick(t, 4 * tq)
    assert tq % LANES == 0 and wide % tq == 0
    return pl.pallas_call(
        functools.partial(_attn_body, tq=tq, wide=wide, out_scale=out_scale),
        grid=(nheads, t // tq),
        in_specs=[
            pl.BlockSpec((1, LANES), lambda h, i: (0, 0)),
            pl.BlockSpec((tq, LANES), lambda h, i: (i, h)),
            pl.BlockSpec((t, LANES), lambda h, i: (0, nheads + h)),
            pl.BlockSpec((t, LANES), lambda h, i: (0, 2 * nheads + h)),
            pl.BlockSpec((1, LANES), lambda h, i: (0, 0)),
            pl.BlockSpec((1, LANES), lambda h, i: (0, h)),
        ],
        out_specs=(pl.BlockSpec((tq, LANES), lambda h, i: (i, h)),
                   pl.BlockSpec((LANES, t), lambda h, i: (h, 0)),
                   pl.BlockSpec((t, LANES), lambda h, i: (0, h))),
        out_shape=(jax.ShapeDtypeStruct((t, nheads * LANES), BF16),
                   jax.ShapeDtypeStruct((nheads * LANES, t), F32),
                   jax.ShapeDtypeStruct((t, nheads * LANES), F32)),
        scratch_shapes=[
            pltpu.VMEM((t, LANES), BF16),
            pltpu.VMEM((t, LANES), BF16),
            pltpu.VMEM((2 * tq, LANES), BF16),
            pltpu.VMEM((2 * tq, LANES), F32),
            pltpu.VMEM((2 * tq, LANES), F32),
            pltpu.VMEM((2 * tq, LANES), F32),
            pltpu.VMEM((2, LANES), F32),
        ],
        compiler_params=_params("parallel", "arbitrary"),
        name="attn_prompt",
    )(lam, proj, proj, proj, gain, beta)


def _attn_decode_body(pt_ref, lam_ref, g_ref, beta_ref, bmask_ref, q_ref, kn_ref, vn_ref, *rest,
                      npages, page, nheads, ts, out_scale):
    del pt_ref
    kp_refs = rest[:npages]
    vp_refs = rest[npages:2 * npages]
    o_ref, ks_ref, vs_ref, kn_scr = rest[2 * npages:]
    past = npages * page
    tail, w = kn_scr.shape
    nrow = bmask_ref.shape[0]
    half = LANES // 2

    for p in range(npages):
        ks_ref[:, p * page:(p + 1) * page] = kp_refs[p][...].astype(BF16)
        for h in range(nheads):
            vs_ref[p * page:(p + 1) * page, h * LANES:(h + 1) * LANES] = (
                vp_refs[p][pl.ds(h, page, stride=nheads), :].astype(BF16))
    kn_scr[...] = jnp.zeros((tail, w), BF16)
    kn_scr[0:ts, :] = kn_ref[0].astype(BF16)
    vs_ref[past:, :] = jnp.zeros((tail, w), BF16)
    vs_ref[past:past + ts, :] = vn_ref[0].astype(BF16)

    q = q_ref[0] * (LOG2E / math.sqrt(half))
    qt = (jnp.tile(q, (nrow // ts, 1)) * bmask_ref[...]).astype(BF16)
    s_past = jnp.dot(qt, ks_ref[...], preferred_element_type=F32)
    s_new = lax.dot_general(qt, kn_scr[...], _NT, preferred_element_type=F32)
    row = lax.broadcasted_iota(jnp.int32, s_new.shape, 0)
    col = lax.broadcasted_iota(jnp.int32, s_new.shape, 1)
    s_new = jnp.where(col <= lax.rem(row, ts), s_new, MASK_VALUE)
    m = jnp.maximum(jnp.max(s_past, axis=1, keepdims=True), jnp.max(s_new, axis=1, keepdims=True))
    p_past = jnp.exp2(s_past - m)
    p_new = jnp.exp2(s_new - m)
    l = jnp.sum(p_past, axis=1, keepdims=True) + jnp.sum(p_new, axis=1, keepdims=True)
    row1 = lax.broadcasted_iota(jnp.int32, (nrow, 1), 0)
    hr = nrow // 2
    wgt = jnp.where(row1 < hr, 1.0, -lam_ref[:, 0:1]) / l
    pw_past = p_past * wgt
    pw_new = p_new * wgt
    a_past = (pw_past[0:hr, :] + pw_past[hr:, :]).astype(BF16)
    a_new = (pw_new[0:hr, :] + pw_new[hr:, :]).astype(BF16)
    o2 = (jnp.dot(a_past, vs_ref[0:past, :], preferred_element_type=F32)
          + jnp.dot(a_new, vs_ref[past:, :], preferred_element_type=F32))
    for h in range(nheads):
        cs = slice(h * LANES, (h + 1) * LANES)
        o = o2[h * 8:(h + 1) * 8, cs]
        ms = jnp.mean(o * o, axis=-1, keepdims=True)
        o = o * lax.rsqrt(ms + EPS) * g_ref[...] * out_scale * beta_ref[:, cs]
        o_ref[0, :, cs] = o[0:ts, :].astype(o_ref.dtype)


def _hgrn_body(rq_ref, rf_ref, ri_ref, rg_ref, lb_ref, gn_ref, beta_ref, tri_ref, *rest,
               nheads, dk, chunk, batched):
    if batched:
        s0_ref, rec_ref, sout_ref, st_ref, o_scr = rest
        nseq = rq_ref.shape[0]
    else:
        rec_ref, st_ref, o_scr = rest
        nseq = 1

    lb = lb_ref[...]
    tri = tri_ref[...]
    tr = lax.broadcasted_iota(jnp.int32, (chunk, chunk), 0)
    tc = lax.broadcasted_iota(jnp.int32, (chunk, chunk), 1)
    causal = tc <= tr
    for bi in range(nseq):
        sb = bi * nheads
        if batched:
            rq, rf, ri, rg = rq_ref[bi], rf_ref[bi], ri_ref[bi], rg_ref[bi]
            for h in range(nheads):
                st_ref[sb + h] = s0_ref[bi, h].T
        else:
            rq, rf, ri, rg = rq_ref[...], rf_ref[...], ri_ref[...], rg_ref[...]
        rows = rq.shape[0]
        q = rq * jax.nn.sigmoid(rq)
        fg = lb + (1.0 - lb) * jax.nn.sigmoid(rf)
        logf = jnp.log(fg)
        kk = 1.0 - fg
        h1, h2, h3 = _split3(logf)
        b = (jnp.dot(tri, h1, preferred_element_type=F32)
             + jnp.dot(tri, h2, preferred_element_type=F32)
             + jnp.dot(tri, h3, preferred_element_type=F32))
        q_in = (q * jnp.exp(b)).astype(BF16)
        k_in = (kk * jnp.exp(-b)).astype(BF16)
        v_bf = ri.astype(BF16)
        for c in range(rows // chunk):
            rs = slice(c * chunk, (c + 1) * chunk)
            b_last = b[(c + 1) * chunk - 1:(c + 1) * chunk, :]
            k_st = (kk[rs, :] * jnp.exp(b_last - b[rs, :])).astype(BF16)
            decay = jnp.exp(b_last)
            for h in range(nheads):
                cs = slice(h * dk, (h + 1) * dk)
                qc, kc, vc = q_in[rs, cs], k_in[rs, cs], v_bf[rs, cs]
                att = lax.dot_general(qc, kc, _NT, preferred_element_type=F32)
                att = jnp.where(causal, att, 0.0).astype(BF16)
                st = st_ref[sb + h]
                o_scr[bi, rs, cs] = (
                    jnp.dot(att, vc, preferred_element_type=F32)
                    + lax.dot_general(qc, st.astype(BF16), _NT, preferred_element_type=F32))
                st_ref[sb + h] = st * decay[:, cs] + lax.dot_general(
                    vc, k_st[:, cs], _TA, preferred_element_type=F32)
        gate = rg * jax.nn.sigmoid(rg)
        for h in range(nheads):
            cs = slice(h * dk, (h + 1) * dk)
            o = o_scr[bi, :, cs]
            ms = jnp.mean(o * o, axis=-1, keepdims=True)
            o = (o * lax.rsqrt(ms + EPS) * gn_ref[...]) * gate[:, cs] * beta_ref[:, cs]
            if batched:
                rec_ref[bi, :, cs] = o.astype(rec_ref.dtype)
                sout_ref[bi, h] = st_ref[sb + h].T
            else:
                rec_ref[:, cs] = o.astype(rec_ref.dtype)


def _tri(rows, chunk):
    r = np.arange(rows)
    return jnp.asarray((r[:, None] >= r[None, :]) & (r[:, None] // chunk == r[None, :] // chunk), BF16)


def _hgrn_attn_body(pt_ref, *refs, n_rec_in, n_att_in, n_srec_in, rec_steps, att_steps,
                    rec_kw, att_kw, srec_kw):
    n_in = n_rec_in + n_att_in + n_srec_in
    rec_in, att_in = refs[:n_rec_in], refs[n_rec_in:n_rec_in + n_att_in]
    srec_rows, (srec_tri, srec_s0) = refs[n_in - n_srec_in:n_in - 2], refs[n_in - 2:n_in]
    rec_ref, sout_ref, att_ref, srec_ref, ssout_ref = refs[n_in:n_in + 5]
    st_ref, o_scr, ks_ref, vs_ref, kn_scr, sst_ref, so_scr = refs[n_in + 5:]
    nsteps = max(rec_steps, att_steps)
    step = pl.program_id(0)

    @pl.when(step == 0)
    def _():
        st_ref[...] = jnp.zeros(st_ref.shape, F32)

    def run_rec():
        _hgrn_body(*rec_in, rec_ref, st_ref, o_scr, **rec_kw)

    def run_att():
        _attn_decode_body(pt_ref, *att_in, att_ref, ks_ref, vs_ref, kn_scr, **att_kw)
        _hgrn_body(*srec_rows, *rec_in[4:7], srec_tri, srec_s0, srec_ref, ssout_ref, sst_ref, so_scr,
                   **srec_kw)

    if rec_steps < nsteps:
        pl.when(step < rec_steps)(run_rec)
    else:
        run_rec()
    if att_steps < nsteps:
        pl.when(step < att_steps)(run_att)
    else:
        run_att()

    @pl.when(step == rec_steps - 1)
    def _():
        for h in range(sout_ref.shape[0]):
            sout_ref[h] = st_ref[h].T


def _hgrn_attn_fused(proj, lb, rec_gain, rec_beta, page_table, lam, att_gain, att_beta, bmask,
                     proj3, cache_kt, cache_v2, s0, *, nrec, dk, dv, nheads, out_scale):
    t = proj.shape[0]
    gw = nrec * dk
    chunk = math.gcd(t, REC_CHUNK)
    nb, ts, _ = proj3.shape
    schunk = math.gcd(ts, REC_CHUNK)
    npages = page_table.shape[1]
    w, page = cache_kt.shape[1], cache_kt.shape[2]
    nrow = bmask.shape[0]
    tail = LANES
    assert dk == dv == LANES and schunk == ts
    assert 8 % ts == 0 and w == nheads * LANES and cache_v2.shape[1:] == (page * nheads, LANES)
    rows = chunk * _pick(t // chunk, max(1, min(4, (t // chunk) // nb)))
    rec_steps, att_steps = t // rows, nb
    nsteps = max(rec_steps, att_steps)

    def rstep(i):
        return jnp.minimum(i, rec_steps - 1) if rec_steps < nsteps else i

    def bstep(i):
        return jnp.minimum(i, att_steps - 1) if att_steps < nsteps else i

    def k_spec(p):
        return pl.BlockSpec((None, w, page), lambda i, pt: (pt[bstep(i), p], 0, 0))

    def v_spec(p):
        return pl.BlockSpec((None, page * nheads, LANES), lambda i, pt: (pt[bstep(i), p], 0, 0))

    rec_specs = [pl.BlockSpec((rows, gw), functools.partial(lambda i, pt, g: (rstep(i), g), g=3 + g))
                 for g in range(4)] + [
        pl.BlockSpec((1, gw), lambda i, pt: (0, 0)),
        pl.BlockSpec((1, dv), lambda i, pt: (0, 0)),
        pl.BlockSpec((1, gw), lambda i, pt: (0, 0)),
        pl.BlockSpec((rows, rows), lambda i, pt: (0, 0)),
    ]
    att_specs = [
        pl.BlockSpec((1, LANES), lambda i, pt: (0, 0)),
        pl.BlockSpec((1, LANES), lambda i, pt: (0, 0)),
        pl.BlockSpec((1, w), lambda i, pt: (0, 0)),
        pl.BlockSpec((nrow, w), lambda i, pt: (0, 0)),
        pl.BlockSpec((1, ts, w), lambda i, pt: (bstep(i), 0, 0)),
        pl.BlockSpec((1, ts, w), lambda i, pt: (bstep(i), 0, 1)),
        pl.BlockSpec((1, ts, w), lambda i, pt: (bstep(i), 0, 2)),
    ] + [k_spec(p) for p in range(npages)] + [v_spec(p) for p in range(npages)]
    srec_specs = [pl.BlockSpec((1, ts, gw), functools.partial(lambda i, pt, g: (bstep(i), 0, g), g=3 + g))
                  for g in range(4)] + [
        pl.BlockSpec((ts, ts), lambda i, pt: (0, 0)),
        pl.BlockSpec((1, nrec, dk, dv), lambda i, pt: (bstep(i), 0, 0, 0)),
    ]
    grid_spec = pltpu.PrefetchScalarGridSpec(
        num_scalar_prefetch=1,
        grid=(nsteps,),
        in_specs=rec_specs + att_specs + srec_specs,
        out_specs=(pl.BlockSpec((rows, gw), lambda i, pt: (rstep(i), 0)),
                   pl.BlockSpec((nrec, dk, dv), lambda i, pt: (0, 0, 0)),
                   pl.BlockSpec((1, ts, w), lambda i, pt: (bstep(i), 0, 0)),
                   pl.BlockSpec((1, ts, gw), lambda i, pt: (bstep(i), 0, 0)),
                   pl.BlockSpec((1, nrec, dk, dv), lambda i, pt: (bstep(i), 0, 0, 0))),
        scratch_shapes=[
            pltpu.VMEM((nrec, dv, dk), F32),
            pltpu.VMEM((1, rows, gw), F32),
            pltpu.VMEM((w, npages * page), BF16),
            pltpu.VMEM((npages * page + tail, w), BF16),
            pltpu.VMEM((tail, w), BF16),
            pltpu.VMEM((nrec, dv, dk), F32),
            pltpu.VMEM((1, ts, gw), F32),
        ],
    )
    return pl.pallas_call(
        functools.partial(
            _hgrn_attn_body, n_rec_in=len(rec_specs), n_att_in=len(att_specs),
            n_srec_in=len(srec_specs), rec_steps=rec_steps, att_steps=att_steps,
            rec_kw=dict(nheads=nrec, dk=dk, chunk=chunk, batched=False),
            att_kw=dict(npages=npages, page=page, nheads=nheads, ts=ts, out_scale=out_scale),
            srec_kw=dict(nheads=nrec, dk=dk, chunk=schunk, batched=True)),
        grid_spec=grid_spec,
        out_shape=(jax.ShapeDtypeStruct((t, gw), BF16),
                   jax.ShapeDtypeStruct((nrec, dk, dv), F32),
                   jax.ShapeDtypeStruct((nb, ts, w), BF16),
                   jax.ShapeDtypeStruct((nb, ts, gw), BF16),
                   jax.ShapeDtypeStruct((nb, nrec, dk, dv), F32)),
        compiler_params=_params("arbitrary"),
        name="hgrn_attn_fused",
    )(page_table, proj, proj, proj, proj, lb, rec_gain, rec_beta, _tri(rows, chunk),
      lam, att_gain, att_beta, bmask, proj3, proj3, proj3,
      *([cache_kt] * npages), *([cache_v2] * npages),
      proj3, proj3, proj3, proj3, _tri(ts, schunk), s0)


def _finish_body(x_ref, att_ref, rec_ref, wo_ref, g_ref, wu_ref, wd_ref, o_ref, hn_ref):
    j = pl.program_id(1)

    @pl.when(j == 0)
    def _():
        aw = att_ref.shape[-1]
        h = (_rows2d(x_ref[...])
             + jnp.dot(_rows2d(att_ref[...]), wo_ref[0:aw, :], preferred_element_type=F32)
             + jnp.dot(_rows2d(rec_ref[...]), wo_ref[aw:, :], preferred_element_type=F32))
        ms = jnp.mean(h * h, axis=-1, keepdims=True)
        hn_ref[...] = (h * lax.rsqrt(ms + EPS) * g_ref[...]).astype(BF16)
        o_ref[...] = h.reshape(o_ref.shape)

    u = jnp.maximum(jnp.dot(hn_ref[...], wu_ref[...], preferred_element_type=F32), 0.0)
    o_ref[...] += jnp.dot((u * u).astype(BF16), wd_ref[...],
                          preferred_element_type=F32).reshape(o_ref.shape)


def _finish(x, att, rec, wo_bf, ffn_g, wu_bf, wd_bf):
    d, aw, rw = x.shape[-1], att.shape[-1], rec.shape[-1]
    m = math.prod(x.shape[:-1])
    ff = wu_bf.shape[1]
    tm = _pick(m, 512)
    fc = _pick(ff, 1024)
    if x.ndim == 3:
        ts = x.shape[1]
        assert tm % ts == 0 and att.shape[:2] == rec.shape[:2] == x.shape[:2]
        row_spec = lambda n: pl.BlockSpec((tm // ts, ts, n), lambda i, j: (i, 0, 0))
    else:
        row_spec = lambda n: pl.BlockSpec((tm, n), lambda i, j: (i, 0))
    return pl.pallas_call(
        _finish_body,
        grid=(m // tm, ff // fc),
        in_specs=[
            row_spec(d),
            row_spec(aw),
            row_spec(rw),
            pl.BlockSpec((aw + rw, d), lambda i, j: (0, 0), pipeline_mode=pl.Buffered(1)),
            pl.BlockSpec((1, d), lambda i, j: (0, 0)),
            pl.BlockSpec((d, fc), lambda i, j: (0, j)),
            pl.BlockSpec((fc, d), lambda i, j: (j, 0)),
        ],
        out_specs=row_spec(d),
        out_shape=jax.ShapeDtypeStruct(x.shape, F32),
        scratch_shapes=[pltpu.VMEM((tm, d), BF16)],
        compiler_params=_params("parallel", "arbitrary"),
        name="finish",
    )(x, att, rec, wo_bf, ffn_g, wu_bf, wd_bf)


def _rope_tables(pos, head_dim):
    rot = head_dim // 4
    half = rot // 2
    inv = ROPE_THETA ** (-np.arange(half, dtype=np.float64) * 2.0 / rot)
    ang = np.asarray(pos, np.float64)[:, None] * inv[None, :]
    cos, sin = np.cos(ang), np.sin(ang)
    ones = np.ones((ang.shape[0], head_dim - rot))
    zeros = np.zeros((ang.shape[0], head_dim - rot))
    zh = np.zeros_like(sin)
    c = np.concatenate([cos, cos, ones], axis=1)
    sa = np.concatenate([-sin, zh, zeros], axis=1)
    sb = np.concatenate([zh, sin, zeros], axis=1)
    reps = LANES // head_dim
    return tuple(np.tile(a, (1, reps)).astype(np.float32) for a in (c, sa, sb))


def kernel(x_prompt, x_sample, cache_k, cache_v, state_rec, page_table, attn_norm, w_in, q_norm, k_norm,
           lambda_q1, lambda_k1, lambda_q2, lambda_k2, att_out_norm, rec_lb_logits, rec_out_norm,
           beta_att, beta_rec, w_out, ffn_norm, w_up, w_down):
    bp, tp, d = x_prompt.shape
    bs, ts, _ = x_sample.shape
    depth = w_in.shape[0]
    n_pool, page, nheads, _, head_dim = cache_k.shape[1:]
    vd = cache_v.shape[-1]
    nrec, dk, dv = state_rec.shape[2:]
    npages = page_table.shape[1]
    past = npages * page
    aw = nheads * vd
    gw = nrec * dk
    assert bp == 1 and 2 * head_dim == vd == LANES and aw == gw and w_in.shape[2] == 7 * gw

    lam_inits = tuple(0.8 - 0.6 * math.exp(-0.3 * l) for l in range(depth))
    lam_all, lb_all = _prep(lambda_q1, lambda_k1, lambda_q2, lambda_k2, rec_lb_logits, lam_inits)

    lane_map = np.arange(MXU_TILE) // head_dim
    gmat = jnp.asarray(lane_map[:, None] == lane_map[None, :], BF16)
    rope_p = _rope_tables(np.arange(tp), head_dim)
    rope_s = tuple(np.tile(a, (bs, 1)) for a in _rope_tables(past + np.arange(ts), head_dim))
    srow, kcol = np.arange(2 * nheads * 8)[:, None], np.arange(aw)[None, :]
    bmask = ((srow // (nheads * 8) == (kcol // head_dim) % 2)
             & ((srow // 8) % nheads == kcol // vd)).astype(np.float32)

    hp = x_prompt.reshape(tp, d)
    hs = x_sample
    outs = [[] for _ in range(6)]
    for l in range(depth):
        w_in_bf = w_in[l].astype(BF16)
        w_out_bf = w_out[l].astype(BF16)
        w_up_bf = w_up[l].astype(BF16)
        w_down_bf = w_down[l].astype(BF16)
        qk_gain = jnp.stack([jnp.tile(q_norm[l], aw // head_dim),
                             jnp.tile(k_norm[l], aw // head_dim)]).reshape(2, 1, aw)
        norm_g = attn_norm[l].reshape(1, d)
        lam = lam_all[l:l + 1]
        lb = lb_all[l:l + 1]
        att_g = att_out_norm[l].reshape(1, vd)
        rec_g = rec_out_norm[l].reshape(1, dv)
        b_att = beta_att[l].reshape(1, aw)
        b_rec = beta_rec[l].reshape(1, gw)
        ffn_g = ffn_norm[l].reshape(1, d)
        out_scale = 1.0 - lam_inits[l]

        proj = _proj(hp, norm_g, w_in_bf, qk_gain, gmat, *rope_p, head_dim=head_dim)
        proj3 = _proj(hs, norm_g, w_in_bf, qk_gain, gmat, *rope_s, head_dim=head_dim)
        cache_kt = jnp.transpose(cache_k[l], (0, 2, 3, 4, 1)).reshape(n_pool, aw, page)
        cache_v2 = cache_v[l].reshape(n_pool, page * nheads, vd)
        rec, s_new, att_s, rec_s, s_new_s = _hgrn_attn_fused(
            proj, lb, rec_g, b_rec, page_table, lam, att_g, b_att, bmask, proj3, cache_kt, cache_v2,
            state_rec[l], nrec=nrec, dk=dk, dv=dv, nheads=nheads, out_scale=out_scale)

        att, k_t, v_p = _attn_prompt(proj, lam, att_g, b_att, nheads=nheads, out_scale=out_scale)
        hp = _finish(hp, att, rec, w_out_bf, ffn_g, w_up_bf, w_down_bf)
        outs[0].append(jnp.transpose(k_t.reshape(nheads, 2, head_dim, tp), (3, 0, 1, 2))
                       .reshape(bp, tp, nheads, 2, head_dim))
        outs[1].append(v_p.reshape(bp, tp, nheads, vd))
        outs[2].append(s_new.reshape(bp, nrec, dk, dv).astype(state_rec.dtype))

        hs = _finish(hs, att_s, rec_s, w_out_bf, ffn_g, w_up_bf, w_down_bf)
        outs[3].append(proj3[:, :, aw:2 * aw].reshape(bs, ts, nheads, 2, head_dim))
        outs[4].append(proj3[:, :, 2 * aw:3 * aw].reshape(bs, ts, nheads, vd))
        outs[5].append(s_new_s.astype(state_rec.dtype))

    return (hp.reshape(bp, tp, d), hs,
            jnp.stack(outs[0]), jnp.stack(outs[1]), jnp.stack(outs[2]),
            jnp.stack(outs[3]), jnp.stack(outs[4]), jnp.stack(outs[5]))
```

```python
import functools
import math

import jax
import jax.numpy as jnp
import numpy as np
from jax import lax
from jax.experimental import pallas as pl
from jax.experimental.pallas import tpu as pltpu

F32 = jnp.float32
BF16 = jnp.bfloat16

EPS = 1e-6
ROPE_THETA = 500000.0
MASK_VALUE = -1e30
LANES = 128
MXU_TILE = 256
LOG2E = 1.4426950408889634
MAX_FIXED_OFFSET = 60.0
SCORE_BOUND_SLACK = 1.0 + 2.0 ** -6
VMEM_LIMIT_BYTES = 56 * 1024 * 1024
REC_CHUNK = 64

_NT = (((1,), (1,)), ((), ()))
_TA = (((0,), (0,)), ((), ()))


def _params(*semantics):
    return pltpu.CompilerParams(dimension_semantics=semantics, vmem_limit_bytes=VMEM_LIMIT_BYTES)


def _pick(n, pref):
    if n <= pref:
        return n
    t = pref
    while n % t:
        t //= 2
    return t


def _split3(x):
    h1 = x.astype(BF16)
    r1 = x - h1.astype(F32)
    h2 = r1.astype(BF16)
    h3 = (r1 - h2.astype(F32)).astype(BF16)
    return h1, h2, h3


def _prep_body(lq1_ref, lk1_ref, lq2_ref, lk2_ref, logit_ref, lam_ref, lb_ref, *, lam_inits):
    for l, lam_init in enumerate(lam_inits):
        s1 = jnp.sum(lq1_ref[l:l + 1, :] * lk1_ref[l:l + 1, :], axis=1, keepdims=True)
        s2 = jnp.sum(lq2_ref[l:l + 1, :] * lk2_ref[l:l + 1, :], axis=1, keepdims=True)
        lam = jnp.exp(s1) - jnp.exp(s2) + lam_init
        lam_ref[l:l + 1, :] = jnp.broadcast_to(lam, (1, LANES))
    x = logit_ref[...]
    e = jnp.exp(x - jnp.max(x, axis=0, keepdims=True))
    sm = e / jnp.sum(e, axis=0, keepdims=True)
    acc = jnp.zeros((1, x.shape[1]), F32)
    for r in range(x.shape[0]):
        acc = acc + sm[r:r + 1, :]
        lb_ref[r:r + 1, :] = acc


def _prep(lq1, lk1, lq2, lk2, logits, lam_inits):
    depth = lq1.shape[0]
    return pl.pallas_call(
        functools.partial(_prep_body, lam_inits=lam_inits),
        out_shape=(jax.ShapeDtypeStruct((depth, LANES), F32),
                   jax.ShapeDtypeStruct(logits.shape, F32)),
        name="prep",
    )(lq1, lk1, lq2, lk2, logits)


def _rows2d(x):
    return x.reshape(-1, x.shape[-1]) if x.ndim == 3 else x


def _store_rows(ref, cols, val):
    if len(ref.shape) == 3:
        ref[:, :, cols] = val.reshape(ref.shape[0], ref.shape[1], val.shape[-1])
    else:
        ref[:, cols] = val


def _proj_body(x_ref, g_ref, w_ref, qkg_ref, gmat_ref, cos_ref, sa_ref, sb_ref, o_ref, xn_ref,
               *, head_dim, rot_half):
    j = pl.program_id(1)

    @pl.when(j == 0)
    def _():
        x = _rows2d(x_ref[...])
        ms = jnp.mean(x * x, axis=-1, keepdims=True)
        xn_ref[...] = (x * lax.rsqrt(ms + EPS) * g_ref[...]).astype(BF16)

    @pl.when(j < 2)
    def _():
        y = jnp.dot(xn_ref[...], w_ref[...], preferred_element_type=F32)
        gm = gmat_ref[...]
        gt = gm.shape[0]
        gain = qkg_ref[0]
        c, sa, sb = cos_ref[...], sa_ref[...], sb_ref[...]
        for t in range(y.shape[1] // gt):
            yt = y[:, t * gt:(t + 1) * gt]
            y2 = yt * yt
            hi = y2.astype(BF16)
            lo = (y2 - hi.astype(F32)).astype(BF16)
            ss = (jnp.dot(hi, gm, preferred_element_type=F32)
                  + jnp.dot(lo, gm, preferred_element_type=F32))
            yn = yt * lax.rsqrt(ss * (1.0 / head_dim) + EPS) * gain[:, t * gt:(t + 1) * gt]
            for u in range(gt // LANES):
                yu = yn[:, u * LANES:(u + 1) * LANES]
                lane0 = t * gt + u * LANES
                _store_rows(o_ref, slice(lane0, lane0 + LANES),
                            yu * c + pltpu.roll(yu, LANES - rot_half, 1) * sa
                            + pltpu.roll(yu, rot_half, 1) * sb)

    @pl.when(j >= 2)
    def _():
        _store_rows(o_ref, slice(None), jnp.dot(xn_ref[...], w_ref[...], preferred_element_type=F32))


def _proj(x, norm_g, w_bf, qk_gain, gmat, cos_t, sa_t, sb_t, *, head_dim):
    d = x.shape[-1]
    m = math.prod(x.shape[:-1])
    gw = qk_gain.shape[2]
    gt = gmat.shape[0]
    ngroups = w_bf.shape[1] // gw
    tm = _pick(m, 1024)
    assert gw % gt == 0 and gt % LANES == 0
    if x.ndim == 3:
        ts = x.shape[1]
        assert tm % ts == 0
        x_spec = pl.BlockSpec((tm // ts, ts, d), lambda i, j: (i, 0, 0))
        o_spec = pl.BlockSpec((tm // ts, ts, gw), lambda i, j: (i, 0, j))
        o_shape = (x.shape[0], ts, ngroups * gw)
    else:
        x_spec = pl.BlockSpec((tm, d), lambda i, j: (i, 0))
        o_spec = pl.BlockSpec((tm, gw), lambda i, j: (i, j))
        o_shape = (m, ngroups * gw)
    return pl.pallas_call(
        functools.partial(_proj_body, head_dim=head_dim, rot_half=head_dim // 8),
        grid=(m // tm, ngroups),
        in_specs=[
            x_spec,
            pl.BlockSpec((1, d), lambda i, j: (0, 0)),
            pl.BlockSpec((d, gw), lambda i, j: (0, j)),
            pl.BlockSpec((1, 1, gw), lambda i, j: (jnp.minimum(j, 1), 0, 0)),
            pl.BlockSpec((gt, gt), lambda i, j: (0, 0)),
            pl.BlockSpec((tm, LANES), lambda i, j: (i, 0)),
            pl.BlockSpec((tm, LANES), lambda i, j: (i, 0)),
            pl.BlockSpec((tm, LANES), lambda i, j: (i, 0)),
        ],
        out_specs=o_spec,
        out_shape=jax.ShapeDtypeStruct(o_shape, F32),
        scratch_shapes=[pltpu.VMEM((tm, d), BF16)],
        compiler_params=_params("parallel", "arbitrary"),
        name="proj",
    )(x, norm_g, w_bf, qk_gain, gmat, cos_t, sa_t, sb_t)


def _attn_body(lam_ref, q_ref, k_ref, v_ref, g_ref, beta_ref, *rest, tq, wide, out_scale, ncast):
    cast_in = rest[:ncast]
    o_ref, kt_ref, vo_ref = rest[ncast:ncast + 3]
    cast_out = rest[ncast + 3:2 * ncast + 3]
    kb_ref, vb_ref, qs_ref, m_ref, l_ref, acc_ref, kmax_ref = rest[2 * ncast + 3:]
    qi = pl.program_id(1)
    half = LANES // 2
    lane = lax.broadcasted_iota(jnp.int32, (1, LANES), 1)

    for src, dst in zip(cast_in, cast_out):
        dst[...] = src[...].astype(BF16)

    @pl.when(qi == 0)
    def _():
        kb = k_ref[...].astype(BF16)
        kb_ref[...] = kb
        vb_ref[...] = v_ref[...].astype(BF16)
        vo_ref[...] = v_ref[...]
        for i in range(k_ref.shape[0] // tq):
            kt_ref[:, i * tq:(i + 1) * tq] = k_ref[i * tq:(i + 1) * tq, :].T
        k2 = kb.astype(F32)
        k2 = k2 * k2
        for c, sel in enumerate((lane < half, lane >= half)):
            n2 = jnp.sum(jnp.where(sel, k2, 0.0), axis=1, keepdims=True)
            kmax_ref[c:c + 1, :] = jnp.broadcast_to(
                jnp.sqrt(jnp.max(n2, axis=0, keepdims=True)), (1, LANES))

    q = q_ref[...] * (LOG2E / math.sqrt(half))
    qs = (jnp.where(lane < half, q, 0.0).astype(BF16), jnp.where(lane >= half, q, 0.0).astype(BF16))
    bounds = []
    for c in range(2):
        qs_ref[c * tq:(c + 1) * tq, :] = qs[c]
        qf = qs[c].astype(F32)
        qn = jnp.sqrt(jnp.sum(qf * qf, axis=1, keepdims=True))
        bounds.append(qn * kmax_ref[c:c + 1, :] * SCORE_BOUND_SLACK)
    fast = jnp.max(jnp.maximum(bounds[0], bounds[1])) <= MAX_FIXED_OFFSET
    for c in range(2):
        m_ref[c * tq:(c + 1) * tq, :] = jnp.where(fast, bounds[c], MASK_VALUE)
    l_ref[...] = jnp.zeros(l_ref.shape, F32)
    acc_ref[...] = jnp.zeros(acc_ref.shape, F32)

    def block(off, width, diag_at, fixed_offset):
        kblk = kb_ref[pl.ds(off, width), :]
        vblk = vb_ref[pl.ds(off, width), :]
        for c in range(2):
            rs = slice(c * tq, (c + 1) * tq)
            s = lax.dot_general(qs_ref[rs, :], kblk, _NT, preferred_element_type=F32)
            if diag_at is not None:
                row = lax.broadcasted_iota(jnp.int32, (tq, tq), 0)
                col = lax.broadcasted_iota(jnp.int32, (tq, tq), 1)
                tail = jnp.where(col <= row, s[:, diag_at:], MASK_VALUE)
                s = tail if diag_at == 0 else jnp.concatenate([s[:, :diag_at], tail], axis=1)
            m_prev = m_ref[rs, :]
            if fixed_offset:
                p = jnp.exp2(s - jnp.tile(m_prev, (1, width // LANES)))
                l_ref[rs, :] = l_ref[rs, :] + jnp.sum(p, axis=1, keepdims=True)
                acc_ref[rs, :] = acc_ref[rs, :] + jnp.dot(
                    p.astype(BF16), vblk, preferred_element_type=F32)
            else:
                m_next = jnp.maximum(m_prev, jnp.max(s, axis=1, keepdims=True))
                alpha = jnp.exp2(m_prev - m_next)
                p = jnp.exp2(s - jnp.tile(m_next, (1, width // LANES)))
                l_ref[rs, :] = alpha * l_ref[rs, :] + jnp.sum(p, axis=1, keepdims=True)
                acc_ref[rs, :] = alpha * acc_ref[rs, :] + jnp.dot(
                    p.astype(BF16), vblk, preferred_element_type=F32)
                m_ref[rs, :] = m_next

    per_wide = wide // tq
    nwide = qi // per_wide
    rem = qi - nwide * per_wide

    def all_blocks(fixed_offset):
        def wide_body(j, carry):
            block(pl.multiple_of(j * wide, wide), wide, None, fixed_offset)
            return carry

        lax.fori_loop(0, nwide, wide_body, 0)
        for r in range(per_wide):
            @pl.when(rem == r)
            def _(r=r):
                block(pl.multiple_of(nwide * wide, wide), (r + 1) * tq, r * tq, fixed_offset)

    pl.when(fast)(functools.partial(all_blocks, True))
    pl.when(jnp.logical_not(fast))(functools.partial(all_blocks, False))

    a = acc_ref[...] / l_ref[...]
    o = a[0:tq, :] - lam_ref[...] * a[tq:2 * tq, :]
    ms = jnp.mean(o * o, axis=-1, keepdims=True)
    o_ref[...] = (o * lax.rsqrt(ms + EPS) * g_ref[...] * out_scale * beta_ref[...]).astype(o_ref.dtype)


def _attn_prompt(proj, lam, gain, beta, weights, *, nheads, out_scale):
    t = proj.shape[0]
    tq = _pick(t, 512)
    wide = _pick(t, 4 * tq)
    nq = t // tq
    nsteps = nheads * nq
    assert tq % LANES == 0 and wide % tq == 0
    assert all(w.shape[0] % (16 * nsteps) == 0 for w in weights)
    cast_specs = [pl.BlockSpec((w.shape[0] // nsteps, w.shape[1]), lambda h, i: (h * nq + i, 0))
                  for w in weights]
    return pl.pallas_call(
        functools.partial(_attn_body, tq=tq, wide=wide, out_scale=out_scale, ncast=len(weights)),
        grid=(nheads, nq),
        in_specs=[
            pl.BlockSpec((1, LANES), lambda h, i: (0, 0)),
            pl.BlockSpec((tq, LANES), lambda h, i: (i, h)),
            pl.BlockSpec((t, LANES), lambda h, i: (0, nheads + h)),
            pl.BlockSpec((t, LANES), lambda h, i: (0, 2 * nheads + h)),
            pl.BlockSpec((1, LANES), lambda h, i: (0, 0)),
            pl.BlockSpec((1, LANES), lambda h, i: (0, h)),
        ] + cast_specs,
        out_specs=(pl.BlockSpec((tq, LANES), lambda h, i: (i, h)),
                   pl.BlockSpec((LANES, t), lambda h, i: (h, 0)),
                   pl.BlockSpec((t, LANES), lambda h, i: (0, h))) + tuple(cast_specs),
        out_shape=(jax.ShapeDtypeStruct((t, nheads * LANES), BF16),
                   jax.ShapeDtypeStruct((nheads * LANES, t), F32),
                   jax.ShapeDtypeStruct((t, nheads * LANES), F32))
        + tuple(jax.ShapeDtypeStruct(w.shape, BF16) for w in weights),
        scratch_shapes=[
            pltpu.VMEM((t, LANES), BF16),
            pltpu.VMEM((t, LANES), BF16),
            pltpu.VMEM((2 * tq, LANES), BF16),
            pltpu.VMEM((2 * tq, LANES), F32),
            pltpu.VMEM((2 * tq, LANES), F32),
            pltpu.VMEM((2 * tq, LANES), F32),
            pltpu.VMEM((2, LANES), F32),
        ],
        compiler_params=_params("parallel", "arbitrary"),
        name="attn_prompt",
    )(lam, proj, proj, proj, gain, beta, *weights)


def _attn_decode_body(pt_ref, lam_ref, g_ref, beta_ref, bmask_ref, q_ref, kn_ref, vn_ref, *rest,
                      npages, page, nheads, ts, out_scale):
    del pt_ref
    kp_refs = rest[:npages]
    vp_refs = rest[npages:2 * npages]
    o_ref, ks_ref, vs_ref, kn_scr = rest[2 * npages:]
    past = npages * page
    tail, w = kn_scr.shape
    nrow = bmask_ref.shape[0]
    half = LANES // 2

    for p in range(npages):
        ks_ref[:, p * page:(p + 1) * page] = kp_refs[p][...].astype(BF16)
        for h in range(nheads):
            vs_ref[p * page:(p + 1) * page, h * LANES:(h + 1) * LANES] = (
                vp_refs[p][pl.ds(h, page, stride=nheads), :].astype(BF16))
    kn_scr[...] = jnp.zeros((tail, w), BF16)
    kn_scr[0:ts, :] = kn_ref[0].astype(BF16)
    vs_ref[past:, :] = jnp.zeros((tail, w), BF16)
    vs_ref[past:past + ts, :] = vn_ref[0].astype(BF16)

    q = q_ref[0] * (LOG2E / math.sqrt(half))
    qt = (jnp.tile(q, (nrow // ts, 1)) * bmask_ref[...]).astype(BF16)
    s_past = jnp.dot(qt, ks_ref[...], preferred_element_type=F32)
    s_new = lax.dot_general(qt, kn_scr[...], _NT, preferred_element_type=F32)
    row = lax.broadcasted_iota(jnp.int32, s_new.shape, 0)
    col = lax.broadcasted_iota(jnp.int32, s_new.shape, 1)
    s_new = jnp.where(col <= lax.rem(row, ts), s_new, MASK_VALUE)
    m = jnp.maximum(jnp.max(s_past, axis=1, keepdims=True), jnp.max(s_new, axis=1, keepdims=True))
    p_past = jnp.exp2(s_past - m)
    p_new = jnp.exp2(s_new - m)
    l = jnp.sum(p_past, axis=1, keepdims=True) + jnp.sum(p_new, axis=1, keepdims=True)
    row1 = lax.broadcasted_iota(jnp.int32, (nrow, 1), 0)
    hr = nrow // 2
    wgt = jnp.where(row1 < hr, 1.0, -lam_ref[:, 0:1]) / l
    pw_past = p_past * wgt
    pw_new = p_new * wgt
    a_past = (pw_past[0:hr, :] + pw_past[hr:, :]).astype(BF16)
    a_new = (pw_new[0:hr, :] + pw_new[hr:, :]).astype(BF16)
    o2 = (jnp.dot(a_past, vs_ref[0:past, :], preferred_element_type=F32)
          + jnp.dot(a_new, vs_ref[past:, :], preferred_element_type=F32))
    for h in range(nheads):
        cs = slice(h * LANES, (h + 1) * LANES)
        o = o2[h * 8:(h + 1) * 8, cs]
        ms = jnp.mean(o * o, axis=-1, keepdims=True)
        o = o * lax.rsqrt(ms + EPS) * g_ref[...] * out_scale * beta_ref[:, cs]
        o_ref[0, :, cs] = o[0:ts, :].astype(o_ref.dtype)


def _hgrn_body(rq_ref, rf_ref, ri_ref, rg_ref, lb_ref, gn_ref, beta_ref, tri_ref, *rest,
               nheads, dk, chunk, batched):
    if batched:
        s0_ref, rec_ref, sout_ref, st_ref, o_scr = rest
        nseq = rq_ref.shape[0]
    else:
        rec_ref, st_ref, o_scr = rest
        nseq = 1

    lb = lb_ref[...]
    tri = tri_ref[...]
    tr = lax.broadcasted_iota(jnp.int32, (chunk, chunk), 0)
    tc = lax.broadcasted_iota(jnp.int32, (chunk, chunk), 1)
    causal = tc <= tr
    for bi in range(nseq):
        sb = bi * nheads
        if batched:
            rq, rf, ri, rg = rq_ref[bi], rf_ref[bi], ri_ref[bi], rg_ref[bi]
            for h in range(nheads):
                st_ref[sb + h] = s0_ref[bi, h].T
        else:
            rq, rf, ri, rg = rq_ref[...], rf_ref[...], ri_ref[...], rg_ref[...]
        rows = rq.shape[0]
        q = rq * jax.nn.sigmoid(rq)
        fg = lb + (1.0 - lb) * jax.nn.sigmoid(rf)
        logf = jnp.log(fg)
        kk = 1.0 - fg
        h1, h2, h3 = _split3(logf)
        b = (jnp.dot(tri, h1, preferred_element_type=F32)
             + jnp.dot(tri, h2, preferred_element_type=F32)
             + jnp.dot(tri, h3, preferred_element_type=F32))
        q_in = (q * jnp.exp(b)).astype(BF16)
        k_in = (kk * jnp.exp(-b)).astype(BF16)
        v_bf = ri.astype(BF16)
        for c in range(rows // chunk):
            rs = slice(c * chunk, (c + 1) * chunk)
            b_last = b[(c + 1) * chunk - 1:(c + 1) * chunk, :]
            k_st = (kk[rs, :] * jnp.exp(b_last - b[rs, :])).astype(BF16)
            decay = jnp.exp(b_last)
            for h in range(nheads):
                cs = slice(h * dk, (h + 1) * dk)
                qc, kc, vc = q_in[rs, cs], k_in[rs, cs], v_bf[rs, cs]
                att = lax.dot_general(qc, kc, _NT, preferred_element_type=F32)
                att = jnp.where(causal, att, 0.0).astype(BF16)
                st = st_ref[sb + h]
                o_scr[bi, rs, cs] = (
                    jnp.dot(att, vc, preferred_element_type=F32)
                    + lax.dot_general(qc, st.astype(BF16), _NT, preferred_element_type=F32))
                st_ref[sb + h] = st * decay[:, cs] + lax.dot_general(
                    vc, k_st[:, cs], _TA, preferred_element_type=F32)
        gate = rg * jax.nn.sigmoid(rg)
        for h in range(nheads):
            cs = slice(h * dk, (h + 1) * dk)
            o = o_scr[bi, :, cs]
            ms = jnp.mean(o * o, axis=-1, keepdims=True)
            o = (o * lax.rsqrt(ms + EPS) * gn_ref[...]) * gate[:, cs] * beta_ref[:, cs]
            if batched:
                rec_ref[bi, :, cs] = o.astype(rec_ref.dtype)
                sout_ref[bi, h] = st_ref[sb + h].T
            else:
                rec_ref[:, cs] = o.astype(rec_ref.dtype)


def _tri(rows, chunk):
    r = np.arange(rows)
    return jnp.asarray((r[:, None] >= r[None, :]) & (r[:, None] // chunk == r[None, :] // chunk), BF16)


def _hgrn_attn_body(pt_ref, *refs, n_rec_in, n_att_in, n_srec_in, rec_steps, att_steps,
                    rec_kw, att_kw, srec_kw):
    n_in = n_rec_in + n_att_in + n_srec_in
    rec_in, att_in = refs[:n_rec_in], refs[n_rec_in:n_rec_in + n_att_in]
    srec_rows, (srec_tri, srec_s0) = refs[n_in - n_srec_in:n_in - 2], refs[n_in - 2:n_in]
    rec_ref, sout_ref, att_ref, srec_ref, ssout_ref = refs[n_in:n_in + 5]
    st_ref, o_scr, ks_ref, vs_ref, kn_scr, sst_ref, so_scr = refs[n_in + 5:]
    nsteps = max(rec_steps, att_steps)
    step = pl.program_id(0)

    @pl.when(step == 0)
    def _():
        st_ref[...] = jnp.zeros(st_ref.shape, F32)

    def run_rec():
        _hgrn_body(*rec_in, rec_ref, st_ref, o_scr, **rec_kw)

    def run_att():
        _attn_decode_body(pt_ref, *att_in, att_ref, ks_ref, vs_ref, kn_scr, **att_kw)
        _hgrn_body(*srec_rows, *rec_in[4:7], srec_tri, srec_s0, srec_ref, ssout_ref, sst_ref, so_scr,
                   **srec_kw)

    if rec_steps < nsteps:
        pl.when(step < rec_steps)(run_rec)
    else:
        run_rec()
    if att_steps < nsteps:
        pl.when(step < att_steps)(run_att)
    else:
        run_att()

    @pl.when(step == rec_steps - 1)
    def _():
        for h in range(sout_ref.shape[0]):
            sout_ref[h] = st_ref[h].T


def _hgrn_attn_fused(proj, lb, rec_gain, rec_beta, page_table, lam, att_gain, att_beta, bmask,
                     proj3, cache_kt, cache_v2, s0, *, nrec, dk, dv, nheads, out_scale):
    t = proj.shape[0]
    gw = nrec * dk
    chunk = math.gcd(t, REC_CHUNK)
    nb, ts, _ = proj3.shape
    schunk = math.gcd(ts, REC_CHUNK)
    npages = page_table.shape[1]
    w, page = cache_kt.shape[1], cache_kt.shape[2]
    nrow = bmask.shape[0]
    tail = LANES
    assert dk == dv == LANES and schunk == ts
    assert 8 % ts == 0 and w == nheads * LANES and cache_v2.shape[1:] == (page * nheads, LANES)
    rows = chunk * _pick(t // chunk, max(1, min(4, (t // chunk) // nb)))
    rec_steps, att_steps = t // rows, nb
    nsteps = max(rec_steps, att_steps)

    def rstep(i):
        return jnp.minimum(i, rec_steps - 1) if rec_steps < nsteps else i

    def bstep(i):
        return jnp.minimum(i, att_steps - 1) if att_steps < nsteps else i

    def k_spec(p):
        return pl.BlockSpec((None, w, page), lambda i, pt: (pt[bstep(i), p], 0, 0))

    def v_spec(p):
        return pl.BlockSpec((None, page * nheads, LANES), lambda i, pt: (pt[bstep(i), p], 0, 0))

    rec_specs = [pl.BlockSpec((rows, gw), functools.partial(lambda i, pt, g: (rstep(i), g), g=3 + g))
                 for g in range(4)] + [
        pl.BlockSpec((1, gw), lambda i, pt: (0, 0)),
        pl.BlockSpec((1, dv), lambda i, pt: (0, 0)),
        pl.BlockSpec((1, gw), lambda i, pt: (0, 0)),
        pl.BlockSpec((rows, rows), lambda i, pt: (0, 0)),
    ]
    att_specs = [
        pl.BlockSpec((1, LANES), lambda i, pt: (0, 0)),
        pl.BlockSpec((1, LANES), lambda i, pt: (0, 0)),
        pl.BlockSpec((1, w), lambda i, pt: (0, 0)),
        pl.BlockSpec((nrow, w), lambda i, pt: (0, 0)),
        pl.BlockSpec((1, ts, w), lambda i, pt: (bstep(i), 0, 0)),
        pl.BlockSpec((1, ts, w), lambda i, pt: (bstep(i), 0, 1)),
        pl.BlockSpec((1, ts, w), lambda i, pt: (bstep(i), 0, 2)),
    ] + [k_spec(p) for p in range(npages)] + [v_spec(p) for p in range(npages)]
    srec_specs = [pl.BlockSpec((1, ts, gw), functools.partial(lambda i, pt, g: (bstep(i), 0, g), g=3 + g))
                  for g in range(4)] + [
        pl.BlockSpec((ts, ts), lambda i, pt: (0, 0)),
        pl.BlockSpec((1, nrec, dk, dv), lambda i, pt: (bstep(i), 0, 0, 0)),
    ]
    grid_spec = pltpu.PrefetchScalarGridSpec(
        num_scalar_prefetch=1,
        grid=(nsteps,),
        in_specs=rec_specs + att_specs + srec_specs,
        out_specs=(pl.BlockSpec((rows, gw), lambda i, pt: (rstep(i), 0)),
                   pl.BlockSpec((nrec, dk, dv), lambda i, pt: (0, 0, 0)),
                   pl.BlockSpec((1, ts, w), lambda i, pt: (bstep(i), 0, 0)),
                   pl.BlockSpec((1, ts, gw), lambda i, pt: (bstep(i), 0, 0)),
                   pl.BlockSpec((1, nrec, dk, dv), lambda i, pt: (bstep(i), 0, 0, 0))),
        scratch_shapes=[
            pltpu.VMEM((nrec, dv, dk), F32),
            pltpu.VMEM((1, rows, gw), F32),
            pltpu.VMEM((w, npages * page), BF16),
            pltpu.VMEM((npages * page + tail, w), BF16),
            pltpu.VMEM((tail, w), BF16),
            pltpu.VMEM((nrec, dv, dk), F32),
            pltpu.VMEM((1, ts, gw), F32),
        ],
    )
    return pl.pallas_call(
        functools.partial(
            _hgrn_attn_body, n_rec_in=len(rec_specs), n_att_in=len(att_specs),
            n_srec_in=len(srec_specs), rec_steps=rec_steps, att_steps=att_steps,
            rec_kw=dict(nheads=nrec, dk=dk, chunk=chunk, batched=False),
            att_kw=dict(npages=npages, page=page, nheads=nheads, ts=ts, out_scale=out_scale),
            srec_kw=dict(nheads=nrec, dk=dk, chunk=schunk, batched=True)),
        grid_spec=grid_spec,
        out_shape=(jax.ShapeDtypeStruct((t, gw), BF16),
                   jax.ShapeDtypeStruct((nrec, dk, dv), F32),
                   jax.ShapeDtypeStruct((nb, ts, w), BF16),
                   jax.ShapeDtypeStruct((nb, ts, gw), BF16),
                   jax.ShapeDtypeStruct((nb, nrec, dk, dv), F32)),
        compiler_params=_params("arbitrary"),
        name="hgrn_attn_fused",
    )(page_table, proj, proj, proj, proj, lb, rec_gain, rec_beta, _tri(rows, chunk),
      lam, att_gain, att_beta, bmask, proj3, proj3, proj3,
      *([cache_kt] * npages), *([cache_v2] * npages),
      proj3, proj3, proj3, proj3, _tri(ts, schunk), s0)


def _finish_body(x_ref, att_ref, rec_ref, wo_ref, g_ref, wu_ref, wd_ref, o_ref, hn_ref):
    j = pl.program_id(1)

    @pl.when(j == 0)
    def _():
        aw = att_ref.shape[-1]
        h = (_rows2d(x_ref[...])
             + jnp.dot(_rows2d(att_ref[...]), wo_ref[0:aw, :], preferred_element_type=F32)
             + jnp.dot(_rows2d(rec_ref[...]), wo_ref[aw:, :], preferred_element_type=F32))
        ms = jnp.mean(h * h, axis=-1, keepdims=True)
        hn_ref[...] = (h * lax.rsqrt(ms + EPS) * g_ref[...]).astype(BF16)
        o_ref[...] = h.reshape(o_ref.shape)

    u = jnp.maximum(jnp.dot(hn_ref[...], wu_ref[...], preferred_element_type=F32), 0.0)
    o_ref[...] += jnp.dot((u * u).astype(BF16), wd_ref[...],
                          preferred_element_type=F32).reshape(o_ref.shape)


def _finish(x, att, rec, wo_bf, ffn_g, wu_bf, wd_bf):
    d, aw, rw = x.shape[-1], att.shape[-1], rec.shape[-1]
    m = math.prod(x.shape[:-1])
    ff = wu_bf.shape[1]
    tm = _pick(m, 512)
    fc = _pick(ff, 1024)
    if x.ndim == 3:
        ts = x.shape[1]
        assert tm % ts == 0 and att.shape[:2] == rec.shape[:2] == x.shape[:2]
        row_spec = lambda n: pl.BlockSpec((tm // ts, ts, n), lambda i, j: (i, 0, 0))
    else:
        row_spec = lambda n: pl.BlockSpec((tm, n), lambda i, j: (i, 0))
    return pl.pallas_call(
        _finish_body,
        grid=(m // tm, ff // fc),
        in_specs=[
            row_spec(d),
            row_spec(aw),
            row_spec(rw),
            pl.BlockSpec((aw + rw, d), lambda i, j: (0, 0), pipeline_mode=pl.Buffered(1)),
            pl.BlockSpec((1, d), lambda i, j: (0, 0)),
            pl.BlockSpec((d, fc), lambda i, j: (0, j)),
            pl.BlockSpec((fc, d), lambda i, j: (j, 0)),
        ],
        out_specs=row_spec(d),
        out_shape=jax.ShapeDtypeStruct(x.shape, F32),
        scratch_shapes=[pltpu.VMEM((tm, d), BF16)],
        compiler_params=_params("parallel", "arbitrary"),
        name="finish",
    )(x, att, rec, wo_bf, ffn_g, wu_bf, wd_bf)


def _rope_tables(pos, head_dim):
    rot = head_dim // 4
    half = rot // 2
    inv = ROPE_THETA ** (-np.arange(half, dtype=np.float64) * 2.0 / rot)
    ang = np.asarray(pos, np.float64)[:, None] * inv[None, :]
    cos, sin = np.cos(ang), np.sin(ang)
    ones = np.ones((ang.shape[0], head_dim - rot))
    zeros = np.zeros((ang.shape[0], head_dim - rot))
    zh = np.zeros_like(sin)
    c = np.concatenate([cos, cos, ones], axis=1)
    sa = np.concatenate([-sin, zh, zeros], axis=1)
    sb = np.concatenate([zh, sin, zeros], axis=1)
    reps = LANES // head_dim
    return tuple(np.tile(a, (1, reps)).astype(np.float32) for a in (c, sa, sb))


def kernel(x_prompt, x_sample, cache_k, cache_v, state_rec, page_table, attn_norm, w_in, q_norm, k_norm,
           lambda_q1, lambda_k1, lambda_q2, lambda_k2, att_out_norm, rec_lb_logits, rec_out_norm,
           beta_att, beta_rec, w_out, ffn_norm, w_up, w_down):
    bp, tp, d = x_prompt.shape
    bs, ts, _ = x_sample.shape
    depth = w_in.shape[0]
    n_pool, page, nheads, _, head_dim = cache_k.shape[1:]
    vd = cache_v.shape[-1]
    nrec, dk, dv = state_rec.shape[2:]
    npages = page_table.shape[1]
    past = npages * page
    aw = nheads * vd
    gw = nrec * dk
    assert bp == 1 and 2 * head_dim == vd == LANES and aw == gw and w_in.shape[2] == 7 * gw

    lam_inits = tuple(0.8 - 0.6 * math.exp(-0.3 * l) for l in range(depth))
    lam_all, lb_all = _prep(lambda_q1, lambda_k1, lambda_q2, lambda_k2, rec_lb_logits, lam_inits)

    lane_map = np.arange(MXU_TILE) // head_dim
    gmat = jnp.asarray(lane_map[:, None] == lane_map[None, :], BF16)
    rope_p = _rope_tables(np.arange(tp), head_dim)
    rope_s = tuple(np.tile(a, (bs, 1)) for a in _rope_tables(past + np.arange(ts), head_dim))
    srow, kcol = np.arange(2 * nheads * 8)[:, None], np.arange(aw)[None, :]
    bmask = ((srow // (nheads * 8) == (kcol // head_dim) % 2)
             & ((srow // 8) % nheads == kcol // vd)).astype(np.float32)

    hp = x_prompt.reshape(tp, d)
    hs = x_sample
    outs = [[] for _ in range(6)]
    for l in range(depth):
        w_in_bf = w_in[l].astype(BF16)
        qk_gain = jnp.stack([jnp.tile(q_norm[l], aw // head_dim),
                             jnp.tile(k_norm[l], aw // head_dim)]).reshape(2, 1, aw)
        norm_g = attn_norm[l].reshape(1, d)
        lam = lam_all[l:l + 1]
        lb = lb_all[l:l + 1]
        att_g = att_out_norm[l].reshape(1, vd)
        rec_g = rec_out_norm[l].reshape(1, dv)
        b_att = beta_att[l].reshape(1, aw)
        b_rec = beta_rec[l].reshape(1, gw)
        ffn_g = ffn_norm[l].reshape(1, d)
        out_scale = 1.0 - lam_inits[l]

        proj = _proj(hp, norm_g, w_in_bf, qk_gain, gmat, *rope_p, head_dim=head_dim)
        proj3 = _proj(hs, norm_g, w_in_bf, qk_gain, gmat, *rope_s, head_dim=head_dim)
        cache_kt = jnp.transpose(cache_k[l], (0, 2, 3, 4, 1)).reshape(n_pool, aw, page)
        cache_v2 = cache_v[l].reshape(n_pool, page * nheads, vd)
        rec, s_new, att_s, rec_s, s_new_s = _hgrn_attn_fused(
            proj, lb, rec_g, b_rec, page_table, lam, att_g, b_att, bmask, proj3, cache_kt, cache_v2,
            state_rec[l], nrec=nrec, dk=dk, dv=dv, nheads=nheads, out_scale=out_scale)

        att, k_t, v_p, w_out_bf, w_up_bf, w_down_bf = _attn_prompt(
            proj, lam, att_g, b_att, (w_out[l], w_up[l], w_down[l]), nheads=nheads, out_scale=out_scale)
        hp = _finish(hp, att, rec, w_out_bf, ffn_g, w_up_bf, w_down_bf)
        outs[0].append(jnp.transpose(k_t.reshape(nheads, 2, head_dim, tp), (3, 0, 1, 2))
                       .reshape(bp, tp, nheads, 2, head_dim))
        outs[1].append(v_p.reshape(bp, tp, nheads, vd))
        outs[2].append(s_new.reshape(bp, nrec, dk, dv).astype(state_rec.dtype))

        hs = _finish(hs, att_s, rec_s, w_out_bf, ffn_g, w_up_bf, w_down_bf)
        outs[3].append(proj3[:, :, aw:2 * aw].reshape(bs, ts, nheads, 2, head_dim))
        outs[4].append(proj3[:, :, 2 * aw:3 * aw].reshape(bs, ts, nheads, vd))
        outs[5].append(s_new_s.astype(state_rec.dtype))

    return (hp.reshape(bp, tp, d), hs,
            jnp.stack(outs[0]), jnp.stack(outs[1]), jnp.stack(outs[2]),
            jnp.stack(outs[3]), jnp.stack(outs[4]), jnp.stack(outs[5]))
```

```python
import functools
import math

import jax
import jax.numpy as jnp
import numpy as np
from jax import lax
from jax.experimental import pallas as pl
from jax.experimental.pallas import tpu as pltpu

F32 = jnp.float32
BF16 = jnp.bfloat16

EPS = 1e-6
ROPE_THETA = 500000.0
MASK_VALUE = -1e30
LANES = 128
MXU_TILE = 256
LOG2E = 1.4426950408889634
MAX_FIXED_OFFSET = 60.0
SCORE_BOUND_SLACK = 1.0 + 2.0 ** -6
VMEM_LIMIT_BYTES = 56 * 1024 * 1024
REC_CHUNK = 64

_NT = (((1,), (1,)), ((), ()))
_TA = (((0,), (0,)), ((), ()))


def _params(*semantics):
    return pltpu.CompilerParams(dimension_semantics=semantics, vmem_limit_bytes=VMEM_LIMIT_BYTES)


def _pick(n, pref):
    if n <= pref:
        return n
    t = pref
    while n % t:
        t //= 2
    return t


def _split3(x):
    h1 = x.astype(BF16)
    r1 = x - h1.astype(F32)
    h2 = r1.astype(BF16)
    h3 = (r1 - h2.astype(F32)).astype(BF16)
    return h1, h2, h3


def _prep_body(lq1_ref, lk1_ref, lq2_ref, lk2_ref, logit_ref, lam_ref, lb_ref, *, lam_inits):
    for l, lam_init in enumerate(lam_inits):
        s1 = jnp.sum(lq1_ref[l:l + 1, :] * lk1_ref[l:l + 1, :], axis=1, keepdims=True)
        s2 = jnp.sum(lq2_ref[l:l + 1, :] * lk2_ref[l:l + 1, :], axis=1, keepdims=True)
        lam = jnp.exp(s1) - jnp.exp(s2) + lam_init
        lam_ref[l:l + 1, :] = jnp.broadcast_to(lam, (1, LANES))
    x = logit_ref[...]
    e = jnp.exp(x - jnp.max(x, axis=0, keepdims=True))
    sm = e / jnp.sum(e, axis=0, keepdims=True)
    acc = jnp.zeros((1, x.shape[1]), F32)
    for r in range(x.shape[0]):
        acc = acc + sm[r:r + 1, :]
        lb_ref[r:r + 1, :] = acc


def _prep(lq1, lk1, lq2, lk2, logits, lam_inits):
    depth = lq1.shape[0]
    return pl.pallas_call(
        functools.partial(_prep_body, lam_inits=lam_inits),
        out_shape=(jax.ShapeDtypeStruct((depth, LANES), F32),
                   jax.ShapeDtypeStruct(logits.shape, F32)),
        name="prep",
    )(lq1, lk1, lq2, lk2, logits)


def _rows2d(x):
    return x.reshape(-1, x.shape[-1]) if x.ndim == 3 else x


def _store_rows(ref, cols, val):
    if len(ref.shape) == 3:
        ref[:, :, cols] = val.reshape(ref.shape[0], ref.shape[1], val.shape[-1])
    else:
        ref[:, cols] = val


def _proj_body(x_ref, g_ref, w_ref, qkg_ref, gmat_ref, cos_ref, sa_ref, sb_ref, o_ref, xn_ref,
               *, head_dim, rot_half):
    j = pl.program_id(1)

    @pl.when(j == 0)
    def _():
        x = _rows2d(x_ref[...])
        ms = jnp.mean(x * x, axis=-1, keepdims=True)
        xn_ref[...] = (x * lax.rsqrt(ms + EPS) * g_ref[...]).astype(BF16)

    @pl.when(j < 2)
    def _():
        y = jnp.dot(xn_ref[...], w_ref[...], preferred_element_type=F32)
        gm = gmat_ref[...]
        gt = gm.shape[0]
        gain = qkg_ref[0]
        c, sa, sb = cos_ref[...], sa_ref[...], sb_ref[...]
        for t in range(y.shape[1] // gt):
            yt = y[:, t * gt:(t + 1) * gt]
            y2 = yt * yt
            hi = y2.astype(BF16)
            lo = (y2 - hi.astype(F32)).astype(BF16)
            ss = (jnp.dot(hi, gm, preferred_element_type=F32)
                  + jnp.dot(lo, gm, preferred_element_type=F32))
            yn = yt * lax.rsqrt(ss * (1.0 / head_dim) + EPS) * gain[:, t * gt:(t + 1) * gt]
            for u in range(gt // LANES):
                yu = yn[:, u * LANES:(u + 1) * LANES]
                lane0 = t * gt + u * LANES
                _store_rows(o_ref, slice(lane0, lane0 + LANES),
                            yu * c + pltpu.roll(yu, LANES - rot_half, 1) * sa
                            + pltpu.roll(yu, rot_half, 1) * sb)

    @pl.when(j >= 2)
    def _():
        _store_rows(o_ref, slice(None), jnp.dot(xn_ref[...], w_ref[...], preferred_element_type=F32))


def _proj(x, norm_g, w_bf, qk_gain, gmat, cos_t, sa_t, sb_t, *, head_dim):
    d = x.shape[-1]
    m = math.prod(x.shape[:-1])
    gw = qk_gain.shape[2]
    gt = gmat.shape[0]
    ngroups = w_bf.shape[1] // gw
    tm = _pick(m, 1024)
    assert gw % gt == 0 and gt % LANES == 0
    if x.ndim == 3:
        ts = x.shape[1]
        assert tm % ts == 0
        x_spec = pl.BlockSpec((tm // ts, ts, d), lambda i, j: (i, 0, 0))
        o_spec = pl.BlockSpec((tm // ts, ts, gw), lambda i, j: (i, 0, j))
        o_shape = (x.shape[0], ts, ngroups * gw)
    else:
        x_spec = pl.BlockSpec((tm, d), lambda i, j: (i, 0))
        o_spec = pl.BlockSpec((tm, gw), lambda i, j: (i, j))
        o_shape = (m, ngroups * gw)
    return pl.pallas_call(
        functools.partial(_proj_body, head_dim=head_dim, rot_half=head_dim // 8),
        grid=(m // tm, ngroups),
        in_specs=[
            x_spec,
            pl.BlockSpec((1, d), lambda i, j: (0, 0)),
            pl.BlockSpec((d, gw), lambda i, j: (0, j)),
            pl.BlockSpec((1, 1, gw), lambda i, j: (jnp.minimum(j, 1), 0, 0)),
            pl.BlockSpec((gt, gt), lambda i, j: (0, 0)),
            pl.BlockSpec((tm, LANES), lambda i, j: (i, 0)),
            pl.BlockSpec((tm, LANES), lambda i, j: (i, 0)),
            pl.BlockSpec((tm, LANES), lambda i, j: (i, 0)),
        ],
        out_specs=o_spec,
        out_shape=jax.ShapeDtypeStruct(o_shape, F32),
        scratch_shapes=[pltpu.VMEM((tm, d), BF16)],
        compiler_params=_params("parallel", "arbitrary"),
        name="proj",
    )(x, norm_g, w_bf, qk_gain, gmat, cos_t, sa_t, sb_t)


def _attn_body(lam_ref, q_ref, k_ref, v_ref, g_ref, beta_ref, *rest, tq, wide, out_scale, ncast):
    cast_in = rest[:ncast]
    o_ref, kt_ref, vo_ref = rest[ncast:ncast + 3]
    cast_out = rest[ncast + 3:2 * ncast + 3]
    kb_ref, vb_ref, qs_ref, m_ref, l_ref, acc_ref, kmax_ref = rest[2 * ncast + 3:]
    qi = pl.program_id(1)
    half = LANES // 2
    lane = lax.broadcasted_iota(jnp.int32, (1, LANES), 1)

    for src, dst in zip(cast_in, cast_out):
        dst[...] = src[...].astype(BF16)

    @pl.when(qi == 0)
    def _():
        kb = k_ref[...].astype(BF16)
        kb_ref[...] = kb
        vb_ref[...] = v_ref[...].astype(BF16)
        vo_ref[...] = v_ref[...]
        for i in range(k_ref.shape[0] // tq):
            kt_ref[:, i * tq:(i + 1) * tq] = k_ref[i * tq:(i + 1) * tq, :].T
        k2 = kb.astype(F32)
        k2 = k2 * k2
        ri = lax.broadcasted_iota(jnp.int32, (LANES, LANES), 0)
        ci = lax.broadcasted_iota(jnp.int32, (LANES, LANES), 1)
        ones_blk = jnp.where((ri >= half) == (ci >= half), 1.0, 0.0).astype(BF16)
        n2 = jnp.max(jnp.dot(k2.astype(BF16), ones_blk, preferred_element_type=F32), axis=0, keepdims=True)
        for c, sel in enumerate((lane < half, lane >= half)):
            kmax_ref[c:c + 1, :] = jnp.broadcast_to(
                jnp.sqrt(jnp.max(jnp.where(sel, n2, 0.0), axis=1, keepdims=True)), (1, LANES))

    q = q_ref[...] * (LOG2E / math.sqrt(half))
    qs = (jnp.where(lane < half, q, 0.0).astype(BF16), jnp.where(lane >= half, q, 0.0).astype(BF16))
    bounds = []
    for c in range(2):
        qs_ref[c * tq:(c + 1) * tq, :] = qs[c]
        qf = qs[c].astype(F32)
        qn2 = jnp.sum(qf * qf, axis=1, keepdims=True)
        qn = qn2 * lax.rsqrt(jnp.maximum(qn2, 1e-30))
        bounds.append(qn * kmax_ref[c:c + 1, :] * SCORE_BOUND_SLACK)
    fast = jnp.max(jnp.maximum(bounds[0], bounds[1])) <= MAX_FIXED_OFFSET
    for c in range(2):
        m_ref[c * tq:(c + 1) * tq, :] = jnp.where(fast, bounds[c], MASK_VALUE)
    l_ref[...] = jnp.zeros(l_ref.shape, F32)
    acc_ref[...] = jnp.zeros(acc_ref.shape, F32)

    def block(off, width, diag_at, fixed_offset):
        kblk = kb_ref[pl.ds(off, width), :]
        vblk = vb_ref[pl.ds(off, width), :]
        for c in range(2):
            rs = slice(c * tq, (c + 1) * tq)
            s = lax.dot_general(qs_ref[rs, :], kblk, _NT, preferred_element_type=F32)
            if diag_at is not None:
                row = lax.broadcasted_iota(jnp.int32, (tq, tq), 0)
                col = lax.broadcasted_iota(jnp.int32, (tq, tq), 1)
                tail = jnp.where(col <= row, s[:, diag_at:], MASK_VALUE)
                s = tail if diag_at == 0 else jnp.concatenate([s[:, :diag_at], tail], axis=1)
            m_prev = m_ref[rs, :]
            if fixed_offset:
                p = jnp.exp2(s - jnp.tile(m_prev, (1, width // LANES)))
                l_ref[rs, :] = l_ref[rs, :] + jnp.sum(p, axis=1, keepdims=True)
                acc_ref[rs, :] = acc_ref[rs, :] + jnp.dot(
                    p.astype(BF16), vblk, preferred_element_type=F32)
            else:
                m_next = jnp.maximum(m_prev, jnp.max(s, axis=1, keepdims=True))
                alpha = jnp.exp2(m_prev - m_next)
                p = jnp.exp2(s - jnp.tile(m_next, (1, width // LANES)))
                l_ref[rs, :] = alpha * l_ref[rs, :] + jnp.sum(p, axis=1, keepdims=True)
                acc_ref[rs, :] = alpha * acc_ref[rs, :] + jnp.dot(
                    p.astype(BF16), vblk, preferred_element_type=F32)
                m_ref[rs, :] = m_next

    per_wide = wide // tq
    nwide = qi // per_wide
    rem = qi - nwide * per_wide

    def all_blocks(fixed_offset):
        def wide_body(j, carry):
            block(pl.multiple_of(j * wide, wide), wide, None, fixed_offset)
            return carry

        lax.fori_loop(0, nwide, wide_body, 0)
        for r in range(per_wide):
            @pl.when(rem == r)
            def _(r=r):
                block(pl.multiple_of(nwide * wide, wide), (r + 1) * tq, r * tq, fixed_offset)

    pl.when(fast)(functools.partial(all_blocks, True))
    pl.when(jnp.logical_not(fast))(functools.partial(all_blocks, False))

    a = acc_ref[...] / l_ref[...]
    o = a[0:tq, :] - lam_ref[...] * a[tq:2 * tq, :]
    ms = jnp.mean(o * o, axis=-1, keepdims=True)
    o_ref[...] = (o * lax.rsqrt(ms + EPS) * g_ref[...] * out_scale * beta_ref[...]).astype(o_ref.dtype)


def _attn_prompt(proj, lam, gain, beta, weights, *, nheads, out_scale):
    t = proj.shape[0]
    tq = _pick(t, 512)
    wide = _pick(t, 4 * tq)
    nq = t // tq
    nsteps = nheads * nq
    assert tq % LANES == 0 and wide % tq == 0
    assert all(w.shape[0] % (16 * nsteps) == 0 for w in weights)
    cast_specs = [pl.BlockSpec((w.shape[0] // nsteps, w.shape[1]), lambda h, i: (h * nq + i, 0))
                  for w in weights]
    return pl.pallas_call(
        functools.partial(_attn_body, tq=tq, wide=wide, out_scale=out_scale, ncast=len(weights)),
        grid=(nheads, nq),
        in_specs=[
            pl.BlockSpec((1, LANES), lambda h, i: (0, 0)),
            pl.BlockSpec((tq, LANES), lambda h, i: (i, h)),
            pl.BlockSpec((t, LANES), lambda h, i: (0, nheads + h)),
            pl.BlockSpec((t, LANES), lambda h, i: (0, 2 * nheads + h)),
            pl.BlockSpec((1, LANES), lambda h, i: (0, 0)),
            pl.BlockSpec((1, LANES), lambda h, i: (0, h)),
        ] + cast_specs,
        out_specs=(pl.BlockSpec((tq, LANES), lambda h, i: (i, h)),
                   pl.BlockSpec((LANES, t), lambda h, i: (h, 0)),
                   pl.BlockSpec((t, LANES), lambda h, i: (0, h))) + tuple(cast_specs),
        out_shape=(jax.ShapeDtypeStruct((t, nheads * LANES), BF16),
                   jax.ShapeDtypeStruct((nheads * LANES, t), F32),
                   jax.ShapeDtypeStruct((t, nheads * LANES), F32))
        + tuple(jax.ShapeDtypeStruct(w.shape, BF16) for w in weights),
        scratch_shapes=[
            pltpu.VMEM((t, LANES), BF16),
            pltpu.VMEM((t, LANES), BF16),
            pltpu.VMEM((2 * tq, LANES), BF16),
            pltpu.VMEM((2 * tq, LANES), F32),
            pltpu.VMEM((2 * tq, LANES), F32),
            pltpu.VMEM((2 * tq, LANES), F32),
            pltpu.VMEM((2, LANES), F32),
        ],
        compiler_params=_params("parallel", "arbitrary"),
        name="attn_prompt",
    )(lam, proj, proj, proj, gain, beta, *weights)


def _attn_decode_body(pt_ref, lam_ref, g_ref, beta_ref, bmask_ref, q_ref, kn_ref, vn_ref, *rest,
                      npages, page, nheads, ts, out_scale):
    del pt_ref
    kp_refs = rest[:npages]
    vp_refs = rest[npages:2 * npages]
    o_ref, ks_ref, vs_ref, kn_scr = rest[2 * npages:]
    past = npages * page
    tail, w = kn_scr.shape
    nrow = bmask_ref.shape[0]
    half = LANES // 2

    for p in range(npages):
        ks_ref[:, p * page:(p + 1) * page] = kp_refs[p][...].astype(BF16)
        for h in range(nheads):
            vs_ref[p * page:(p + 1) * page, h * LANES:(h + 1) * LANES] = (
                vp_refs[p][pl.ds(h, page, stride=nheads), :].astype(BF16))
    kn_scr[...] = jnp.zeros((tail, w), BF16)
    kn_scr[0:ts, :] = kn_ref[0].astype(BF16)
    vs_ref[past:, :] = jnp.zeros((tail, w), BF16)
    vs_ref[past:past + ts, :] = vn_ref[0].astype(BF16)

    q = q_ref[0] * (LOG2E / math.sqrt(half))
    qt = (jnp.tile(q, (nrow // ts, 1)) * bmask_ref[...]).astype(BF16)
    s_past = jnp.dot(qt, ks_ref[...], preferred_element_type=F32)
    s_new = lax.dot_general(qt, kn_scr[...], _NT, preferred_element_type=F32)
    row = lax.broadcasted_iota(jnp.int32, s_new.shape, 0)
    col = lax.broadcasted_iota(jnp.int32, s_new.shape, 1)
    s_new = jnp.where(col <= lax.rem(row, ts), s_new, MASK_VALUE)
    m = jnp.maximum(jnp.max(s_past, axis=1, keepdims=True), jnp.max(s_new, axis=1, keepdims=True))
    p_past = jnp.exp2(s_past - m)
    p_new = jnp.exp2(s_new - m)
    l = jnp.sum(p_past, axis=1, keepdims=True) + jnp.sum(p_new, axis=1, keepdims=True)
    row1 = lax.broadcasted_iota(jnp.int32, (nrow, 1), 0)
    hr = nrow // 2
    wgt = jnp.where(row1 < hr, 1.0, -lam_ref[:, 0:1]) / l
    pw_past = p_past * wgt
    pw_new = p_new * wgt
    a_past = (pw_past[0:hr, :] + pw_past[hr:, :]).astype(BF16)
    a_new = (pw_new[0:hr, :] + pw_new[hr:, :]).astype(BF16)
    o2 = (jnp.dot(a_past, vs_ref[0:past, :], preferred_element_type=F32)
          + jnp.dot(a_new, vs_ref[past:, :], preferred_element_type=F32))
    for h in range(nheads):
        cs = slice(h * LANES, (h + 1) * LANES)
        o = o2[h * 8:(h + 1) * 8, cs]
        ms = jnp.mean(o * o, axis=-1, keepdims=True)
        o = o * lax.rsqrt(ms + EPS) * g_ref[...] * out_scale * beta_ref[:, cs]
        o_ref[0, :, cs] = o[0:ts, :].astype(o_ref.dtype)


def _hgrn_body(rq_ref, rf_ref, ri_ref, rg_ref, lb_ref, gn_ref, beta_ref, tri_ref, *rest,
               nheads, dk, chunk, batched):
    if batched:
        s0_ref, rec_ref, sout_ref, st_ref, o_scr = rest
        nseq = rq_ref.shape[0]
    else:
        rec_ref, st_ref, o_scr = rest
        nseq = 1

    lb = lb_ref[...]
    tri = tri_ref[...]
    tr = lax.broadcasted_iota(jnp.int32, (chunk, chunk), 0)
    tc = lax.broadcasted_iota(jnp.int32, (chunk, chunk), 1)
    causal = tc <= tr
    for bi in range(nseq):
        sb = bi * nheads
        if batched:
            rq, rf, ri, rg = rq_ref[bi], rf_ref[bi], ri_ref[bi], rg_ref[bi]
            for h in range(nheads):
                st_ref[sb + h] = s0_ref[bi, h].T
        else:
            rq, rf, ri, rg = rq_ref[...], rf_ref[...], ri_ref[...], rg_ref[...]
        rows = rq.shape[0]
        q = rq * jax.nn.sigmoid(rq)
        fg = lb + (1.0 - lb) * jax.nn.sigmoid(rf)
        logf = jnp.log(fg)
        kk = 1.0 - fg
        h1, h2, h3 = _split3(logf)
        b = (jnp.dot(tri, h1, preferred_element_type=F32)
             + jnp.dot(tri, h2, preferred_element_type=F32)
             + jnp.dot(tri, h3, preferred_element_type=F32))
        q_in = (q * jnp.exp(b)).astype(BF16)
        k_in = (kk * jnp.exp(-b)).astype(BF16)
        v_bf = ri.astype(BF16)
        for c in range(rows // chunk):
            rs = slice(c * chunk, (c + 1) * chunk)
            b_last = b[(c + 1) * chunk - 1:(c + 1) * chunk, :]
            k_st = (kk[rs, :] * jnp.exp(b_last - b[rs, :])).astype(BF16)
            decay = jnp.exp(b_last)
            for h in range(nheads):
                cs = slice(h * dk, (h + 1) * dk)
                qc, kc, vc = q_in[rs, cs], k_in[rs, cs], v_bf[rs, cs]
                att = lax.dot_general(qc, kc, _NT, preferred_element_type=F32)
                att = jnp.where(causal, att, 0.0).astype(BF16)
                st = st_ref[sb + h]
                o_scr[bi, rs, cs] = (
                    jnp.dot(att, vc, preferred_element_type=F32)
                    + lax.dot_general(qc, st.astype(BF16), _NT, preferred_element_type=F32))
                st_ref[sb + h] = st * decay[:, cs] + lax.dot_general(
                    vc, k_st[:, cs], _TA, preferred_element_type=F32)
        gate = rg * jax.nn.sigmoid(rg)
        for h in range(nheads):
            cs = slice(h * dk, (h + 1) * dk)
            o = o_scr[bi, :, cs]
            ms = jnp.mean(o * o, axis=-1, keepdims=True)
            o = (o * lax.rsqrt(ms + EPS) * gn_ref[...]) * gate[:, cs] * beta_ref[:, cs]
            if batched:
                rec_ref[bi, :, cs] = o.astype(rec_ref.dtype)
                sout_ref[bi, h] = st_ref[sb + h].T
            else:
                rec_ref[:, cs] = o.astype(rec_ref.dtype)


def _tri(rows, chunk):
    r = np.arange(rows)
    return jnp.asarray((r[:, None] >= r[None, :]) & (r[:, None] // chunk == r[None, :] // chunk), BF16)


def _hgrn_attn_body(pt_ref, *refs, n_rec_in, n_att_in, n_srec_in, rec_steps, att_steps,
                    rec_kw, att_kw, srec_kw):
    n_in = n_rec_in + n_att_in + n_srec_in
    rec_in, att_in = refs[:n_rec_in], refs[n_rec_in:n_rec_in + n_att_in]
    srec_rows, (srec_tri, srec_s0) = refs[n_in - n_srec_in:n_in - 2], refs[n_in - 2:n_in]
    rec_ref, sout_ref, att_ref, srec_ref, ssout_ref = refs[n_in:n_in + 5]
    st_ref, o_scr, ks_ref, vs_ref, kn_scr, sst_ref, so_scr = refs[n_in + 5:]
    nsteps = max(rec_steps, att_steps)
    step = pl.program_id(0)

    @pl.when(step == 0)
    def _():
        st_ref[...] = jnp.zeros(st_ref.shape, F32)

    def run_rec():
        _hgrn_body(*rec_in, rec_ref, st_ref, o_scr, **rec_kw)

    def run_att():
        _attn_decode_body(pt_ref, *att_in, att_ref, ks_ref, vs_ref, kn_scr, **att_kw)
        _hgrn_body(*srec_rows, *rec_in[4:7], srec_tri, srec_s0, srec_ref, ssout_ref, sst_ref, so_scr,
                   **srec_kw)

    if rec_steps < nsteps:
        pl.when(step < rec_steps)(run_rec)
    else:
        run_rec()
    if att_steps < nsteps:
        pl.when(step < att_steps)(run_att)
    else:
        run_att()

    @pl.when(step == rec_steps - 1)
    def _():
        for h in range(sout_ref.shape[0]):
            sout_ref[h] = st_ref[h].T


def _hgrn_attn_fused(proj, lb, rec_gain, rec_beta, page_table, lam, att_gain, att_beta, bmask,
                     proj3, cache_kt, cache_v2, s0, *, nrec, dk, dv, nheads, out_scale):
    t = proj.shape[0]
    gw = nrec * dk
    chunk = math.gcd(t, REC_CHUNK)
    nb, ts, _ = proj3.shape
    schunk = math.gcd(ts, REC_CHUNK)
    npages = page_table.shape[1]
    w, page = cache_kt.shape[1], cache_kt.shape[2]
    nrow = bmask.shape[0]
    tail = LANES
    assert dk == dv == LANES and schunk == ts
    assert 8 % ts == 0 and w == nheads * LANES and cache_v2.shape[1:] == (page * nheads, LANES)
    rows = chunk * _pick(t // chunk, max(1, min(4, (t // chunk) // nb)))
    rec_steps, att_steps = t // rows, nb
    nsteps = max(rec_steps, att_steps)

    def rstep(i):
        return jnp.minimum(i, rec_steps - 1) if rec_steps < nsteps else i

    def bstep(i):
        return jnp.minimum(i, att_steps - 1) if att_steps < nsteps else i

    def k_spec(p):
        return pl.BlockSpec((None, w, page), lambda i, pt: (pt[bstep(i), p], 0, 0))

    def v_spec(p):
        return pl.BlockSpec((None, page * nheads, LANES), lambda i, pt: (pt[bstep(i), p], 0, 0))

    rec_specs = [pl.BlockSpec((rows, gw), functools.partial(lambda i, pt, g: (rstep(i), g), g=3 + g))
                 for g in range(4)] + [
        pl.BlockSpec((1, gw), lambda i, pt: (0, 0)),
        pl.BlockSpec((1, dv), lambda i, pt: (0, 0)),
        pl.BlockSpec((1, gw), lambda i, pt: (0, 0)),
        pl.BlockSpec((rows, rows), lambda i, pt: (0, 0)),
    ]
    att_specs = [
        pl.BlockSpec((1, LANES), lambda i, pt: (0, 0)),
        pl.BlockSpec((1, LANES), lambda i, pt: (0, 0)),
        pl.BlockSpec((1, w), lambda i, pt: (0, 0)),
        pl.BlockSpec((nrow, w), lambda i, pt: (0, 0)),
        pl.BlockSpec((1, ts, w), lambda i, pt: (bstep(i), 0, 0)),
        pl.BlockSpec((1, ts, w), lambda i, pt: (bstep(i), 0, 1)),
        pl.BlockSpec((1, ts, w), lambda i, pt: (bstep(i), 0, 2)),
    ] + [k_spec(p) for p in range(npages)] + [v_spec(p) for p in range(npages)]
    srec_specs = [pl.BlockSpec((1, ts, gw), functools.partial(lambda i, pt, g: (bstep(i), 0, g), g=3 + g))
                  for g in range(4)] + [
        pl.BlockSpec((ts, ts), lambda i, pt: (0, 0)),
        pl.BlockSpec((1, nrec, dk, dv), lambda i, pt: (bstep(i), 0, 0, 0)),
    ]
    grid_spec = pltpu.PrefetchScalarGridSpec(
        num_scalar_prefetch=1,
        grid=(nsteps,),
        in_specs=rec_specs + att_specs + srec_specs,
        out_specs=(pl.BlockSpec((rows, gw), lambda i, pt: (rstep(i), 0)),
                   pl.BlockSpec((nrec, dk, dv), lambda i, pt: (0, 0, 0)),
                   pl.BlockSpec((1, ts, w), lambda i, pt: (bstep(i), 0, 0)),
                   pl.BlockSpec((1, ts, gw), lambda i, pt: (bstep(i), 0, 0)),
                   pl.BlockSpec((1, nrec, dk, dv), lambda i, pt: (bstep(i), 0, 0, 0))),
        scratch_shapes=[
            pltpu.VMEM((nrec, dv, dk), F32),
            pltpu.VMEM((1, rows, gw), F32),
            pltpu.VMEM((w, npages * page), BF16),
            pltpu.VMEM((npages * page + tail, w), BF16),
            pltpu.VMEM((tail, w), BF16),
            pltpu.VMEM((nrec, dv, dk), F32),
            pltpu.VMEM((1, ts, gw), F32),
        ],
    )
    return pl.pallas_call(
        functools.partial(
            _hgrn_attn_body, n_rec_in=len(rec_specs), n_att_in=len(att_specs),
            n_srec_in=len(srec_specs), rec_steps=rec_steps, att_steps=att_steps,
            rec_kw=dict(nheads=nrec, dk=dk, chunk=chunk, batched=False),
            att_kw=dict(npages=npages, page=page, nheads=nheads, ts=ts, out_scale=out_scale),
            srec_kw=dict(nheads=nrec, dk=dk, chunk=schunk, batched=True)),
        grid_spec=grid_spec,
        out_shape=(jax.ShapeDtypeStruct((t, gw), BF16),
                   jax.ShapeDtypeStruct((nrec, dk, dv), F32),
                   jax.ShapeDtypeStruct((nb, ts, w), BF16),
                   jax.ShapeDtypeStruct((nb, ts, gw), BF16),
                   jax.ShapeDtypeStruct((nb, nrec, dk, dv), F32)),
        compiler_params=_params("arbitrary"),
        name="hgrn_attn_fused",
    )(page_table, proj, proj, proj, proj, lb, rec_gain, rec_beta, _tri(rows, chunk),
      lam, att_gain, att_beta, bmask, proj3, proj3, proj3,
      *([cache_kt] * npages), *([cache_v2] * npages),
      proj3, proj3, proj3, proj3, _tri(ts, schunk), s0)


def _finish_body(x_ref, att_ref, rec_ref, wo_ref, g_ref, wu_ref, wd_ref, o_ref, hn_ref):
    j = pl.program_id(1)

    @pl.when(j == 0)
    def _():
        aw = att_ref.shape[-1]
        h = (_rows2d(x_ref[...])
             + jnp.dot(_rows2d(att_ref[...]), wo_ref[0:aw, :], preferred_element_type=F32)
             + jnp.dot(_rows2d(rec_ref[...]), wo_ref[aw:, :], preferred_element_type=F32))
        ms = jnp.mean(h * h, axis=-1, keepdims=True)
        hn_ref[...] = (h * lax.rsqrt(ms + EPS) * g_ref[...]).astype(BF16)
        o_ref[...] = h.reshape(o_ref.shape)

    u = jnp.maximum(jnp.dot(hn_ref[...], wu_ref[...], preferred_element_type=F32), 0.0)
    o_ref[...] += jnp.dot((u * u).astype(BF16), wd_ref[...],
                          preferred_element_type=F32).reshape(o_ref.shape)


def _finish(x, att, rec, wo_bf, ffn_g, wu_bf, wd_bf):
    d, aw, rw = x.shape[-1], att.shape[-1], rec.shape[-1]
    m = math.prod(x.shape[:-1])
    ff = wu_bf.shape[1]
    tm = _pick(m, 512)
    fc = _pick(ff, 1024)
    if x.ndim == 3:
        ts = x.shape[1]
        assert tm % ts == 0 and att.shape[:2] == rec.shape[:2] == x.shape[:2]
        row_spec = lambda n: pl.BlockSpec((tm // ts, ts, n), lambda i, j: (i, 0, 0))
    else:
        row_spec = lambda n: pl.BlockSpec((tm, n), lambda i, j: (i, 0))
    return pl.pallas_call(
        _finish_body,
        grid=(m // tm, ff // fc),
        in_specs=[
            row_spec(d),
            row_spec(aw),
            row_spec(rw),
            pl.BlockSpec((aw + rw, d), lambda i, j: (0, 0), pipeline_mode=pl.Buffered(1)),
            pl.BlockSpec((1, d), lambda i, j: (0, 0)),
            pl.BlockSpec((d, fc), lambda i, j: (0, j)),
            pl.BlockSpec((fc, d), lambda i, j: (j, 0)),
        ],
        out_specs=row_spec(d),
        out_shape=jax.ShapeDtypeStruct(x.shape, F32),
        scratch_shapes=[pltpu.VMEM((tm, d), BF16)],
        compiler_params=_params("parallel", "arbitrary"),
        name="finish",
    )(x, att, rec, wo_bf, ffn_g, wu_bf, wd_bf)


def _rope_tables(pos, head_dim):
    rot = head_dim // 4
    half = rot // 2
    inv = ROPE_THETA ** (-np.arange(half, dtype=np.float64) * 2.0 / rot)
    ang = np.asarray(pos, np.float64)[:, None] * inv[None, :]
    cos, sin = np.cos(ang), np.sin(ang)
    ones = np.ones((ang.shape[0], head_dim - rot))
    zeros = np.zeros((ang.shape[0], head_dim - rot))
    zh = np.zeros_like(sin)
    c = np.concatenate([cos, cos, ones], axis=1)
    sa = np.concatenate([-sin, zh, zeros], axis=1)
    sb = np.concatenate([zh, sin, zeros], axis=1)
    reps = LANES // head_dim
    return tuple(np.tile(a, (1, reps)).astype(np.float32) for a in (c, sa, sb))


def kernel(x_prompt, x_sample, cache_k, cache_v, state_rec, page_table, attn_norm, w_in, q_norm, k_norm,
           lambda_q1, lambda_k1, lambda_q2, lambda_k2, att_out_norm, rec_lb_logits, rec_out_norm,
           beta_att, beta_rec, w_out, ffn_norm, w_up, w_down):
    bp, tp, d = x_prompt.shape
    bs, ts, _ = x_sample.shape
    depth = w_in.shape[0]
    n_pool, page, nheads, _, head_dim = cache_k.shape[1:]
    vd = cache_v.shape[-1]
    nrec, dk, dv = state_rec.shape[2:]
    npages = page_table.shape[1]
    past = npages * page
    aw = nheads * vd
    gw = nrec * dk
    assert bp == 1 and 2 * head_dim == vd == LANES and aw == gw and w_in.shape[2] == 7 * gw

    lam_inits = tuple(0.8 - 0.6 * math.exp(-0.3 * l) for l in range(depth))
    lam_all, lb_all = _prep(lambda_q1, lambda_k1, lambda_q2, lambda_k2, rec_lb_logits, lam_inits)

    lane_map = np.arange(MXU_TILE) // head_dim
    gmat = jnp.asarray(lane_map[:, None] == lane_map[None, :], BF16)
    rope_p = _rope_tables(np.arange(tp), head_dim)
    rope_s = tuple(np.tile(a, (bs, 1)) for a in _rope_tables(past + np.arange(ts), head_dim))
    srow, kcol = np.arange(2 * nheads * 8)[:, None], np.arange(aw)[None, :]
    bmask = ((srow // (nheads * 8) == (kcol // head_dim) % 2)
             & ((srow // 8) % nheads == kcol // vd)).astype(np.float32)

    hp = x_prompt.reshape(tp, d)
    hs = x_sample
    outs = [[] for _ in range(6)]
    for l in range(depth):
        w_in_bf = w_in[l].astype(BF16)
        qk_gain = jnp.stack([jnp.tile(q_norm[l], aw // head_dim),
                             jnp.tile(k_norm[l], aw // head_dim)]).reshape(2, 1, aw)
        norm_g = attn_norm[l].reshape(1, d)
        lam = lam_all[l:l + 1]
        lb = lb_all[l:l + 1]
        att_g = att_out_norm[l].reshape(1, vd)
        rec_g = rec_out_norm[l].reshape(1, dv)
        b_att = beta_att[l].reshape(1, aw)
        b_rec = beta_rec[l].reshape(1, gw)
        ffn_g = ffn_norm[l].reshape(1, d)
        out_scale = 1.0 - lam_inits[l]

        proj = _proj(hp, norm_g, w_in_bf, qk_gain, gmat, *rope_p, head_dim=head_dim)
        proj3 = _proj(hs, norm_g, w_in_bf, qk_gain, gmat, *rope_s, head_dim=head_dim)
        cache_kt = jnp.transpose(cache_k[l], (0, 2, 3, 4, 1)).reshape(n_pool, aw, page)
        cache_v2 = cache_v[l].reshape(n_pool, page * nheads, vd)
        rec, s_new, att_s, rec_s, s_new_s = _hgrn_attn_fused(
            proj, lb, rec_g, b_rec, page_table, lam, att_g, b_att, bmask, proj3, cache_kt, cache_v2,
            state_rec[l], nrec=nrec, dk=dk, dv=dv, nheads=nheads, out_scale=out_scale)

        att, k_t, v_p, w_out_bf, w_up_bf, w_down_bf = _attn_prompt(
            proj, lam, att_g, b_att, (w_out[l], w_up[l], w_down[l]), nheads=nheads, out_scale=out_scale)
        hp = _finish(hp, att, rec, w_out_bf, ffn_g, w_up_bf, w_down_bf)
        outs[0].append(jnp.transpose(k_t.reshape(nheads, 2, head_dim, tp), (3, 0, 1, 2))
                       .reshape(bp, tp, nheads, 2, head_dim))
        outs[1].append(v_p.reshape(bp, tp, nheads, vd))
        outs[2].append(s_new.reshape(bp, nrec, dk, dv).astype(state_rec.dtype))

        hs = _finish(hs, att_s, rec_s, w_out_bf, ffn_g, w_up_bf, w_down_bf)
        outs[3].append(proj3[:, :, aw:2 * aw].reshape(bs, ts, nheads, 2, head_dim))
        outs[4].append(proj3[:, :, 2 * aw:3 * aw].reshape(bs, ts, nheads, vd))
        outs[5].append(s_new_s.astype(state_rec.dtype))

    return (hp.reshape(bp, tp, d), hs,
            jnp.stack(outs[0]), jnp.stack(outs[1]), jnp.stack(outs[2]),
            jnp.stack(outs[3]), jnp.stack(outs[4]), jnp.stack(outs[5]))
```

```python
import functools
import math

import jax
import jax.numpy as jnp
import numpy as np
from jax import lax
from jax.experimental import pallas as pl
from jax.experimental.pallas import tpu as pltpu

F32 = jnp.float32
BF16 = jnp.bfloat16

EPS = 1e-6
ROPE_THETA = 500000.0
MASK_VALUE = -1e30
LANES = 128
MXU_TILE = 256
LOG2E = 1.4426950408889634
MAX_FIXED_OFFSET = 60.0
SCORE_BOUND_SLACK = 1.0 + 2.0 ** -6
VMEM_LIMIT_BYTES = 56 * 1024 * 1024
REC_CHUNK = 64

_NT = (((1,), (1,)), ((), ()))
_TA = (((0,), (0,)), ((), ()))


def _params(*semantics):
    return pltpu.CompilerParams(dimension_semantics=semantics, vmem_limit_bytes=VMEM_LIMIT_BYTES)


def _pick(n, pref):
    if n <= pref:
        return n
    t = pref
    while n % t:
        t //= 2
    return t


def _split3(x):
    h1 = x.astype(BF16)
    r1 = x - h1.astype(F32)
    h2 = r1.astype(BF16)
    h3 = (r1 - h2.astype(F32)).astype(BF16)
    return h1, h2, h3


def _prep_body(lq1_ref, lk1_ref, lq2_ref, lk2_ref, logit_ref, lam_ref, lb_ref, *, lam_inits):
    for l, lam_init in enumerate(lam_inits):
        s1 = jnp.sum(lq1_ref[l:l + 1, :] * lk1_ref[l:l + 1, :], axis=1, keepdims=True)
        s2 = jnp.sum(lq2_ref[l:l + 1, :] * lk2_ref[l:l + 1, :], axis=1, keepdims=True)
        lam = jnp.exp(s1) - jnp.exp(s2) + lam_init
        lam_ref[l:l + 1, :] = jnp.broadcast_to(lam, (1, LANES))
    x = logit_ref[...]
    e = jnp.exp(x - jnp.max(x, axis=0, keepdims=True))
    sm = e / jnp.sum(e, axis=0, keepdims=True)
    acc = jnp.zeros((1, x.shape[1]), F32)
    for r in range(x.shape[0]):
        acc = acc + sm[r:r + 1, :]
        lb_ref[r:r + 1, :] = acc


def _prep(lq1, lk1, lq2, lk2, logits, lam_inits):
    depth = lq1.shape[0]
    return pl.pallas_call(
        functools.partial(_prep_body, lam_inits=lam_inits),
        out_shape=(jax.ShapeDtypeStruct((depth, LANES), F32),
                   jax.ShapeDtypeStruct(logits.shape, F32)),
        name="prep",
    )(lq1, lk1, lq2, lk2, logits)


def _rows2d(x):
    return x.reshape(-1, x.shape[-1]) if x.ndim == 3 else x


def _store_rows(ref, cols, val):
    if len(ref.shape) == 3:
        ref[:, :, cols] = val.reshape(ref.shape[0], ref.shape[1], val.shape[-1])
    else:
        ref[:, cols] = val


def _proj_body(x_ref, g_ref, w_ref, qkg_ref, gmat_ref, cos_ref, sa_ref, sb_ref, o_ref, *rest,
               head_dim, rot_half):
    j = pl.program_id(1)
    w = w_ref[...]
    if len(rest) == 2:
        wbf_ref, xn_ref = rest
        w = w.astype(BF16)
        wbf_ref[...] = w
    else:
        xn_ref, = rest

    @pl.when(j == 0)
    def _():
        x = _rows2d(x_ref[...])
        ms = jnp.mean(x * x, axis=-1, keepdims=True)
        xn_ref[...] = (x * lax.rsqrt(ms + EPS) * g_ref[...]).astype(BF16)

    @pl.when(j < 2)
    def _():
        y = jnp.dot(xn_ref[...], w, preferred_element_type=F32)
        gm = gmat_ref[...]
        gt = gm.shape[0]
        gain = qkg_ref[0]
        c, sa, sb = cos_ref[...], sa_ref[...], sb_ref[...]
        for t in range(y.shape[1] // gt):
            yt = y[:, t * gt:(t + 1) * gt]
            y2 = yt * yt
            hi = y2.astype(BF16)
            lo = (y2 - hi.astype(F32)).astype(BF16)
            ss = (jnp.dot(hi, gm, preferred_element_type=F32)
                  + jnp.dot(lo, gm, preferred_element_type=F32))
            yn = yt * lax.rsqrt(ss * (1.0 / head_dim) + EPS) * gain[:, t * gt:(t + 1) * gt]
            for u in range(gt // LANES):
                yu = yn[:, u * LANES:(u + 1) * LANES]
                lane0 = t * gt + u * LANES
                _store_rows(o_ref, slice(lane0, lane0 + LANES),
                            yu * c + pltpu.roll(yu, LANES - rot_half, 1) * sa
                            + pltpu.roll(yu, rot_half, 1) * sb)

    @pl.when(j >= 2)
    def _():
        _store_rows(o_ref, slice(None), jnp.dot(xn_ref[...], w, preferred_element_type=F32))


def _proj(x, norm_g, w, qk_gain, gmat, cos_t, sa_t, sb_t, *, head_dim):
    d = x.shape[-1]
    m = math.prod(x.shape[:-1])
    gw = qk_gain.shape[2]
    gt = gmat.shape[0]
    ngroups = w.shape[1] // gw
    tm = _pick(m, 1024)
    emit_w = w.dtype == F32
    assert gw % gt == 0 and gt % LANES == 0 and (m == tm or not emit_w)
    if x.ndim == 3:
        ts = x.shape[1]
        assert tm % ts == 0
        x_spec = pl.BlockSpec((tm // ts, ts, d), lambda i, j: (i, 0, 0))
        o_spec = pl.BlockSpec((tm // ts, ts, gw), lambda i, j: (i, 0, j))
        o_shape = (x.shape[0], ts, ngroups * gw)
    else:
        x_spec = pl.BlockSpec((tm, d), lambda i, j: (i, 0))
        o_spec = pl.BlockSpec((tm, gw), lambda i, j: (i, j))
        o_shape = (m, ngroups * gw)
    return pl.pallas_call(
        functools.partial(_proj_body, head_dim=head_dim, rot_half=head_dim // 8),
        grid=(m // tm, ngroups),
        in_specs=[
            x_spec,
            pl.BlockSpec((1, d), lambda i, j: (0, 0)),
            pl.BlockSpec((d, gw), lambda i, j: (0, j)),
            pl.BlockSpec((1, 1, gw), lambda i, j: (jnp.minimum(j, 1), 0, 0)),
            pl.BlockSpec((gt, gt), lambda i, j: (0, 0)),
            pl.BlockSpec((tm, LANES), lambda i, j: (i, 0)),
            pl.BlockSpec((tm, LANES), lambda i, j: (i, 0)),
            pl.BlockSpec((tm, LANES), lambda i, j: (i, 0)),
        ],
        out_specs=(o_spec, pl.BlockSpec((d, gw), lambda i, j: (0, j))) if emit_w else o_spec,
        out_shape=((jax.ShapeDtypeStruct(o_shape, F32), jax.ShapeDtypeStruct(w.shape, BF16))
                   if emit_w else jax.ShapeDtypeStruct(o_shape, F32)),
        scratch_shapes=[pltpu.VMEM((tm, d), BF16)],
        compiler_params=_params("parallel", "arbitrary"),
        name="proj",
    )(x, norm_g, w, qk_gain, gmat, cos_t, sa_t, sb_t)


def _attn_body(lam_ref, q_ref, k_ref, v_ref, g_ref, beta_ref, *rest, tq, wide, out_scale, ncast):
    cast_in = rest[:ncast]
    o_ref, kt_ref, vo_ref = rest[ncast:ncast + 3]
    cast_out = rest[ncast + 3:2 * ncast + 3]
    kb_ref, vb_ref, qs_ref, m_ref, l_ref, acc_ref, kmax_ref = rest[2 * ncast + 3:]
    qi = pl.program_id(1)
    half = LANES // 2
    lane = lax.broadcasted_iota(jnp.int32, (1, LANES), 1)

    for src, dst in zip(cast_in, cast_out):
        dst[...] = src[...].astype(BF16)

    @pl.when(qi == 0)
    def _():
        kb = k_ref[...].astype(BF16)
        kb_ref[...] = kb
        vb_ref[...] = v_ref[...].astype(BF16)
        vo_ref[...] = v_ref[...]
        for i in range(k_ref.shape[0] // tq):
            kt_ref[:, i * tq:(i + 1) * tq] = k_ref[i * tq:(i + 1) * tq, :].T
        k2 = kb.astype(F32)
        k2 = k2 * k2
        ri = lax.broadcasted_iota(jnp.int32, (LANES, LANES), 0)
        ci = lax.broadcasted_iota(jnp.int32, (LANES, LANES), 1)
        ones_blk = jnp.where((ri >= half) == (ci >= half), 1.0, 0.0).astype(BF16)
        n2 = jnp.max(jnp.dot(k2.astype(BF16), ones_blk, preferred_element_type=F32), axis=0, keepdims=True)
        for c, sel in enumerate((lane < half, lane >= half)):
            kmax_ref[c:c + 1, :] = jnp.broadcast_to(
                jnp.sqrt(jnp.max(jnp.where(sel, n2, 0.0), axis=1, keepdims=True)), (1, LANES))

    q = q_ref[...] * (LOG2E / math.sqrt(half))
    qs = (jnp.where(lane < half, q, 0.0).astype(BF16), jnp.where(lane >= half, q, 0.0).astype(BF16))
    bounds = []
    for c in range(2):
        qs_ref[c * tq:(c + 1) * tq, :] = qs[c]
        qf = qs[c].astype(F32)
        qn2 = jnp.sum(qf * qf, axis=1, keepdims=True)
        qn = qn2 * lax.rsqrt(jnp.maximum(qn2, 1e-30))
        bounds.append(qn * kmax_ref[c:c + 1, :] * SCORE_BOUND_SLACK)
    fast = jnp.max(jnp.maximum(bounds[0], bounds[1])) <= MAX_FIXED_OFFSET
    for c in range(2):
        m_ref[c * tq:(c + 1) * tq, :] = jnp.where(fast, bounds[c], MASK_VALUE)
    l_ref[...] = jnp.zeros(l_ref.shape, F32)
    acc_ref[...] = jnp.zeros(acc_ref.shape, F32)

    def block(off, width, diag_at, fixed_offset):
        kblk = kb_ref[pl.ds(off, width), :]
        vblk = vb_ref[pl.ds(off, width), :]
        for c in range(2):
            rs = slice(c * tq, (c + 1) * tq)
            s = lax.dot_general(qs_ref[rs, :], kblk, _NT, preferred_element_type=F32)
            if diag_at is not None:
                row = lax.broadcasted_iota(jnp.int32, (tq, tq), 0)
                col = lax.broadcasted_iota(jnp.int32, (tq, tq), 1)
                tail = jnp.where(col <= row, s[:, diag_at:], MASK_VALUE)
                s = tail if diag_at == 0 else jnp.concatenate([s[:, :diag_at], tail], axis=1)
            m_prev = m_ref[rs, :]
            if fixed_offset:
                p = jnp.exp2(s - jnp.tile(m_prev, (1, width // LANES)))
                l_ref[rs, :] = l_ref[rs, :] + jnp.sum(p, axis=1, keepdims=True)
                acc_ref[rs, :] = acc_ref[rs, :] + jnp.dot(
                    p.astype(BF16), vblk, preferred_element_type=F32)
            else:
                m_next = jnp.maximum(m_prev, jnp.max(s, axis=1, keepdims=True))
                alpha = jnp.exp2(m_prev - m_next)
                p = jnp.exp2(s - jnp.tile(m_next, (1, width // LANES)))
                l_ref[rs, :] = alpha * l_ref[rs, :] + jnp.sum(p, axis=1, keepdims=True)
                acc_ref[rs, :] = alpha * acc_ref[rs, :] + jnp.dot(
                    p.astype(BF16), vblk, preferred_element_type=F32)
                m_ref[rs, :] = m_next

    per_wide = wide // tq
    nwide = qi // per_wide
    rem = qi - nwide * per_wide

    def all_blocks(fixed_offset):
        def wide_body(j, carry):
            block(pl.multiple_of(j * wide, wide), wide, None, fixed_offset)
            return carry

        lax.fori_loop(0, nwide, wide_body, 0)
        for r in range(per_wide):
            @pl.when(rem == r)
            def _(r=r):
                block(pl.multiple_of(nwide * wide, wide), (r + 1) * tq, r * tq, fixed_offset)

    pl.when(fast)(functools.partial(all_blocks, True))
    pl.when(jnp.logical_not(fast))(functools.partial(all_blocks, False))

    a = acc_ref[...] / l_ref[...]
    o = a[0:tq, :] - lam_ref[...] * a[tq:2 * tq, :]
    ms = jnp.mean(o * o, axis=-1, keepdims=True)
    o_ref[...] = (o * lax.rsqrt(ms + EPS) * g_ref[...] * out_scale * beta_ref[...]).astype(o_ref.dtype)


def _attn_prompt(proj, lam, gain, beta, weights, *, nheads, out_scale):
    t = proj.shape[0]
    tq = _pick(t, 512)
    wide = _pick(t, 4 * tq)
    nq = t // tq
    nsteps = nheads * nq
    assert tq % LANES == 0 and wide % tq == 0
    assert all(w.shape[0] % (16 * nsteps) == 0 for w in weights)
    cast_specs = [pl.BlockSpec((w.shape[0] // nsteps, w.shape[1]), lambda h, i: (h * nq + i, 0))
                  for w in weights]
    return pl.pallas_call(
        functools.partial(_attn_body, tq=tq, wide=wide, out_scale=out_scale, ncast=len(weights)),
        grid=(nheads, nq),
        in_specs=[
            pl.BlockSpec((1, LANES), lambda h, i: (0, 0)),
            pl.BlockSpec((tq, LANES), lambda h, i: (i, h)),
            pl.BlockSpec((t, LANES), lambda h, i: (0, nheads + h)),
            pl.BlockSpec((t, LANES), lambda h, i: (0, 2 * nheads + h)),
            pl.BlockSpec((1, LANES), lambda h, i: (0, 0)),
            pl.BlockSpec((1, LANES), lambda h, i: (0, h)),
        ] + cast_specs,
        out_specs=(pl.BlockSpec((tq, LANES), lambda h, i: (i, h)),
                   pl.BlockSpec((LANES, t), lambda h, i: (h, 0)),
                   pl.BlockSpec((t, LANES), lambda h, i: (0, h))) + tuple(cast_specs),
        out_shape=(jax.ShapeDtypeStruct((t, nheads * LANES), BF16),
                   jax.ShapeDtypeStruct((nheads * LANES, t), F32),
                   jax.ShapeDtypeStruct((t, nheads * LANES), F32))
        + tuple(jax.ShapeDtypeStruct(w.shape, BF16) for w in weights),
        scratch_shapes=[
            pltpu.VMEM((t, LANES), BF16),
            pltpu.VMEM((t, LANES), BF16),
            pltpu.VMEM((2 * tq, LANES), BF16),
            pltpu.VMEM((2 * tq, LANES), F32),
            pltpu.VMEM((2 * tq, LANES), F32),
            pltpu.VMEM((2 * tq, LANES), F32),
            pltpu.VMEM((2, LANES), F32),
        ],
        compiler_params=_params("parallel", "arbitrary"),
        name="attn_prompt",
    )(lam, proj, proj, proj, gain, beta, *weights)


def _attn_decode_body(pt_ref, lam_ref, g_ref, beta_ref, bmask_ref, q_ref, kn_ref, vn_ref, *rest,
                      npages, page, nheads, ts, out_scale):
    del pt_ref
    kp_refs = rest[:npages]
    vp_refs = rest[npages:2 * npages]
    o_ref, ks_ref, vs_ref, kn_scr = rest[2 * npages:]
    past = npages * page
    tail, w = kn_scr.shape
    nrow = bmask_ref.shape[0]
    half = LANES // 2

    for p in range(npages):
        ks_ref[:, p * page:(p + 1) * page] = kp_refs[p][...].astype(BF16)
        for h in range(nheads):
            vs_ref[p * page:(p + 1) * page, h * LANES:(h + 1) * LANES] = (
                vp_refs[p][pl.ds(h, page, stride=nheads), :].astype(BF16))
    kn_scr[...] = jnp.zeros((tail, w), BF16)
    kn_scr[0:ts, :] = kn_ref[0].astype(BF16)
    vs_ref[past:, :] = jnp.zeros((tail, w), BF16)
    vs_ref[past:past + ts, :] = vn_ref[0].astype(BF16)

    q = q_ref[0] * (LOG2E / math.sqrt(half))
    qt = (jnp.tile(q, (nrow // ts, 1)) * bmask_ref[...]).astype(BF16)
    s_past = jnp.dot(qt, ks_ref[...], preferred_element_type=F32)
    s_new = lax.dot_general(qt, kn_scr[...], _NT, preferred_element_type=F32)
    row = lax.broadcasted_iota(jnp.int32, s_new.shape, 0)
    col = lax.broadcasted_iota(jnp.int32, s_new.shape, 1)
    s_new = jnp.where(col <= lax.rem(row, ts), s_new, MASK_VALUE)
    m = jnp.maximum(jnp.max(s_past, axis=1, keepdims=True), jnp.max(s_new, axis=1, keepdims=True))
    p_past = jnp.exp2(s_past - m)
    p_new = jnp.exp2(s_new - m)
    l = jnp.sum(p_past, axis=1, keepdims=True) + jnp.sum(p_new, axis=1, keepdims=True)
    row1 = lax.broadcasted_iota(jnp.int32, (nrow, 1), 0)
    hr = nrow // 2
    wgt = jnp.where(row1 < hr, 1.0, -lam_ref[:, 0:1]) / l
    pw_past = p_past * wgt
    pw_new = p_new * wgt
    a_past = (pw_past[0:hr, :] + pw_past[hr:, :]).astype(BF16)
    a_new = (pw_new[0:hr, :] + pw_new[hr:, :]).astype(BF16)
    o2 = (jnp.dot(a_past, vs_ref[0:past, :], preferred_element_type=F32)
          + jnp.dot(a_new, vs_ref[past:, :], preferred_element_type=F32))
    for h in range(nheads):
        cs = slice(h * LANES, (h + 1) * LANES)
        o = o2[h * 8:(h + 1) * 8, cs]
        ms = jnp.mean(o * o, axis=-1, keepdims=True)
        o = o * lax.rsqrt(ms + EPS) * g_ref[...] * out_scale * beta_ref[:, cs]
        o_ref[0, :, cs] = o[0:ts, :].astype(o_ref.dtype)


def _hgrn_body(rq_ref, rf_ref, ri_ref, rg_ref, lb_ref, gn_ref, beta_ref, tri_ref, *rest,
               nheads, dk, chunk, batched):
    if batched:
        s0_ref, rec_ref, sout_ref, st_ref, o_scr = rest
        nseq = rq_ref.shape[0]
    else:
        rec_ref, st_ref, o_scr = rest
        nseq = 1

    lb = lb_ref[...]
    tri = tri_ref[...]
    tr = lax.broadcasted_iota(jnp.int32, (chunk, chunk), 0)
    tc = lax.broadcasted_iota(jnp.int32, (chunk, chunk), 1)
    causal = tc <= tr
    for bi in range(nseq):
        sb = bi * nheads
        if batched:
            rq, rf, ri, rg = rq_ref[bi], rf_ref[bi], ri_ref[bi], rg_ref[bi]
            for h in range(nheads):
                st_ref[sb + h] = s0_ref[bi, h].T
        else:
            rq, rf, ri, rg = rq_ref[...], rf_ref[...], ri_ref[...], rg_ref[...]
        rows = rq.shape[0]
        q = rq * jax.nn.sigmoid(rq)
        fg = lb + (1.0 - lb) * jax.nn.sigmoid(rf)
        logf = jnp.log(fg)
        kk = 1.0 - fg
        h1, h2, h3 = _split3(logf)
        b = (jnp.dot(tri, h1, preferred_element_type=F32)
             + jnp.dot(tri, h2, preferred_element_type=F32)
             + jnp.dot(tri, h3, preferred_element_type=F32))
        q_in = (q * jnp.exp(b)).astype(BF16)
        k_in = (kk * jnp.exp(-b)).astype(BF16)
        v_bf = ri.astype(BF16)
        for c in range(rows // chunk):
            rs = slice(c * chunk, (c + 1) * chunk)
            b_last = b[(c + 1) * chunk - 1:(c + 1) * chunk, :]
            k_st = (kk[rs, :] * jnp.exp(b_last - b[rs, :])).astype(BF16)
            decay = jnp.exp(b_last)
            for h in range(nheads):
                cs = slice(h * dk, (h + 1) * dk)
                qc, kc, vc = q_in[rs, cs], k_in[rs, cs], v_bf[rs, cs]
                att = lax.dot_general(qc, kc, _NT, preferred_element_type=F32)
                att = jnp.where(causal, att, 0.0).astype(BF16)
                st = st_ref[sb + h]
                o_scr[bi, rs, cs] = (
                    jnp.dot(att, vc, preferred_element_type=F32)
                    + lax.dot_general(qc, st.astype(BF16), _NT, preferred_element_type=F32))
                st_ref[sb + h] = st * decay[:, cs] + lax.dot_general(
                    vc, k_st[:, cs], _TA, preferred_element_type=F32)
        gate = rg * jax.nn.sigmoid(rg)
        for h in range(nheads):
            cs = slice(h * dk, (h + 1) * dk)
            o = o_scr[bi, :, cs]
            ms = jnp.mean(o * o, axis=-1, keepdims=True)
            o = (o * lax.rsqrt(ms + EPS) * gn_ref[...]) * gate[:, cs] * beta_ref[:, cs]
            if batched:
                rec_ref[bi, :, cs] = o.astype(rec_ref.dtype)
                sout_ref[bi, h] = st_ref[sb + h].T
            else:
                rec_ref[:, cs] = o.astype(rec_ref.dtype)


def _tri(rows, chunk):
    r = np.arange(rows)
    return jnp.asarray((r[:, None] >= r[None, :]) & (r[:, None] // chunk == r[None, :] // chunk), BF16)


def _hgrn_attn_body(pt_ref, *refs, n_rec_in, n_att_in, n_srec_in, rec_steps, att_steps,
                    rec_kw, att_kw, srec_kw):
    n_in = n_rec_in + n_att_in + n_srec_in
    rec_in, att_in = refs[:n_rec_in], refs[n_rec_in:n_rec_in + n_att_in]
    srec_rows, (srec_tri, srec_s0) = refs[n_in - n_srec_in:n_in - 2], refs[n_in - 2:n_in]
    rec_ref, sout_ref, att_ref, srec_ref, ssout_ref = refs[n_in:n_in + 5]
    st_ref, o_scr, ks_ref, vs_ref, kn_scr, sst_ref, so_scr = refs[n_in + 5:]
    nsteps = max(rec_steps, att_steps)
    step = pl.program_id(0)

    @pl.when(step == 0)
    def _():
        st_ref[...] = jnp.zeros(st_ref.shape, F32)

    def run_rec():
        _hgrn_body(*rec_in, rec_ref, st_ref, o_scr, **rec_kw)

    def run_att():
        _attn_decode_body(pt_ref, *att_in, att_ref, ks_ref, vs_ref, kn_scr, **att_kw)
        _hgrn_body(*srec_rows, *rec_in[4:7], srec_tri, srec_s0, srec_ref, ssout_ref, sst_ref, so_scr,
                   **srec_kw)

    if rec_steps < nsteps:
        pl.when(step < rec_steps)(run_rec)
    else:
        run_rec()
    if att_steps < nsteps:
        pl.when(step < att_steps)(run_att)
    else:
        run_att()

    @pl.when(step == rec_steps - 1)
    def _():
        for h in range(sout_ref.shape[0]):
            sout_ref[h] = st_ref[h].T


def _hgrn_attn_fused(proj, lb, rec_gain, rec_beta, page_table, lam, att_gain, att_beta, bmask,
                     proj3, cache_kt, cache_v2, s0, *, nrec, dk, dv, nheads, out_scale):
    t = proj.shape[0]
    gw = nrec * dk
    chunk = math.gcd(t, REC_CHUNK)
    nb, ts, _ = proj3.shape
    schunk = math.gcd(ts, REC_CHUNK)
    npages = page_table.shape[1]
    w, page = cache_kt.shape[1], cache_kt.shape[2]
    nrow = bmask.shape[0]
    tail = LANES
    assert dk == dv == LANES and schunk == ts
    assert 8 % ts == 0 and w == nheads * LANES and cache_v2.shape[1:] == (page * nheads, LANES)
    rows = chunk * _pick(t // chunk, max(1, min(4, (t // chunk) // nb)))
    rec_steps, att_steps = t // rows, nb
    nsteps = max(rec_steps, att_steps)

    def rstep(i):
        return jnp.minimum(i, rec_steps - 1) if rec_steps < nsteps else i

    def bstep(i):
        return jnp.minimum(i, att_steps - 1) if att_steps < nsteps else i

    def k_spec(p):
        return pl.BlockSpec((None, w, page), lambda i, pt: (pt[bstep(i), p], 0, 0))

    def v_spec(p):
        return pl.BlockSpec((None, page * nheads, LANES), lambda i, pt: (pt[bstep(i), p], 0, 0))

    rec_specs = [pl.BlockSpec((rows, gw), functools.partial(lambda i, pt, g: (rstep(i), g), g=3 + g))
                 for g in range(4)] + [
        pl.BlockSpec((1, gw), lambda i, pt: (0, 0)),
        pl.BlockSpec((1, dv), lambda i, pt: (0, 0)),
        pl.BlockSpec((1, gw), lambda i, pt: (0, 0)),
        pl.BlockSpec((rows, rows), lambda i, pt: (0, 0)),
    ]
    att_specs = [
        pl.BlockSpec((1, LANES), lambda i, pt: (0, 0)),
        pl.BlockSpec((1, LANES), lambda i, pt: (0, 0)),
        pl.BlockSpec((1, w), lambda i, pt: (0, 0)),
        pl.BlockSpec((nrow, w), lambda i, pt: (0, 0)),
        pl.BlockSpec((1, ts, w), lambda i, pt: (bstep(i), 0, 0)),
        pl.BlockSpec((1, ts, w), lambda i, pt: (bstep(i), 0, 1)),
        pl.BlockSpec((1, ts, w), lambda i, pt: (bstep(i), 0, 2)),
    ] + [k_spec(p) for p in range(npages)] + [v_spec(p) for p in range(npages)]
    srec_specs = [pl.BlockSpec((1, ts, gw), functools.partial(lambda i, pt, g: (bstep(i), 0, g), g=3 + g))
                  for g in range(4)] + [
        pl.BlockSpec((ts, ts), lambda i, pt: (0, 0)),
        pl.BlockSpec((1, nrec, dk, dv), lambda i, pt: (bstep(i), 0, 0, 0)),
    ]
    grid_spec = pltpu.PrefetchScalarGridSpec(
        num_scalar_prefetch=1,
        grid=(nsteps,),
        in_specs=rec_specs + att_specs + srec_specs,
        out_specs=(pl.BlockSpec((rows, gw), lambda i, pt: (rstep(i), 0)),
                   pl.BlockSpec((nrec, dk, dv), lambda i, pt: (0, 0, 0)),
                   pl.BlockSpec((1, ts, w), lambda i, pt: (bstep(i), 0, 0)),
                   pl.BlockSpec((1, ts, gw), lambda i, pt: (bstep(i), 0, 0)),
                   pl.BlockSpec((1, nrec, dk, dv), lambda i, pt: (bstep(i), 0, 0, 0))),
        scratch_shapes=[
            pltpu.VMEM((nrec, dv, dk), F32),
            pltpu.VMEM((1, rows, gw), F32),
            pltpu.VMEM((w, npages * page), BF16),
            pltpu.VMEM((npages * page + tail, w), BF16),
            pltpu.VMEM((tail, w), BF16),
            pltpu.VMEM((nrec, dv, dk), F32),
            pltpu.VMEM((1, ts, gw), F32),
        ],
    )
    return pl.pallas_call(
        functools.partial(
            _hgrn_attn_body, n_rec_in=len(rec_specs), n_att_in=len(att_specs),
            n_srec_in=len(srec_specs), rec_steps=rec_steps, att_steps=att_steps,
            rec_kw=dict(nheads=nrec, dk=dk, chunk=chunk, batched=False),
            att_kw=dict(npages=npages, page=page, nheads=nheads, ts=ts, out_scale=out_scale),
            srec_kw=dict(nheads=nrec, dk=dk, chunk=schunk, batched=True)),
        grid_spec=grid_spec,
        out_shape=(jax.ShapeDtypeStruct((t, gw), BF16),
                   jax.ShapeDtypeStruct((nrec, dk, dv), F32),
                   jax.ShapeDtypeStruct((nb, ts, w), BF16),
                   jax.ShapeDtypeStruct((nb, ts, gw), BF16),
                   jax.ShapeDtypeStruct((nb, nrec, dk, dv), F32)),
        compiler_params=_params("arbitrary"),
        name="hgrn_attn_fused",
    )(page_table, proj, proj, proj, proj, lb, rec_gain, rec_beta, _tri(rows, chunk),
      lam, att_gain, att_beta, bmask, proj3, proj3, proj3,
      *([cache_kt] * npages), *([cache_v2] * npages),
      proj3, proj3, proj3, proj3, _tri(ts, schunk), s0)


def _finish_body(x_ref, att_ref, rec_ref, wo_ref, g_ref, wu_ref, wd_ref, o_ref, hn_ref):
    j = pl.program_id(1)

    @pl.when(j == 0)
    def _():
        aw = att_ref.shape[-1]
        h = (_rows2d(x_ref[...])
             + jnp.dot(_rows2d(att_ref[...]), wo_ref[0:aw, :], preferred_element_type=F32)
             + jnp.dot(_rows2d(rec_ref[...]), wo_ref[aw:, :], preferred_element_type=F32))
        ms = jnp.mean(h * h, axis=-1, keepdims=True)
        hn_ref[...] = (h * lax.rsqrt(ms + EPS) * g_ref[...]).astype(BF16)
        o_ref[...] = h.reshape(o_ref.shape)

    u = jnp.maximum(jnp.dot(hn_ref[...], wu_ref[...], preferred_element_type=F32), 0.0)
    o_ref[...] += jnp.dot((u * u).astype(BF16), wd_ref[...],
                          preferred_element_type=F32).reshape(o_ref.shape)


def _finish(x, att, rec, wo_bf, ffn_g, wu_bf, wd_bf):
    d, aw, rw = x.shape[-1], att.shape[-1], rec.shape[-1]
    m = math.prod(x.shape[:-1])
    ff = wu_bf.shape[1]
    tm = _pick(m, 512)
    fc = _pick(ff, 1024)
    if x.ndim == 3:
        ts = x.shape[1]
        assert tm % ts == 0 and att.shape[:2] == rec.shape[:2] == x.shape[:2]
        row_spec = lambda n: pl.BlockSpec((tm // ts, ts, n), lambda i, j: (i, 0, 0))
    else:
        row_spec = lambda n: pl.BlockSpec((tm, n), lambda i, j: (i, 0))
    return pl.pallas_call(
        _finish_body,
        grid=(m // tm, ff // fc),
        in_specs=[
            row_spec(d),
            row_spec(aw),
            row_spec(rw),
            pl.BlockSpec((aw + rw, d), lambda i, j: (0, 0), pipeline_mode=pl.Buffered(1)),
            pl.BlockSpec((1, d), lambda i, j: (0, 0)),
            pl.BlockSpec((d, fc), lambda i, j: (0, j)),
            pl.BlockSpec((fc, d), lambda i, j: (j, 0)),
        ],
        out_specs=row_spec(d),
        out_shape=jax.ShapeDtypeStruct(x.shape, F32),
        scratch_shapes=[pltpu.VMEM((tm, d), BF16)],
        compiler_params=_params("parallel", "arbitrary"),
        name="finish",
    )(x, att, rec, wo_bf, ffn_g, wu_bf, wd_bf)


def _rope_tables(pos, head_dim):
    rot = head_dim // 4
    half = rot // 2
    inv = ROPE_THETA ** (-np.arange(half, dtype=np.float64) * 2.0 / rot)
    ang = np.asarray(pos, np.float64)[:, None] * inv[None, :]
    cos, sin = np.cos(ang), np.sin(ang)
    ones = np.ones((ang.shape[0], head_dim - rot))
    zeros = np.zeros((ang.shape[0], head_dim - rot))
    zh = np.zeros_like(sin)
    c = np.concatenate([cos, cos, ones], axis=1)
    sa = np.concatenate([-sin, zh, zeros], axis=1)
    sb = np.concatenate([zh, sin, zeros], axis=1)
    reps = LANES // head_dim
    return tuple(np.tile(a, (1, reps)).astype(np.float32) for a in (c, sa, sb))


def kernel(x_prompt, x_sample, cache_k, cache_v, state_rec, page_table, attn_norm, w_in, q_norm, k_norm,
           lambda_q1, lambda_k1, lambda_q2, lambda_k2, att_out_norm, rec_lb_logits, rec_out_norm,
           beta_att, beta_rec, w_out, ffn_norm, w_up, w_down):
    bp, tp, d = x_prompt.shape
    bs, ts, _ = x_sample.shape
    depth = w_in.shape[0]
    n_pool, page, nheads, _, head_dim = cache_k.shape[1:]
    vd = cache_v.shape[-1]
    nrec, dk, dv = state_rec.shape[2:]
    npages = page_table.shape[1]
    past = npages * page
    aw = nheads * vd
    gw = nrec * dk
    assert bp == 1 and 2 * head_dim == vd == LANES and aw == gw and w_in.shape[2] == 7 * gw

    lam_inits = tuple(0.8 - 0.6 * math.exp(-0.3 * l) for l in range(depth))
    lam_all, lb_all = _prep(lambda_q1, lambda_k1, lambda_q2, lambda_k2, rec_lb_logits, lam_inits)

    lane_map = np.arange(MXU_TILE) // head_dim
    gmat = jnp.asarray(lane_map[:, None] == lane_map[None, :], BF16)
    rope_p = _rope_tables(np.arange(tp), head_dim)
    rope_s = tuple(np.tile(a, (bs, 1)) for a in _rope_tables(past + np.arange(ts), head_dim))
    srow, kcol = np.arange(2 * nheads * 8)[:, None], np.arange(aw)[None, :]
    bmask = ((srow // (nheads * 8) == (kcol // head_dim) % 2)
             & ((srow // 8) % nheads == kcol // vd)).astype(np.float32)

    hp = x_prompt.reshape(tp, d)
    hs = x_sample
    outs = [[] for _ in range(6)]
    for l in range(depth):
        qk_gain = jnp.stack([jnp.tile(q_norm[l], aw // head_dim),
                             jnp.tile(k_norm[l], aw // head_dim)]).reshape(2, 1, aw)
        norm_g = attn_norm[l].reshape(1, d)
        lam = lam_all[l:l + 1]
        lb = lb_all[l:l + 1]
        att_g = att_out_norm[l].reshape(1, vd)
        rec_g = rec_out_norm[l].reshape(1, dv)
        b_att = beta_att[l].reshape(1, aw)
        b_rec = beta_rec[l].reshape(1, gw)
        ffn_g = ffn_norm[l].reshape(1, d)
        out_scale = 1.0 - lam_inits[l]

        proj3, w_in_bf = _proj(hs, norm_g, w_in[l], qk_gain, gmat, *rope_s, head_dim=head_dim)
        proj = _proj(hp, norm_g, w_in_bf, qk_gain, gmat, *rope_p, head_dim=head_dim)
        cache_kt = jnp.transpose(cache_k[l], (0, 2, 3, 4, 1)).reshape(n_pool, aw, page)
        cache_v2 = cache_v[l].reshape(n_pool, page * nheads, vd)
        rec, s_new, att_s, rec_s, s_new_s = _hgrn_attn_fused(
            proj, lb, rec_g, b_rec, page_table, lam, att_g, b_att, bmask, proj3, cache_kt, cache_v2,
            state_rec[l], nrec=nrec, dk=dk, dv=dv, nheads=nheads, out_scale=out_scale)

        att, k_t, v_p, w_out_bf, w_up_bf, w_down_bf = _attn_prompt(
            proj, lam, att_g, b_att, (w_out[l], w_up[l], w_down[l]), nheads=nheads, out_scale=out_scale)
        hp = _finish(hp, att, rec, w_out_bf, ffn_g, w_up_bf, w_down_bf)
        outs[0].append(jnp.transpose(k_t.reshape(nheads, 2, head_dim, tp), (3, 0, 1, 2))
                       .reshape(bp, tp, nheads, 2, head_dim))
        outs[1].append(v_p.reshape(bp, tp, nheads, vd))
        outs[2].append(s_new.reshape(bp, nrec, dk, dv).astype(state_rec.dtype))

        hs = _finish(hs, att_s, rec_s, w_out_bf, ffn_g, w_up_bf, w_down_bf)
        outs[3].append(proj3[:, :, aw:2 * aw].reshape(bs, ts, nheads, 2, head_dim))
        outs[4].append(proj3[:, :, 2 * aw:3 * aw].reshape(bs, ts, nheads, vd))
        outs[5].append(s_new_s.astype(state_rec.dtype))

    return (hp.reshape(bp, tp, d), hs,
            jnp.stack(outs[0]), jnp.stack(outs[1]), jnp.stack(outs[2]),
            jnp.stack(outs[3]), jnp.stack(outs[4]), jnp.stack(outs[5]))
```

```python
import functools
import math

import jax
import jax.numpy as jnp
import numpy as np
from jax import lax
from jax.experimental import pallas as pl
from jax.experimental.pallas import tpu as pltpu

F32 = jnp.float32
BF16 = jnp.bfloat16

EPS = 1e-6
ROPE_THETA = 500000.0
MASK_VALUE = -1e30
LANES = 128
MXU_TILE = 256
LOG2E = 1.4426950408889634
MAX_FIXED_OFFSET = 60.0
SCORE_BOUND_SLACK = 1.0 + 2.0 ** -6
VMEM_LIMIT_BYTES = 56 * 1024 * 1024
REC_CHUNK = 64

_NT = (((1,), (1,)), ((), ()))
_TA = (((0,), (0,)), ((), ()))


def _params(*semantics):
    return pltpu.CompilerParams(dimension_semantics=semantics, vmem_limit_bytes=VMEM_LIMIT_BYTES)


def _pick(n, pref):
    if n <= pref:
        return n
    t = pref
    while n % t:
        t //= 2
    return t


def _split3(x):
    h1 = x.astype(BF16)
    r1 = x - h1.astype(F32)
    h2 = r1.astype(BF16)
    h3 = (r1 - h2.astype(F32)).astype(BF16)
    return h1, h2, h3


def _prep_body(lq1_ref, lk1_ref, lq2_ref, lk2_ref, logit_ref, lam_ref, lb_ref, *, lam_inits):
    for l, lam_init in enumerate(lam_inits):
        s1 = jnp.sum(lq1_ref[l:l + 1, :] * lk1_ref[l:l + 1, :], axis=1, keepdims=True)
        s2 = jnp.sum(lq2_ref[l:l + 1, :] * lk2_ref[l:l + 1, :], axis=1, keepdims=True)
        lam = jnp.exp(s1) - jnp.exp(s2) + lam_init
        lam_ref[l:l + 1, :] = jnp.broadcast_to(lam, (1, LANES))
    x = logit_ref[...]
    e = jnp.exp(x - jnp.max(x, axis=0, keepdims=True))
    sm = e / jnp.sum(e, axis=0, keepdims=True)
    acc = jnp.zeros((1, x.shape[1]), F32)
    for r in range(x.shape[0]):
        acc = acc + sm[r:r + 1, :]
        lb_ref[r:r + 1, :] = acc


def _prep(lq1, lk1, lq2, lk2, logits, lam_inits):
    depth = lq1.shape[0]
    return pl.pallas_call(
        functools.partial(_prep_body, lam_inits=lam_inits),
        out_shape=(jax.ShapeDtypeStruct((depth, LANES), F32),
                   jax.ShapeDtypeStruct(logits.shape, F32)),
        name="prep",
    )(lq1, lk1, lq2, lk2, logits)


def _rows2d(x):
    return x.reshape(-1, x.shape[-1]) if x.ndim == 3 else x


def _store_rows(ref, cols, val):
    if len(ref.shape) == 3:
        ref[:, :, cols] = val.reshape(ref.shape[0], ref.shape[1], val.shape[-1])
    else:
        ref[:, cols] = val


def _proj_body(x_ref, g_ref, w_ref, qkg_ref, gmat_ref, cos_ref, sa_ref, sb_ref, o_ref, *rest,
               head_dim, rot_half):
    j = pl.program_id(1)
    w = w_ref[...]
    if len(rest) == 2:
        wbf_ref, xn_ref = rest
        w = w.astype(BF16)
        wbf_ref[...] = w
    else:
        xn_ref, = rest

    @pl.when(j == 0)
    def _():
        x = _rows2d(x_ref[...])
        ms = jnp.mean(x * x, axis=-1, keepdims=True)
        xn_ref[...] = (x * lax.rsqrt(ms + EPS) * g_ref[...]).astype(BF16)

    @pl.when(j < 2)
    def _():
        y = jnp.dot(xn_ref[...], w, preferred_element_type=F32)
        gm = gmat_ref[...]
        gt = gm.shape[0]
        gain = qkg_ref[0]
        c, sa, sb = cos_ref[...], sa_ref[...], sb_ref[...]
        for t in range(y.shape[1] // gt):
            yt = y[:, t * gt:(t + 1) * gt]
            y2 = yt * yt
            hi = y2.astype(BF16)
            lo = (y2 - hi.astype(F32)).astype(BF16)
            ss = (jnp.dot(hi, gm, preferred_element_type=F32)
                  + jnp.dot(lo, gm, preferred_element_type=F32))
            yn = yt * lax.rsqrt(ss * (1.0 / head_dim) + EPS) * gain[:, t * gt:(t + 1) * gt]
            for u in range(gt // LANES):
                yu = yn[:, u * LANES:(u + 1) * LANES]
                lane0 = t * gt + u * LANES
                _store_rows(o_ref, slice(lane0, lane0 + LANES),
                            yu * c + pltpu.roll(yu, LANES - rot_half, 1) * sa
                            + pltpu.roll(yu, rot_half, 1) * sb)

    @pl.when(j >= 2)
    def _():
        _store_rows(o_ref, slice(None), jnp.dot(xn_ref[...], w, preferred_element_type=F32))


def _proj(x, norm_g, w, qk_gain, gmat, cos_t, sa_t, sb_t, *, head_dim):
    d = x.shape[-1]
    m = math.prod(x.shape[:-1])
    gw = qk_gain.shape[2]
    gt = gmat.shape[0]
    ngroups = w.shape[1] // gw
    tm = _pick(m, 1024)
    emit_w = w.dtype == F32
    assert gw % gt == 0 and gt % LANES == 0 and (m == tm or not emit_w)
    if x.ndim == 3:
        ts = x.shape[1]
        assert tm % ts == 0
        x_spec = pl.BlockSpec((tm // ts, ts, d), lambda i, j: (i, 0, 0))
        o_spec = pl.BlockSpec((tm // ts, ts, gw), lambda i, j: (i, 0, j))
        o_shape = (x.shape[0], ts, ngroups * gw)
    else:
        x_spec = pl.BlockSpec((tm, d), lambda i, j: (i, 0))
        o_spec = pl.BlockSpec((tm, gw), lambda i, j: (i, j))
        o_shape = (m, ngroups * gw)
    return pl.pallas_call(
        functools.partial(_proj_body, head_dim=head_dim, rot_half=head_dim // 8),
        grid=(m // tm, ngroups),
        in_specs=[
            x_spec,
            pl.BlockSpec((1, d), lambda i, j: (0, 0)),
            pl.BlockSpec((d, gw), lambda i, j: (0, j)),
            pl.BlockSpec((1, 1, gw), lambda i, j: (jnp.minimum(j, 1), 0, 0)),
            pl.BlockSpec((gt, gt), lambda i, j: (0, 0)),
            pl.BlockSpec((tm, LANES), lambda i, j: (i, 0)),
            pl.BlockSpec((tm, LANES), lambda i, j: (i, 0)),
            pl.BlockSpec((tm, LANES), lambda i, j: (i, 0)),
        ],
        out_specs=(o_spec, pl.BlockSpec((d, gw), lambda i, j: (0, j))) if emit_w else o_spec,
        out_shape=((jax.ShapeDtypeStruct(o_shape, F32), jax.ShapeDtypeStruct(w.shape, BF16))
                   if emit_w else jax.ShapeDtypeStruct(o_shape, F32)),
        scratch_shapes=[pltpu.VMEM((tm, d), BF16)],
        compiler_params=_params("parallel", "arbitrary"),
        name="proj",
    )(x, norm_g, w, qk_gain, gmat, cos_t, sa_t, sb_t)


def _attn_body(lam_ref, q_ref, k_ref, v_ref, g_ref, beta_ref, *rest, tq, wide, out_scale, ncast):
    cast_in = rest[:ncast]
    o_ref, kt_ref, vo_ref = rest[ncast:ncast + 3]
    cast_out = rest[ncast + 3:2 * ncast + 3]
    kb_ref, vb_ref, qs_ref, m_ref, l_ref, acc_ref, kmax_ref = rest[2 * ncast + 3:]
    qi = pl.program_id(1)
    half = LANES // 2
    lane = lax.broadcasted_iota(jnp.int32, (1, LANES), 1)

    for src, dst in zip(cast_in, cast_out):
        dst[...] = src[...].astype(BF16)

    @pl.when(qi == 0)
    def _():
        kb = k_ref[...].astype(BF16)
        kb_ref[...] = kb
        vb_ref[...] = v_ref[...].astype(BF16)
        vo_ref[...] = v_ref[...]
        for i in range(k_ref.shape[0] // tq):
            kt_ref[:, i * tq:(i + 1) * tq] = k_ref[i * tq:(i + 1) * tq, :].T
        k2 = kb.astype(F32)
        k2 = k2 * k2
        ri = lax.broadcasted_iota(jnp.int32, (LANES, LANES), 0)
        ci = lax.broadcasted_iota(jnp.int32, (LANES, LANES), 1)
        ones_blk = jnp.where((ri >= half) == (ci >= half), 1.0, 0.0).astype(BF16)
        n2 = jnp.max(jnp.dot(k2.astype(BF16), ones_blk, preferred_element_type=F32), axis=0, keepdims=True)
        for c, sel in enumerate((lane < half, lane >= half)):
            kmax_ref[c:c + 1, :] = jnp.broadcast_to(
                jnp.sqrt(jnp.max(jnp.where(sel, n2, 0.0), axis=1, keepdims=True)), (1, LANES))

    q = q_ref[...] * (LOG2E / math.sqrt(half))
    qs = (jnp.where(lane < half, q, 0.0).astype(BF16), jnp.where(lane >= half, q, 0.0).astype(BF16))
    bounds = []
    for c in range(2):
        qs_ref[c * tq:(c + 1) * tq, :] = qs[c]
        qf = qs[c].astype(F32)
        qn2 = jnp.sum(qf * qf, axis=1, keepdims=True)
        qn = qn2 * lax.rsqrt(jnp.maximum(qn2, 1e-30))
        bounds.append(qn * kmax_ref[c:c + 1, :] * SCORE_BOUND_SLACK)
    fast = jnp.max(jnp.maximum(bounds[0], bounds[1])) <= MAX_FIXED_OFFSET
    for c in range(2):
        m_ref[c * tq:(c + 1) * tq, :] = jnp.where(fast, bounds[c], MASK_VALUE)
    l_ref[...] = jnp.zeros(l_ref.shape, F32)
    acc_ref[...] = jnp.zeros(acc_ref.shape, F32)

    def block(off, width, diag_at, fixed_offset):
        kblk = kb_ref[pl.ds(off, width), :]
        vblk = vb_ref[pl.ds(off, width), :]
        for c in range(2):
            rs = slice(c * tq, (c + 1) * tq)
            s = lax.dot_general(qs_ref[rs, :], kblk, _NT, preferred_element_type=F32)
            if diag_at is not None:
                row = lax.broadcasted_iota(jnp.int32, (tq, tq), 0)
                col = lax.broadcasted_iota(jnp.int32, (tq, tq), 1)
                tail = jnp.where(col <= row, s[:, diag_at:], MASK_VALUE)
                s = tail if diag_at == 0 else jnp.concatenate([s[:, :diag_at], tail], axis=1)
            m_prev = m_ref[rs, :]
            if fixed_offset:
                p = jnp.exp2(s - jnp.tile(m_prev, (1, width // LANES)))
                l_ref[rs, :] = l_ref[rs, :] + jnp.sum(p, axis=1, keepdims=True)
                acc_ref[rs, :] = acc_ref[rs, :] + jnp.dot(
                    p.astype(BF16), vblk, preferred_element_type=F32)
            else:
                m_next = jnp.maximum(m_prev, jnp.max(s, axis=1, keepdims=True))
                alpha = jnp.exp2(m_prev - m_next)
                p = jnp.exp2(s - jnp.tile(m_next, (1, width // LANES)))
                l_ref[rs, :] = alpha * l_ref[rs, :] + jnp.sum(p, axis=1, keepdims=True)
                acc_ref[rs, :] = alpha * acc_ref[rs, :] + jnp.dot(
                    p.astype(BF16), vblk, preferred_element_type=F32)
                m_ref[rs, :] = m_next

    per_wide = wide // tq
    nwide = qi // per_wide
    rem = qi - nwide * per_wide

    def all_blocks(fixed_offset):
        def wide_body(j, carry):
            block(pl.multiple_of(j * wide, wide), wide, None, fixed_offset)
            return carry

        lax.fori_loop(0, nwide, wide_body, 0)
        for r in range(per_wide):
            @pl.when(rem == r)
            def _(r=r):
                block(pl.multiple_of(nwide * wide, wide), (r + 1) * tq, r * tq, fixed_offset)

    pl.when(fast)(functools.partial(all_blocks, True))
    pl.when(jnp.logical_not(fast))(functools.partial(all_blocks, False))

    a = acc_ref[...] / l_ref[...]
    o = a[0:tq, :] - lam_ref[...] * a[tq:2 * tq, :]
    ms = jnp.mean(o * o, axis=-1, keepdims=True)
    o_ref[...] = (o * lax.rsqrt(ms + EPS) * g_ref[...] * out_scale * beta_ref[...]).astype(o_ref.dtype)


def _attn_prompt(proj, lam, gain, beta, weights, *, nheads, out_scale):
    t = proj.shape[0]
    tq = _pick(t, 512)
    wide = _pick(t, 8 * tq)
    nq = t // tq
    nsteps = nheads * nq
    assert tq % LANES == 0 and wide % tq == 0
    assert all(w.shape[0] % (16 * nsteps) == 0 for w in weights)
    cast_specs = [pl.BlockSpec((w.shape[0] // nsteps, w.shape[1]), lambda h, i: (h * nq + i, 0))
                  for w in weights]
    return pl.pallas_call(
        functools.partial(_attn_body, tq=tq, wide=wide, out_scale=out_scale, ncast=len(weights)),
        grid=(nheads, nq),
        in_specs=[
            pl.BlockSpec((1, LANES), lambda h, i: (0, 0)),
            pl.BlockSpec((tq, LANES), lambda h, i: (i, h)),
            pl.BlockSpec((t, LANES), lambda h, i: (0, nheads + h)),
            pl.BlockSpec((t, LANES), lambda h, i: (0, 2 * nheads + h)),
            pl.BlockSpec((1, LANES), lambda h, i: (0, 0)),
            pl.BlockSpec((1, LANES), lambda h, i: (0, h)),
        ] + cast_specs,
        out_specs=(pl.BlockSpec((tq, LANES), lambda h, i: (i, h)),
                   pl.BlockSpec((LANES, t), lambda h, i: (h, 0)),
                   pl.BlockSpec((t, LANES), lambda h, i: (0, h))) + tuple(cast_specs),
        out_shape=(jax.ShapeDtypeStruct((t, nheads * LANES), BF16),
                   jax.ShapeDtypeStruct((nheads * LANES, t), F32),
                   jax.ShapeDtypeStruct((t, nheads * LANES), F32))
        + tuple(jax.ShapeDtypeStruct(w.shape, BF16) for w in weights),
        scratch_shapes=[
            pltpu.VMEM((t, LANES), BF16),
            pltpu.VMEM((t, LANES), BF16),
            pltpu.VMEM((2 * tq, LANES), BF16),
            pltpu.VMEM((2 * tq, LANES), F32),
            pltpu.VMEM((2 * tq, LANES), F32),
            pltpu.VMEM((2 * tq, LANES), F32),
            pltpu.VMEM((2, LANES), F32),
        ],
        compiler_params=_params("parallel", "arbitrary"),
        name="attn_prompt",
    )(lam, proj, proj, proj, gain, beta, *weights)


def _attn_decode_body(pt_ref, lam_ref, g_ref, beta_ref, bmask_ref, q_ref, kn_ref, vn_ref, *rest,
                      npages, page, nheads, ts, out_scale):
    del pt_ref
    kp_refs = rest[:npages]
    vp_refs = rest[npages:2 * npages]
    o_ref, ks_ref, vs_ref, kn_scr = rest[2 * npages:]
    past = npages * page
    tail, w = kn_scr.shape
    nrow = bmask_ref.shape[0]
    half = LANES // 2

    for p in range(npages):
        ks_ref[:, p * page:(p + 1) * page] = kp_refs[p][...].astype(BF16)
        for h in range(nheads):
            vs_ref[p * page:(p + 1) * page, h * LANES:(h + 1) * LANES] = (
                vp_refs[p][pl.ds(h, page, stride=nheads), :].astype(BF16))
    kn_scr[...] = jnp.zeros((tail, w), BF16)
    kn_scr[0:ts, :] = kn_ref[0].astype(BF16)
    vs_ref[past:, :] = jnp.zeros((tail, w), BF16)
    vs_ref[past:past + ts, :] = vn_ref[0].astype(BF16)

    q = q_ref[0] * (LOG2E / math.sqrt(half))
    qt = (jnp.tile(q, (nrow // ts, 1)) * bmask_ref[...]).astype(BF16)
    s_past = jnp.dot(qt, ks_ref[...], preferred_element_type=F32)
    s_new = lax.dot_general(qt, kn_scr[...], _NT, preferred_element_type=F32)
    row = lax.broadcasted_iota(jnp.int32, s_new.shape, 0)
    col = lax.broadcasted_iota(jnp.int32, s_new.shape, 1)
    s_new = jnp.where(col <= lax.rem(row, ts), s_new, MASK_VALUE)
    m = jnp.maximum(jnp.max(s_past, axis=1, keepdims=True), jnp.max(s_new, axis=1, keepdims=True))
    p_past = jnp.exp2(s_past - m)
    p_new = jnp.exp2(s_new - m)
    l = jnp.sum(p_past, axis=1, keepdims=True) + jnp.sum(p_new, axis=1, keepdims=True)
    row1 = lax.broadcasted_iota(jnp.int32, (nrow, 1), 0)
    hr = nrow // 2
    wgt = jnp.where(row1 < hr, 1.0, -lam_ref[:, 0:1]) / l
    pw_past = p_past * wgt
    pw_new = p_new * wgt
    a_past = (pw_past[0:hr, :] + pw_past[hr:, :]).astype(BF16)
    a_new = (pw_new[0:hr, :] + pw_new[hr:, :]).astype(BF16)
    o2 = (jnp.dot(a_past, vs_ref[0:past, :], preferred_element_type=F32)
          + jnp.dot(a_new, vs_ref[past:, :], preferred_element_type=F32))
    for h in range(nheads):
        cs = slice(h * LANES, (h + 1) * LANES)
        o = o2[h * 8:(h + 1) * 8, cs]
        ms = jnp.mean(o * o, axis=-1, keepdims=True)
        o = o * lax.rsqrt(ms + EPS) * g_ref[...] * out_scale * beta_ref[:, cs]
        o_ref[0, :, cs] = o[0:ts, :].astype(o_ref.dtype)


def _hgrn_body(rq_ref, rf_ref, ri_ref, rg_ref, lb_ref, gn_ref, beta_ref, tri_ref, *rest,
               nheads, dk, chunk, batched):
    if batched:
        s0_ref, rec_ref, sout_ref, st_ref, o_scr = rest
        nseq = rq_ref.shape[0]
    else:
        rec_ref, st_ref, o_scr = rest
        nseq = 1

    lb = lb_ref[...]
    tri = tri_ref[...]
    tr = lax.broadcasted_iota(jnp.int32, (chunk, chunk), 0)
    tc = lax.broadcasted_iota(jnp.int32, (chunk, chunk), 1)
    causal = tc <= tr
    for bi in range(nseq):
        sb = bi * nheads
        if batched:
            rq, rf, ri, rg = rq_ref[bi], rf_ref[bi], ri_ref[bi], rg_ref[bi]
            for h in range(nheads):
                st_ref[sb + h] = s0_ref[bi, h].T
        else:
            rq, rf, ri, rg = rq_ref[...], rf_ref[...], ri_ref[...], rg_ref[...]
        rows = rq.shape[0]
        q = rq * jax.nn.sigmoid(rq)
        fg = lb + (1.0 - lb) * jax.nn.sigmoid(rf)
        logf = jnp.log(fg)
        kk = 1.0 - fg
        h1, h2, h3 = _split3(logf)
        b = (jnp.dot(tri, h1, preferred_element_type=F32)
             + jnp.dot(tri, h2, preferred_element_type=F32)
             + jnp.dot(tri, h3, preferred_element_type=F32))
        q_in = (q * jnp.exp(b)).astype(BF16)
        k_in = (kk * jnp.exp(-b)).astype(BF16)
        v_bf = ri.astype(BF16)
        for c in range(rows // chunk):
            rs = slice(c * chunk, (c + 1) * chunk)
            b_last = b[(c + 1) * chunk - 1:(c + 1) * chunk, :]
            k_st = (kk[rs, :] * jnp.exp(b_last - b[rs, :])).astype(BF16)
            decay = jnp.exp(b_last)
            for h in range(nheads):
                cs = slice(h * dk, (h + 1) * dk)
                qc, kc, vc = q_in[rs, cs], k_in[rs, cs], v_bf[rs, cs]
                att = lax.dot_general(qc, kc, _NT, preferred_element_type=F32)
                att = jnp.where(causal, att, 0.0).astype(BF16)
                st = st_ref[sb + h]
                o_scr[bi, rs, cs] = (
                    jnp.dot(att, vc, preferred_element_type=F32)
                    + lax.dot_general(qc, st.astype(BF16), _NT, preferred_element_type=F32))
                st_ref[sb + h] = st * decay[:, cs] + lax.dot_general(
                    vc, k_st[:, cs], _TA, preferred_element_type=F32)
        gate = rg * jax.nn.sigmoid(rg)
        for h in range(nheads):
            cs = slice(h * dk, (h + 1) * dk)
            o = o_scr[bi, :, cs]
            ms = jnp.mean(o * o, axis=-1, keepdims=True)
            o = (o * lax.rsqrt(ms + EPS) * gn_ref[...]) * gate[:, cs] * beta_ref[:, cs]
            if batched:
                rec_ref[bi, :, cs] = o.astype(rec_ref.dtype)
                sout_ref[bi, h] = st_ref[sb + h].T
            else:
                rec_ref[:, cs] = o.astype(rec_ref.dtype)


def _tri(rows, chunk):
    r = np.arange(rows)
    return jnp.asarray((r[:, None] >= r[None, :]) & (r[:, None] // chunk == r[None, :] // chunk), BF16)


def _hgrn_attn_body(pt_ref, *refs, n_rec_in, n_att_in, n_srec_in, rec_steps, att_steps,
                    rec_kw, att_kw, srec_kw):
    n_in = n_rec_in + n_att_in + n_srec_in
    rec_in, att_in = refs[:n_rec_in], refs[n_rec_in:n_rec_in + n_att_in]
    srec_rows, (srec_tri, srec_s0) = refs[n_in - n_srec_in:n_in - 2], refs[n_in - 2:n_in]
    rec_ref, sout_ref, att_ref, srec_ref, ssout_ref = refs[n_in:n_in + 5]
    st_ref, o_scr, ks_ref, vs_ref, kn_scr, sst_ref, so_scr = refs[n_in + 5:]
    nsteps = max(rec_steps, att_steps)
    step = pl.program_id(0)

    @pl.when(step == 0)
    def _():
        st_ref[...] = jnp.zeros(st_ref.shape, F32)

    def run_rec():
        _hgrn_body(*rec_in, rec_ref, st_ref, o_scr, **rec_kw)

    def run_att():
        _attn_decode_body(pt_ref, *att_in, att_ref, ks_ref, vs_ref, kn_scr, **att_kw)
        _hgrn_body(*srec_rows, *rec_in[4:7], srec_tri, srec_s0, srec_ref, ssout_ref, sst_ref, so_scr,
                   **srec_kw)

    if rec_steps < nsteps:
        pl.when(step < rec_steps)(run_rec)
    else:
        run_rec()
    if att_steps < nsteps:
        pl.when(step < att_steps)(run_att)
    else:
        run_att()

    @pl.when(step == rec_steps - 1)
    def _():
        for h in range(sout_ref.shape[0]):
            sout_ref[h] = st_ref[h].T


def _hgrn_attn_fused(proj, lb, rec_gain, rec_beta, page_table, lam, att_gain, att_beta, bmask,
                     proj3, cache_kt, cache_v2, s0, *, nrec, dk, dv, nheads, out_scale):
    t = proj.shape[0]
    gw = nrec * dk
    chunk = math.gcd(t, REC_CHUNK)
    nb, ts, _ = proj3.shape
    schunk = math.gcd(ts, REC_CHUNK)
    npages = page_table.shape[1]
    w, page = cache_kt.shape[1], cache_kt.shape[2]
    nrow = bmask.shape[0]
    tail = LANES
    assert dk == dv == LANES and schunk == ts
    assert 8 % ts == 0 and w == nheads * LANES and cache_v2.shape[1:] == (page * nheads, LANES)
    rows = chunk * _pick(t // chunk, max(1, min(4, (t // chunk) // nb)))
    rec_steps, att_steps = t // rows, nb
    nsteps = max(rec_steps, att_steps)

    def rstep(i):
        return jnp.minimum(i, rec_steps - 1) if rec_steps < nsteps else i

    def bstep(i):
        return jnp.minimum(i, att_steps - 1) if att_steps < nsteps else i

    def k_spec(p):
        return pl.BlockSpec((None, w, page), lambda i, pt: (pt[bstep(i), p], 0, 0))

    def v_spec(p):
        return pl.BlockSpec((None, page * nheads, LANES), lambda i, pt: (pt[bstep(i), p], 0, 0))

    rec_specs = [pl.BlockSpec((rows, gw), functools.partial(lambda i, pt, g: (rstep(i), g), g=3 + g))
                 for g in range(4)] + [
        pl.BlockSpec((1, gw), lambda i, pt: (0, 0)),
        pl.BlockSpec((1, dv), lambda i, pt: (0, 0)),
        pl.BlockSpec((1, gw), lambda i, pt: (0, 0)),
        pl.BlockSpec((rows, rows), lambda i, pt: (0, 0)),
    ]
    att_specs = [
        pl.BlockSpec((1, LANES), lambda i, pt: (0, 0)),
        pl.BlockSpec((1, LANES), lambda i, pt: (0, 0)),
        pl.BlockSpec((1, w), lambda i, pt: (0, 0)),
        pl.BlockSpec((nrow, w), lambda i, pt: (0, 0)),
        pl.BlockSpec((1, ts, w), lambda i, pt: (bstep(i), 0, 0)),
        pl.BlockSpec((1, ts, w), lambda i, pt: (bstep(i), 0, 1)),
        pl.BlockSpec((1, ts, w), lambda i, pt: (bstep(i), 0, 2)),
    ] + [k_spec(p) for p in range(npages)] + [v_spec(p) for p in range(npages)]
    srec_specs = [pl.BlockSpec((1, ts, gw), functools.partial(lambda i, pt, g: (bstep(i), 0, g), g=3 + g))
                  for g in range(4)] + [
        pl.BlockSpec((ts, ts), lambda i, pt: (0, 0)),
        pl.BlockSpec((1, nrec, dk, dv), lambda i, pt: (bstep(i), 0, 0, 0)),
    ]
    grid_spec = pltpu.PrefetchScalarGridSpec(
        num_scalar_prefetch=1,
        grid=(nsteps,),
        in_specs=rec_specs + att_specs + srec_specs,
        out_specs=(pl.BlockSpec((rows, gw), lambda i, pt: (rstep(i), 0)),
                   pl.BlockSpec((nrec, dk, dv), lambda i, pt: (0, 0, 0)),
                   pl.BlockSpec((1, ts, w), lambda i, pt: (bstep(i), 0, 0)),
                   pl.BlockSpec((1, ts, gw), lambda i, pt: (bstep(i), 0, 0)),
                   pl.BlockSpec((1, nrec, dk, dv), lambda i, pt: (bstep(i), 0, 0, 0))),
        scratch_shapes=[
            pltpu.VMEM((nrec, dv, dk), F32),
            pltpu.VMEM((1, rows, gw), F32),
            pltpu.VMEM((w, npages * page), BF16),
            pltpu.VMEM((npages * page + tail, w), BF16),
            pltpu.VMEM((tail, w), BF16),
            pltpu.VMEM((nrec, dv, dk), F32),
            pltpu.VMEM((1, ts, gw), F32),
        ],
    )
    return pl.pallas_call(
        functools.partial(
            _hgrn_attn_body, n_rec_in=len(rec_specs), n_att_in=len(att_specs),
            n_srec_in=len(srec_specs), rec_steps=rec_steps, att_steps=att_steps,
            rec_kw=dict(nheads=nrec, dk=dk, chunk=chunk, batched=False),
            att_kw=dict(npages=npages, page=page, nheads=nheads, ts=ts, out_scale=out_scale),
            srec_kw=dict(nheads=nrec, dk=dk, chunk=schunk, batched=True)),
        grid_spec=grid_spec,
        out_shape=(jax.ShapeDtypeStruct((t, gw), BF16),
                   jax.ShapeDtypeStruct((nrec, dk, dv), F32),
                   jax.ShapeDtypeStruct((nb, ts, w), BF16),
                   jax.ShapeDtypeStruct((nb, ts, gw), BF16),
                   jax.ShapeDtypeStruct((nb, nrec, dk, dv), F32)),
        compiler_params=_params("arbitrary"),
        name="hgrn_attn_fused",
    )(page_table, proj, proj, proj, proj, lb, rec_gain, rec_beta, _tri(rows, chunk),
      lam, att_gain, att_beta, bmask, proj3, proj3, proj3,
      *([cache_kt] * npages), *([cache_v2] * npages),
      proj3, proj3, proj3, proj3, _tri(ts, schunk), s0)


def _finish_body(x_ref, att_ref, rec_ref, wo_ref, g_ref, wu_ref, wd_ref, o_ref, hn_ref):
    j = pl.program_id(1)

    @pl.when(j == 0)
    def _():
        aw = att_ref.shape[-1]
        h = (_rows2d(x_ref[...])
             + jnp.dot(_rows2d(att_ref[...]), wo_ref[0:aw, :], preferred_element_type=F32)
             + jnp.dot(_rows2d(rec_ref[...]), wo_ref[aw:, :], preferred_element_type=F32))
        ms = jnp.mean(h * h, axis=-1, keepdims=True)
        hn_ref[...] = (h * lax.rsqrt(ms + EPS) * g_ref[...]).astype(BF16)
        o_ref[...] = h.reshape(o_ref.shape)

    u = jnp.maximum(jnp.dot(hn_ref[...], wu_ref[...], preferred_element_type=F32), 0.0)
    o_ref[...] += jnp.dot((u * u).astype(BF16), wd_ref[...],
                          preferred_element_type=F32).reshape(o_ref.shape)


def _finish(x, att, rec, wo_bf, ffn_g, wu_bf, wd_bf):
    d, aw, rw = x.shape[-1], att.shape[-1], rec.shape[-1]
    m = math.prod(x.shape[:-1])
    ff = wu_bf.shape[1]
    tm = _pick(m, 512)
    fc = _pick(ff, 1024)
    if x.ndim == 3:
        ts = x.shape[1]
        assert tm % ts == 0 and att.shape[:2] == rec.shape[:2] == x.shape[:2]
        row_spec = lambda n: pl.BlockSpec((tm // ts, ts, n), lambda i, j: (i, 0, 0))
    else:
        row_spec = lambda n: pl.BlockSpec((tm, n), lambda i, j: (i, 0))
    return pl.pallas_call(
        _finish_body,
        grid=(m // tm, ff // fc),
        in_specs=[
            row_spec(d),
            row_spec(aw),
            row_spec(rw),
            pl.BlockSpec((aw + rw, d), lambda i, j: (0, 0), pipeline_mode=pl.Buffered(1)),
            pl.BlockSpec((1, d), lambda i, j: (0, 0)),
            pl.BlockSpec((d, fc), lambda i, j: (0, j)),
            pl.BlockSpec((fc, d), lambda i, j: (j, 0)),
        ],
        out_specs=row_spec(d),
        out_shape=jax.ShapeDtypeStruct(x.shape, F32),
        scratch_shapes=[pltpu.VMEM((tm, d), BF16)],
        compiler_params=_params("parallel", "arbitrary"),
        name="finish",
    )(x, att, rec, wo_bf, ffn_g, wu_bf, wd_bf)


def _rope_tables(pos, head_dim):
    rot = head_dim // 4
    half = rot // 2
    inv = ROPE_THETA ** (-np.arange(half, dtype=np.float64) * 2.0 / rot)
    ang = np.asarray(pos, np.float64)[:, None] * inv[None, :]
    cos, sin = np.cos(ang), np.sin(ang)
    ones = np.ones((ang.shape[0], head_dim - rot))
    zeros = np.zeros((ang.shape[0], head_dim - rot))
    zh = np.zeros_like(sin)
    c = np.concatenate([cos, cos, ones], axis=1)
    sa = np.concatenate([-sin, zh, zeros], axis=1)
    sb = np.concatenate([zh, sin, zeros], axis=1)
    reps = LANES // head_dim
    return tuple(np.tile(a, (1, reps)).astype(np.float32) for a in (c, sa, sb))


def kernel(x_prompt, x_sample, cache_k, cache_v, state_rec, page_table, attn_norm, w_in, q_norm, k_norm,
           lambda_q1, lambda_k1, lambda_q2, lambda_k2, att_out_norm, rec_lb_logits, rec_out_norm,
           beta_att, beta_rec, w_out, ffn_norm, w_up, w_down):
    bp, tp, d = x_prompt.shape
    bs, ts, _ = x_sample.shape
    depth = w_in.shape[0]
    n_pool, page, nheads, _, head_dim = cache_k.shape[1:]
    vd = cache_v.shape[-1]
    nrec, dk, dv = state_rec.shape[2:]
    npages = page_table.shape[1]
    past = npages * page
    aw = nheads * vd
    gw = nrec * dk
    assert bp == 1 and 2 * head_dim == vd == LANES and aw == gw and w_in.shape[2] == 7 * gw

    lam_inits = tuple(0.8 - 0.6 * math.exp(-0.3 * l) for l in range(depth))
    lam_all, lb_all = _prep(lambda_q1, lambda_k1, lambda_q2, lambda_k2, rec_lb_logits, lam_inits)

    lane_map = np.arange(MXU_TILE) // head_dim
    gmat = jnp.asarray(lane_map[:, None] == lane_map[None, :], BF16)
    rope_p = _rope_tables(np.arange(tp), head_dim)
    rope_s = tuple(np.tile(a, (bs, 1)) for a in _rope_tables(past + np.arange(ts), head_dim))
    srow, kcol = np.arange(2 * nheads * 8)[:, None], np.arange(aw)[None, :]
    bmask = ((srow // (nheads * 8) == (kcol // head_dim) % 2)
             & ((srow // 8) % nheads == kcol // vd)).astype(np.float32)

    hp = x_prompt.reshape(tp, d)
    hs = x_sample
    outs = [[] for _ in range(6)]
    for l in range(depth):
        qk_gain = jnp.stack([jnp.tile(q_norm[l], aw // head_dim),
                             jnp.tile(k_norm[l], aw // head_dim)]).reshape(2, 1, aw)
        norm_g = attn_norm[l].reshape(1, d)
        lam = lam_all[l:l + 1]
        lb = lb_all[l:l + 1]
        att_g = att_out_norm[l].reshape(1, vd)
        rec_g = rec_out_norm[l].reshape(1, dv)
        b_att = beta_att[l].reshape(1, aw)
        b_rec = beta_rec[l].reshape(1, gw)
        ffn_g = ffn_norm[l].reshape(1, d)
        out_scale = 1.0 - lam_inits[l]

        proj3, w_in_bf = _proj(hs, norm_g, w_in[l], qk_gain, gmat, *rope_s, head_dim=head_dim)
        proj = _proj(hp, norm_g, w_in_bf, qk_gain, gmat, *rope_p, head_dim=head_dim)
        cache_kt = jnp.transpose(cache_k[l], (0, 2, 3, 4, 1)).reshape(n_pool, aw, page)
        cache_v2 = cache_v[l].reshape(n_pool, page * nheads, vd)
        rec, s_new, att_s, rec_s, s_new_s = _hgrn_attn_fused(
            proj, lb, rec_g, b_rec, page_table, lam, att_g, b_att, bmask, proj3, cache_kt, cache_v2,
            state_rec[l], nrec=nrec, dk=dk, dv=dv, nheads=nheads, out_scale=out_scale)

        att, k_t, v_p, w_out_bf, w_up_bf, w_down_bf = _attn_prompt(
            proj, lam, att_g, b_att, (w_out[l], w_up[l], w_down[l]), nheads=nheads, out_scale=out_scale)
        hp = _finish(hp, att, rec, w_out_bf, ffn_g, w_up_bf, w_down_bf)
        outs[0].append(jnp.transpose(k_t.reshape(nheads, 2, head_dim, tp), (3, 0, 1, 2))
                       .reshape(bp, tp, nheads, 2, head_dim))
        outs[1].append(v_p.reshape(bp, tp, nheads, vd))
        outs[2].append(s_new.reshape(bp, nrec, dk, dv).astype(state_rec.dtype))

        hs = _finish(hs, att_s, rec_s, w_out_bf, ffn_g, w_up_bf, w_down_bf)
        outs[3].append(proj3[:, :, aw:2 * aw].reshape(bs, ts, nheads, 2, head_dim))
        outs[4].append(proj3[:, :, 2 * aw:3 * aw].reshape(bs, ts, nheads, vd))
        outs[5].append(s_new_s.astype(state_rec.dtype))

    return (hp.reshape(bp, tp, d), hs,
            jnp.stack(outs[0]), jnp.stack(outs[1]), jnp.stack(outs[2]),
            jnp.stack(outs[3]), jnp.stack(outs[4]), jnp.stack(outs[5]))
```

```python
import functools
import math

import jax
import jax.numpy as jnp
import numpy as np
from jax import lax
from jax.experimental import pallas as pl
from jax.experimental.pallas import tpu as pltpu

F32 = jnp.float32
BF16 = jnp.bfloat16

EPS = 1e-6
ROPE_THETA = 500000.0
MASK_VALUE = -1e30
LANES = 128
MXU_TILE = 256
LOG2E = 1.4426950408889634
MAX_FIXED_OFFSET = 60.0
SCORE_BOUND_SLACK = 1.0 + 2.0 ** -6
VMEM_LIMIT_BYTES = 56 * 1024 * 1024
REC_CHUNK = 64

_NT = (((1,), (1,)), ((), ()))
_TA = (((0,), (0,)), ((), ()))


def _params(*semantics):
    return pltpu.CompilerParams(dimension_semantics=semantics, vmem_limit_bytes=VMEM_LIMIT_BYTES)


def _pick(n, pref):
    if n <= pref:
        return n
    t = pref
    while n % t:
        t //= 2
    return t


def _split3(x):
    h1 = x.astype(BF16)
    r1 = x - h1.astype(F32)
    h2 = r1.astype(BF16)
    h3 = (r1 - h2.astype(F32)).astype(BF16)
    return h1, h2, h3


def _prep_body(lq1_ref, lk1_ref, lq2_ref, lk2_ref, logit_ref, lam_ref, lb_ref, *, lam_inits):
    for l, lam_init in enumerate(lam_inits):
        s1 = jnp.sum(lq1_ref[l:l + 1, :] * lk1_ref[l:l + 1, :], axis=1, keepdims=True)
        s2 = jnp.sum(lq2_ref[l:l + 1, :] * lk2_ref[l:l + 1, :], axis=1, keepdims=True)
        lam = jnp.exp(s1) - jnp.exp(s2) + lam_init
        lam_ref[l:l + 1, :] = jnp.broadcast_to(lam, (1, LANES))
    x = logit_ref[...]
    e = jnp.exp(x - jnp.max(x, axis=0, keepdims=True))
    sm = e / jnp.sum(e, axis=0, keepdims=True)
    acc = jnp.zeros((1, x.shape[1]), F32)
    for r in range(x.shape[0]):
        acc = acc + sm[r:r + 1, :]
        lb_ref[r:r + 1, :] = acc


def _prep(lq1, lk1, lq2, lk2, logits, lam_inits):
    depth = lq1.shape[0]
    return pl.pallas_call(
        functools.partial(_prep_body, lam_inits=lam_inits),
        out_shape=(jax.ShapeDtypeStruct((depth, LANES), F32),
                   jax.ShapeDtypeStruct(logits.shape, F32)),
        name="prep",
    )(lq1, lk1, lq2, lk2, logits)


def _rows2d(x):
    return x.reshape(-1, x.shape[-1]) if x.ndim == 3 else x


def _store_rows(ref, cols, val):
    if len(ref.shape) == 3:
        ref[:, :, cols] = val.reshape(ref.shape[0], ref.shape[1], val.shape[-1])
    else:
        ref[:, cols] = val


def _proj_body(x_ref, g_ref, w_ref, qkg_ref, gmat_ref, cos_ref, sa_ref, sb_ref, o_ref, *rest,
               head_dim, rot_half):
    j = pl.program_id(1)
    w = w_ref[...]
    if len(rest) == 2:
        wbf_ref, xn_ref = rest
        w = w.astype(BF16)
        wbf_ref[...] = w
    else:
        xn_ref, = rest

    @pl.when(j == 0)
    def _():
        x = _rows2d(x_ref[...])
        ms = jnp.mean(x * x, axis=-1, keepdims=True)
        xn_ref[...] = (x * lax.rsqrt(ms + EPS) * g_ref[...]).astype(BF16)

    @pl.when(j < 2)
    def _():
        y = jnp.dot(xn_ref[...], w, preferred_element_type=F32)
        gm = gmat_ref[...]
        gt = gm.shape[0]
        gain = qkg_ref[0]
        c, sa, sb = cos_ref[...], sa_ref[...], sb_ref[...]
        for t in range(y.shape[1] // gt):
            yt = y[:, t * gt:(t + 1) * gt]
            y2 = yt * yt
            hi = y2.astype(BF16)
            lo = (y2 - hi.astype(F32)).astype(BF16)
            ss = (jnp.dot(hi, gm, preferred_element_type=F32)
                  + jnp.dot(lo, gm, preferred_element_type=F32))
            yn = yt * lax.rsqrt(ss * (1.0 / head_dim) + EPS) * gain[:, t * gt:(t + 1) * gt]
            for u in range(gt // LANES):
                yu = yn[:, u * LANES:(u + 1) * LANES]
                lane0 = t * gt + u * LANES
                _store_rows(o_ref, slice(lane0, lane0 + LANES),
                            yu * c + pltpu.roll(yu, LANES - rot_half, 1) * sa
                            + pltpu.roll(yu, rot_half, 1) * sb)

    @pl.when(j >= 2)
    def _():
        _store_rows(o_ref, slice(None), jnp.dot(xn_ref[...], w, preferred_element_type=F32))


def _proj(x, norm_g, w, qk_gain, gmat, cos_t, sa_t, sb_t, *, head_dim):
    d = x.shape[-1]
    m = math.prod(x.shape[:-1])
    gw = qk_gain.shape[2]
    gt = gmat.shape[0]
    ngroups = w.shape[1] // gw
    tm = _pick(m, 1024)
    emit_w = w.dtype == F32
    assert gw % gt == 0 and gt % LANES == 0 and (m == tm or not emit_w)
    if x.ndim == 3:
        ts = x.shape[1]
        assert tm % ts == 0
        x_spec = pl.BlockSpec((tm // ts, ts, d), lambda i, j: (i, 0, 0))
        o_spec = pl.BlockSpec((tm // ts, ts, gw), lambda i, j: (i, 0, j))
        o_shape = (x.shape[0], ts, ngroups * gw)
    else:
        x_spec = pl.BlockSpec((tm, d), lambda i, j: (i, 0))
        o_spec = pl.BlockSpec((tm, gw), lambda i, j: (i, j))
        o_shape = (m, ngroups * gw)
    return pl.pallas_call(
        functools.partial(_proj_body, head_dim=head_dim, rot_half=head_dim // 8),
        grid=(m // tm, ngroups),
        in_specs=[
            x_spec,
            pl.BlockSpec((1, d), lambda i, j: (0, 0)),
            pl.BlockSpec((d, gw), lambda i, j: (0, j)),
            pl.BlockSpec((1, 1, gw), lambda i, j: (jnp.minimum(j, 1), 0, 0)),
            pl.BlockSpec((gt, gt), lambda i, j: (0, 0)),
            pl.BlockSpec((tm, LANES), lambda i, j: (i, 0)),
            pl.BlockSpec((tm, LANES), lambda i, j: (i, 0)),
            pl.BlockSpec((tm, LANES), lambda i, j: (i, 0)),
        ],
        out_specs=(o_spec, pl.BlockSpec((d, gw), lambda i, j: (0, j))) if emit_w else o_spec,
        out_shape=((jax.ShapeDtypeStruct(o_shape, F32), jax.ShapeDtypeStruct(w.shape, BF16))
                   if emit_w else jax.ShapeDtypeStruct(o_shape, F32)),
        scratch_shapes=[pltpu.VMEM((tm, d), BF16)],
        compiler_params=_params("parallel", "arbitrary"),
        name="proj",
    )(x, norm_g, w, qk_gain, gmat, cos_t, sa_t, sb_t)


def _attn_body(lam_ref, q_ref, k_ref, v_ref, g_ref, beta_ref, *rest, tq, wide, out_scale, ncast):
    cast_in = rest[:ncast]
    o_ref, kt_ref, vo_ref = rest[ncast:ncast + 3]
    cast_out = rest[ncast + 3:2 * ncast + 3]
    kb_ref, vb_ref, qs_ref, m_ref, l_ref, acc_ref, kmax_ref = rest[2 * ncast + 3:]
    qi = pl.program_id(1)
    half = LANES // 2
    lane = lax.broadcasted_iota(jnp.int32, (1, LANES), 1)

    for src, dst in zip(cast_in, cast_out):
        dst[...] = src[...].astype(BF16)

    @pl.when(qi == 0)
    def _():
        kb = k_ref[...].astype(BF16)
        kb_ref[...] = kb
        vb_ref[...] = v_ref[...].astype(BF16)
        vo_ref[...] = v_ref[...]
        for i in range(k_ref.shape[0] // tq):
            kt_ref[:, i * tq:(i + 1) * tq] = k_ref[i * tq:(i + 1) * tq, :].T
        k2 = kb.astype(F32)
        k2 = k2 * k2
        ri = lax.broadcasted_iota(jnp.int32, (LANES, LANES), 0)
        ci = lax.broadcasted_iota(jnp.int32, (LANES, LANES), 1)
        ones_blk = jnp.where((ri >= half) == (ci >= half), 1.0, 0.0).astype(BF16)
        n2 = jnp.max(jnp.dot(k2.astype(BF16), ones_blk, preferred_element_type=F32), axis=0, keepdims=True)
        for c, sel in enumerate((lane < half, lane >= half)):
            kmax_ref[c:c + 1, :] = jnp.broadcast_to(
                jnp.sqrt(jnp.max(jnp.where(sel, n2, 0.0), axis=1, keepdims=True)), (1, LANES))

    q = q_ref[...] * (LOG2E / math.sqrt(half))
    qs = (jnp.where(lane < half, q, 0.0).astype(BF16), jnp.where(lane >= half, q, 0.0).astype(BF16))
    bounds = []
    for c in range(2):
        qs_ref[c * tq:(c + 1) * tq, :] = qs[c]
        qf = qs[c].astype(F32)
        qn2 = jnp.sum(qf * qf, axis=1, keepdims=True)
        qn = qn2 * lax.rsqrt(jnp.maximum(qn2, 1e-30))
        bounds.append(qn * kmax_ref[c:c + 1, :] * SCORE_BOUND_SLACK)
    fast = jnp.max(jnp.maximum(bounds[0], bounds[1])) <= MAX_FIXED_OFFSET
    for c in range(2):
        m_ref[c * tq:(c + 1) * tq, :] = jnp.where(fast, bounds[c], MASK_VALUE)
    l_ref[...] = jnp.zeros(l_ref.shape, F32)
    acc_ref[...] = jnp.zeros(acc_ref.shape, F32)

    def block(off, width, diag_at, fixed_offset):
        kblk = kb_ref[pl.ds(off, width), :]
        vblk = vb_ref[pl.ds(off, width), :]
        for c in range(2):
            rs = slice(c * tq, (c + 1) * tq)
            s = lax.dot_general(qs_ref[rs, :], kblk, _NT, preferred_element_type=F32)
            if diag_at is not None:
                row = lax.broadcasted_iota(jnp.int32, (tq, tq), 0)
                col = lax.broadcasted_iota(jnp.int32, (tq, tq), 1)
                tail = jnp.where(col <= row, s[:, diag_at:], MASK_VALUE)
                s = tail if diag_at == 0 else jnp.concatenate([s[:, :diag_at], tail], axis=1)
            m_prev = m_ref[rs, :]
            if fixed_offset:
                p = jnp.exp2(s - jnp.tile(m_prev, (1, width // LANES)))
                l_ref[rs, :] = l_ref[rs, :] + jnp.sum(p, axis=1, keepdims=True)
                acc_ref[rs, :] = acc_ref[rs, :] + jnp.dot(
                    p.astype(BF16), vblk, preferred_element_type=F32)
            else:
                m_next = jnp.maximum(m_prev, jnp.max(s, axis=1, keepdims=True))
                alpha = jnp.exp2(m_prev - m_next)
                p = jnp.exp2(s - jnp.tile(m_next, (1, width // LANES)))
                l_ref[rs, :] = alpha * l_ref[rs, :] + jnp.sum(p, axis=1, keepdims=True)
                acc_ref[rs, :] = alpha * acc_ref[rs, :] + jnp.dot(
                    p.astype(BF16), vblk, preferred_element_type=F32)
                m_ref[rs, :] = m_next

    per_wide = wide // tq
    nwide = qi // per_wide
    rem = qi - nwide * per_wide

    def all_blocks(fixed_offset):
        def wide_body(j, carry):
            block(pl.multiple_of(j * wide, wide), wide, None, fixed_offset)
            return carry

        lax.fori_loop(0, nwide, wide_body, 0)
        for r in range(per_wide):
            @pl.when(rem == r)
            def _(r=r):
                block(pl.multiple_of(nwide * wide, wide), (r + 1) * tq, r * tq, fixed_offset)

    pl.when(fast)(functools.partial(all_blocks, True))
    pl.when(jnp.logical_not(fast))(functools.partial(all_blocks, False))

    a = acc_ref[...] / l_ref[...]
    o = a[0:tq, :] - lam_ref[...] * a[tq:2 * tq, :]
    ms = jnp.mean(o * o, axis=-1, keepdims=True)
    o_ref[...] = (o * lax.rsqrt(ms + EPS) * g_ref[...] * out_scale * beta_ref[...]).astype(o_ref.dtype)


def _attn_prompt(proj, lam, gain, beta, weights, *, nheads, out_scale):
    t = proj.shape[0]
    tq = _pick(t, 512)
    wide = _pick(t, 2 * tq)
    nq = t // tq
    nsteps = nheads * nq
    assert tq % LANES == 0 and wide % tq == 0
    assert all(w.shape[0] % (16 * nsteps) == 0 for w in weights)
    cast_specs = [pl.BlockSpec((w.shape[0] // nsteps, w.shape[1]), lambda h, i: (h * nq + i, 0))
                  for w in weights]
    return pl.pallas_call(
        functools.partial(_attn_body, tq=tq, wide=wide, out_scale=out_scale, ncast=len(weights)),
        grid=(nheads, nq),
        in_specs=[
            pl.BlockSpec((1, LANES), lambda h, i: (0, 0)),
            pl.BlockSpec((tq, LANES), lambda h, i: (i, h)),
            pl.BlockSpec((t, LANES), lambda h, i: (0, nheads + h)),
            pl.BlockSpec((t, LANES), lambda h, i: (0, 2 * nheads + h)),
            pl.BlockSpec((1, LANES), lambda h, i: (0, 0)),
            pl.BlockSpec((1, LANES), lambda h, i: (0, h)),
        ] + cast_specs,
        out_specs=(pl.BlockSpec((tq, LANES), lambda h, i: (i, h)),
                   pl.BlockSpec((LANES, t), lambda h, i: (h, 0)),
                   pl.BlockSpec((t, LANES), lambda h, i: (0, h))) + tuple(cast_specs),
        out_shape=(jax.ShapeDtypeStruct((t, nheads * LANES), BF16),
                   jax.ShapeDtypeStruct((nheads * LANES, t), F32),
                   jax.ShapeDtypeStruct((t, nheads * LANES), F32))
        + tuple(jax.ShapeDtypeStruct(w.shape, BF16) for w in weights),
        scratch_shapes=[
            pltpu.VMEM((t, LANES), BF16),
            pltpu.VMEM((t, LANES), BF16),
            pltpu.VMEM((2 * tq, LANES), BF16),
            pltpu.VMEM((2 * tq, LANES), F32),
            pltpu.VMEM((2 * tq, LANES), F32),
            pltpu.VMEM((2 * tq, LANES), F32),
            pltpu.VMEM((2, LANES), F32),
        ],
        compiler_params=_params("parallel", "arbitrary"),
        name="attn_prompt",
    )(lam, proj, proj, proj, gain, beta, *weights)


def _attn_decode_body(pt_ref, lam_ref, g_ref, beta_ref, bmask_ref, q_ref, kn_ref, vn_ref, *rest,
                      npages, page, nheads, ts, out_scale):
    del pt_ref
    kp_refs = rest[:npages]
    vp_refs = rest[npages:2 * npages]
    o_ref, ks_ref, vs_ref, kn_scr = rest[2 * npages:]
    past = npages * page
    tail, w = kn_scr.shape
    nrow = bmask_ref.shape[0]
    half = LANES // 2

    for p in range(npages):
        ks_ref[:, p * page:(p + 1) * page] = kp_refs[p][...].astype(BF16)
        for h in range(nheads):
            vs_ref[p * page:(p + 1) * page, h * LANES:(h + 1) * LANES] = (
                vp_refs[p][pl.ds(h, page, stride=nheads), :].astype(BF16))
    kn_scr[...] = jnp.zeros((tail, w), BF16)
    kn_scr[0:ts, :] = kn_ref[0].astype(BF16)
    vs_ref[past:, :] = jnp.zeros((tail, w), BF16)
    vs_ref[past:past + ts, :] = vn_ref[0].astype(BF16)

    q = q_ref[0] * (LOG2E / math.sqrt(half))
    qt = (jnp.tile(q, (nrow // ts, 1)) * bmask_ref[...]).astype(BF16)
    s_past = jnp.dot(qt, ks_ref[...], preferred_element_type=F32)
    s_new = lax.dot_general(qt, kn_scr[...], _NT, preferred_element_type=F32)
    row = lax.broadcasted_iota(jnp.int32, s_new.shape, 0)
    col = lax.broadcasted_iota(jnp.int32, s_new.shape, 1)
    s_new = jnp.where(col <= lax.rem(row, ts), s_new, MASK_VALUE)
    m = jnp.maximum(jnp.max(s_past, axis=1, keepdims=True), jnp.max(s_new, axis=1, keepdims=True))
    p_past = jnp.exp2(s_past - m)
    p_new = jnp.exp2(s_new - m)
    l = jnp.sum(p_past, axis=1, keepdims=True) + jnp.sum(p_new, axis=1, keepdims=True)
    row1 = lax.broadcasted_iota(jnp.int32, (nrow, 1), 0)
    hr = nrow // 2
    wgt = jnp.where(row1 < hr, 1.0, -lam_ref[:, 0:1]) / l
    pw_past = p_past * wgt
    pw_new = p_new * wgt
    a_past = (pw_past[0:hr, :] + pw_past[hr:, :]).astype(BF16)
    a_new = (pw_new[0:hr, :] + pw_new[hr:, :]).astype(BF16)
    o2 = (jnp.dot(a_past, vs_ref[0:past, :], preferred_element_type=F32)
          + jnp.dot(a_new, vs_ref[past:, :], preferred_element_type=F32))
    for h in range(nheads):
        cs = slice(h * LANES, (h + 1) * LANES)
        o = o2[h * 8:(h + 1) * 8, cs]
        ms = jnp.mean(o * o, axis=-1, keepdims=True)
        o = o * lax.rsqrt(ms + EPS) * g_ref[...] * out_scale * beta_ref[:, cs]
        o_ref[0, :, cs] = o[0:ts, :].astype(o_ref.dtype)


def _hgrn_body(rq_ref, rf_ref, ri_ref, rg_ref, lb_ref, gn_ref, beta_ref, tri_ref, *rest,
               nheads, dk, chunk, batched):
    if batched:
        s0_ref, rec_ref, sout_ref, st_ref, o_scr = rest
        nseq = rq_ref.shape[0]
    else:
        rec_ref, st_ref, o_scr = rest
        nseq = 1

    lb = lb_ref[...]
    tri = tri_ref[...]
    tr = lax.broadcasted_iota(jnp.int32, (chunk, chunk), 0)
    tc = lax.broadcasted_iota(jnp.int32, (chunk, chunk), 1)
    causal = tc <= tr
    for bi in range(nseq):
        sb = bi * nheads
        if batched:
            rq, rf, ri, rg = rq_ref[bi], rf_ref[bi], ri_ref[bi], rg_ref[bi]
            for h in range(nheads):
                st_ref[sb + h] = s0_ref[bi, h].T
        else:
            rq, rf, ri, rg = rq_ref[...], rf_ref[...], ri_ref[...], rg_ref[...]
        rows = rq.shape[0]
        q = rq * jax.nn.sigmoid(rq)
        fg = lb + (1.0 - lb) * jax.nn.sigmoid(rf)
        logf = jnp.log(fg)
        kk = 1.0 - fg
        h1, h2, h3 = _split3(logf)
        b = (jnp.dot(tri, h1, preferred_element_type=F32)
             + jnp.dot(tri, h2, preferred_element_type=F32)
             + jnp.dot(tri, h3, preferred_element_type=F32))
        q_in = (q * jnp.exp(b)).astype(BF16)
        k_in = (kk * jnp.exp(-b)).astype(BF16)
        v_bf = ri.astype(BF16)
        for c in range(rows // chunk):
            rs = slice(c * chunk, (c + 1) * chunk)
            b_last = b[(c + 1) * chunk - 1:(c + 1) * chunk, :]
            k_st = (kk[rs, :] * jnp.exp(b_last - b[rs, :])).astype(BF16)
            decay = jnp.exp(b_last)
            for h in range(nheads):
                cs = slice(h * dk, (h + 1) * dk)
                qc, kc, vc = q_in[rs, cs], k_in[rs, cs], v_bf[rs, cs]
                att = lax.dot_general(qc, kc, _NT, preferred_element_type=F32)
                att = jnp.where(causal, att, 0.0).astype(BF16)
                st = st_ref[sb + h]
                o_scr[bi, rs, cs] = (
                    jnp.dot(att, vc, preferred_element_type=F32)
                    + lax.dot_general(qc, st.astype(BF16), _NT, preferred_element_type=F32))
                st_ref[sb + h] = st * decay[:, cs] + lax.dot_general(
                    vc, k_st[:, cs], _TA, preferred_element_type=F32)
        gate = rg * jax.nn.sigmoid(rg)
        for h in range(nheads):
            cs = slice(h * dk, (h + 1) * dk)
            o = o_scr[bi, :, cs]
            ms = jnp.mean(o * o, axis=-1, keepdims=True)
            o = (o * lax.rsqrt(ms + EPS) * gn_ref[...]) * gate[:, cs] * beta_ref[:, cs]
            if batched:
                rec_ref[bi, :, cs] = o.astype(rec_ref.dtype)
                sout_ref[bi, h] = st_ref[sb + h].T
            else:
                rec_ref[:, cs] = o.astype(rec_ref.dtype)


def _tri(rows, chunk):
    r = np.arange(rows)
    return jnp.asarray((r[:, None] >= r[None, :]) & (r[:, None] // chunk == r[None, :] // chunk), BF16)


def _hgrn_attn_body(pt_ref, *refs, n_rec_in, n_att_in, n_srec_in, rec_steps, att_steps,
                    rec_kw, att_kw, srec_kw):
    n_in = n_rec_in + n_att_in + n_srec_in
    rec_in, att_in = refs[:n_rec_in], refs[n_rec_in:n_rec_in + n_att_in]
    srec_rows, (srec_tri, srec_s0) = refs[n_in - n_srec_in:n_in - 2], refs[n_in - 2:n_in]
    rec_ref, sout_ref, att_ref, srec_ref, ssout_ref = refs[n_in:n_in + 5]
    st_ref, o_scr, ks_ref, vs_ref, kn_scr, sst_ref, so_scr = refs[n_in + 5:]
    nsteps = max(rec_steps, att_steps)
    step = pl.program_id(0)

    @pl.when(step == 0)
    def _():
        st_ref[...] = jnp.zeros(st_ref.shape, F32)

    def run_rec():
        _hgrn_body(*rec_in, rec_ref, st_ref, o_scr, **rec_kw)

    def run_att():
        _attn_decode_body(pt_ref, *att_in, att_ref, ks_ref, vs_ref, kn_scr, **att_kw)
        _hgrn_body(*srec_rows, *rec_in[4:7], srec_tri, srec_s0, srec_ref, ssout_ref, sst_ref, so_scr,
                   **srec_kw)

    if rec_steps < nsteps:
        pl.when(step < rec_steps)(run_rec)
    else:
        run_rec()
    if att_steps < nsteps:
        pl.when(step < att_steps)(run_att)
    else:
        run_att()

    @pl.when(step == rec_steps - 1)
    def _():
        for h in range(sout_ref.shape[0]):
            sout_ref[h] = st_ref[h].T


def _hgrn_attn_fused(proj, lb, rec_gain, rec_beta, page_table, lam, att_gain, att_beta, bmask,
                     proj3, cache_kt, cache_v2, s0, *, nrec, dk, dv, nheads, out_scale):
    t = proj.shape[0]
    gw = nrec * dk
    chunk = math.gcd(t, REC_CHUNK)
    nb, ts, _ = proj3.shape
    schunk = math.gcd(ts, REC_CHUNK)
    npages = page_table.shape[1]
    w, page = cache_kt.shape[1], cache_kt.shape[2]
    nrow = bmask.shape[0]
    tail = LANES
    assert dk == dv == LANES and schunk == ts
    assert 8 % ts == 0 and w == nheads * LANES and cache_v2.shape[1:] == (page * nheads, LANES)
    rows = chunk * _pick(t // chunk, max(1, min(4, (t // chunk) // nb)))
    rec_steps, att_steps = t // rows, nb
    nsteps = max(rec_steps, att_steps)

    def rstep(i):
        return jnp.minimum(i, rec_steps - 1) if rec_steps < nsteps else i

    def bstep(i):
        return jnp.minimum(i, att_steps - 1) if att_steps < nsteps else i

    def k_spec(p):
        return pl.BlockSpec((None, w, page), lambda i, pt: (pt[bstep(i), p], 0, 0))

    def v_spec(p):
        return pl.BlockSpec((None, page * nheads, LANES), lambda i, pt: (pt[bstep(i), p], 0, 0))

    rec_specs = [pl.BlockSpec((rows, gw), functools.partial(lambda i, pt, g: (rstep(i), g), g=3 + g))
                 for g in range(4)] + [
        pl.BlockSpec((1, gw), lambda i, pt: (0, 0)),
        pl.BlockSpec((1, dv), lambda i, pt: (0, 0)),
        pl.BlockSpec((1, gw), lambda i, pt: (0, 0)),
        pl.BlockSpec((rows, rows), lambda i, pt: (0, 0)),
    ]
    att_specs = [
        pl.BlockSpec((1, LANES), lambda i, pt: (0, 0)),
        pl.BlockSpec((1, LANES), lambda i, pt: (0, 0)),
        pl.BlockSpec((1, w), lambda i, pt: (0, 0)),
        pl.BlockSpec((nrow, w), lambda i, pt: (0, 0)),
        pl.BlockSpec((1, ts, w), lambda i, pt: (bstep(i), 0, 0)),
        pl.BlockSpec((1, ts, w), lambda i, pt: (bstep(i), 0, 1)),
        pl.BlockSpec((1, ts, w), lambda i, pt: (bstep(i), 0, 2)),
    ] + [k_spec(p) for p in range(npages)] + [v_spec(p) for p in range(npages)]
    srec_specs = [pl.BlockSpec((1, ts, gw), functools.partial(lambda i, pt, g: (bstep(i), 0, g), g=3 + g))
                  for g in range(4)] + [
        pl.BlockSpec((ts, ts), lambda i, pt: (0, 0)),
        pl.BlockSpec((1, nrec, dk, dv), lambda i, pt: (bstep(i), 0, 0, 0)),
    ]
    grid_spec = pltpu.PrefetchScalarGridSpec(
        num_scalar_prefetch=1,
        grid=(nsteps,),
        in_specs=rec_specs + att_specs + srec_specs,
        out_specs=(pl.BlockSpec((rows, gw), lambda i, pt: (rstep(i), 0)),
                   pl.BlockSpec((nrec, dk, dv), lambda i, pt: (0, 0, 0)),
                   pl.BlockSpec((1, ts, w), lambda i, pt: (bstep(i), 0, 0)),
                   pl.BlockSpec((1, ts, gw), lambda i, pt: (bstep(i), 0, 0)),
                   pl.BlockSpec((1, nrec, dk, dv), lambda i, pt: (bstep(i), 0, 0, 0))),
        scratch_shapes=[
            pltpu.VMEM((nrec, dv, dk), F32),
            pltpu.VMEM((1, rows, gw), F32),
            pltpu.VMEM((w, npages * page), BF16),
            pltpu.VMEM((npages * page + tail, w), BF16),
            pltpu.VMEM((tail, w), BF16),
            pltpu.VMEM((nrec, dv, dk), F32),
            pltpu.VMEM((1, ts, gw), F32),
        ],
    )
    return pl.pallas_call(
        functools.partial(
            _hgrn_attn_body, n_rec_in=len(rec_specs), n_att_in=len(att_specs),
            n_srec_in=len(srec_specs), rec_steps=rec_steps, att_steps=att_steps,
            rec_kw=dict(nheads=nrec, dk=dk, chunk=chunk, batched=False),
            att_kw=dict(npages=npages, page=page, nheads=nheads, ts=ts, out_scale=out_scale),
            srec_kw=dict(nheads=nrec, dk=dk, chunk=schunk, batched=True)),
        grid_spec=grid_spec,
        out_shape=(jax.ShapeDtypeStruct((t, gw), BF16),
                   jax.ShapeDtypeStruct((nrec, dk, dv), F32),
                   jax.ShapeDtypeStruct((nb, ts, w), BF16),
                   jax.ShapeDtypeStruct((nb, ts, gw), BF16),
                   jax.ShapeDtypeStruct((nb, nrec, dk, dv), F32)),
        compiler_params=_params("arbitrary"),
        name="hgrn_attn_fused",
    )(page_table, proj, proj, proj, proj, lb, rec_gain, rec_beta, _tri(rows, chunk),
      lam, att_gain, att_beta, bmask, proj3, proj3, proj3,
      *([cache_kt] * npages), *([cache_v2] * npages),
      proj3, proj3, proj3, proj3, _tri(ts, schunk), s0)


def _finish_body(x_ref, att_ref, rec_ref, wo_ref, g_ref, wu_ref, wd_ref, o_ref, hn_ref):
    j = pl.program_id(1)

    @pl.when(j == 0)
    def _():
        aw = att_ref.shape[-1]
        h = (_rows2d(x_ref[...])
             + jnp.dot(_rows2d(att_ref[...]), wo_ref[0:aw, :], preferred_element_type=F32)
             + jnp.dot(_rows2d(rec_ref[...]), wo_ref[aw:, :], preferred_element_type=F32))
        ms = jnp.mean(h * h, axis=-1, keepdims=True)
        hn_ref[...] = (h * lax.rsqrt(ms + EPS) * g_ref[...]).astype(BF16)
        o_ref[...] = h.reshape(o_ref.shape)

    u = jnp.maximum(jnp.dot(hn_ref[...], wu_ref[...], preferred_element_type=F32), 0.0)
    o_ref[...] += jnp.dot((u * u).astype(BF16), wd_ref[...],
                          preferred_element_type=F32).reshape(o_ref.shape)


def _finish(x, att, rec, wo_bf, ffn_g, wu_bf, wd_bf):
    d, aw, rw = x.shape[-1], att.shape[-1], rec.shape[-1]
    m = math.prod(x.shape[:-1])
    ff = wu_bf.shape[1]
    tm = _pick(m, 512)
    fc = _pick(ff, 1024)
    if x.ndim == 3:
        ts = x.shape[1]
        assert tm % ts == 0 and att.shape[:2] == rec.shape[:2] == x.shape[:2]
        row_spec = lambda n: pl.BlockSpec((tm // ts, ts, n), lambda i, j: (i, 0, 0))
    else:
        row_spec = lambda n: pl.BlockSpec((tm, n), lambda i, j: (i, 0))
    return pl.pallas_call(
        _finish_body,
        grid=(m // tm, ff // fc),
        in_specs=[
            row_spec(d),
            row_spec(aw),
            row_spec(rw),
            pl.BlockSpec((aw + rw, d), lambda i, j: (0, 0), pipeline_mode=pl.Buffered(1)),
            pl.BlockSpec((1, d), lambda i, j: (0, 0)),
            pl.BlockSpec((d, fc), lambda i, j: (0, j)),
            pl.BlockSpec((fc, d), lambda i, j: (j, 0)),
        ],
        out_specs=row_spec(d),
        out_shape=jax.ShapeDtypeStruct(x.shape, F32),
        scratch_shapes=[pltpu.VMEM((tm, d), BF16)],
        compiler_params=_params("parallel", "arbitrary"),
        name="finish",
    )(x, att, rec, wo_bf, ffn_g, wu_bf, wd_bf)


def _rope_tables(pos, head_dim):
    rot = head_dim // 4
    half = rot // 2
    inv = ROPE_THETA ** (-np.arange(half, dtype=np.float64) * 2.0 / rot)
    ang = np.asarray(pos, np.float64)[:, None] * inv[None, :]
    cos, sin = np.cos(ang), np.sin(ang)
    ones = np.ones((ang.shape[0], head_dim - rot))
    zeros = np.zeros((ang.shape[0], head_dim - rot))
    zh = np.zeros_like(sin)
    c = np.concatenate([cos, cos, ones], axis=1)
    sa = np.concatenate([-sin, zh, zeros], axis=1)
    sb = np.concatenate([zh, sin, zeros], axis=1)
    reps = LANES // head_dim
    return tuple(np.tile(a, (1, reps)).astype(np.float32) for a in (c, sa, sb))


def kernel(x_prompt, x_sample, cache_k, cache_v, state_rec, page_table, attn_norm, w_in, q_norm, k_norm,
           lambda_q1, lambda_k1, lambda_q2, lambda_k2, att_out_norm, rec_lb_logits, rec_out_norm,
           beta_att, beta_rec, w_out, ffn_norm, w_up, w_down):
    bp, tp, d = x_prompt.shape
    bs, ts, _ = x_sample.shape
    depth = w_in.shape[0]
    n_pool, page, nheads, _, head_dim = cache_k.shape[1:]
    vd = cache_v.shape[-1]
    nrec, dk, dv = state_rec.shape[2:]
    npages = page_table.shape[1]
    past = npages * page
    aw = nheads * vd
    gw = nrec * dk
    assert bp == 1 and 2 * head_dim == vd == LANES and aw == gw and w_in.shape[2] == 7 * gw

    lam_inits = tuple(0.8 - 0.6 * math.exp(-0.3 * l) for l in range(depth))
    lam_all, lb_all = _prep(lambda_q1, lambda_k1, lambda_q2, lambda_k2, rec_lb_logits, lam_inits)

    lane_map = np.arange(MXU_TILE) // head_dim
    gmat = jnp.asarray(lane_map[:, None] == lane_map[None, :], BF16)
    rope_p = _rope_tables(np.arange(tp), head_dim)
    rope_s = tuple(np.tile(a, (bs, 1)) for a in _rope_tables(past + np.arange(ts), head_dim))
    srow, kcol = np.arange(2 * nheads * 8)[:, None], np.arange(aw)[None, :]
    bmask = ((srow // (nheads * 8) == (kcol // head_dim) % 2)
             & ((srow // 8) % nheads == kcol // vd)).astype(np.float32)

    hp = x_prompt.reshape(tp, d)
    hs = x_sample
    outs = [[] for _ in range(6)]
    for l in range(depth):
        qk_gain = jnp.stack([jnp.tile(q_norm[l], aw // head_dim),
                             jnp.tile(k_norm[l], aw // head_dim)]).reshape(2, 1, aw)
        norm_g = attn_norm[l].reshape(1, d)
        lam = lam_all[l:l + 1]
        lb = lb_all[l:l + 1]
        att_g = att_out_norm[l].reshape(1, vd)
        rec_g = rec_out_norm[l].reshape(1, dv)
        b_att = beta_att[l].reshape(1, aw)
        b_rec = beta_rec[l].reshape(1, gw)
        ffn_g = ffn_norm[l].reshape(1, d)
        out_scale = 1.0 - lam_inits[l]

        proj3, w_in_bf = _proj(hs, norm_g, w_in[l], qk_gain, gmat, *rope_s, head_dim=head_dim)
        proj = _proj(hp, norm_g, w_in_bf, qk_gain, gmat, *rope_p, head_dim=head_dim)
        cache_kt = jnp.transpose(cache_k[l], (0, 2, 3, 4, 1)).reshape(n_pool, aw, page)
        cache_v2 = cache_v[l].reshape(n_pool, page * nheads, vd)
        rec, s_new, att_s, rec_s, s_new_s = _hgrn_attn_fused(
            proj, lb, rec_g, b_rec, page_table, lam, att_g, b_att, bmask, proj3, cache_kt, cache_v2,
            state_rec[l], nrec=nrec, dk=dk, dv=dv, nheads=nheads, out_scale=out_scale)

        att, k_t, v_p, w_out_bf, w_up_bf, w_down_bf = _attn_prompt(
            proj, lam, att_g, b_att, (w_out[l], w_up[l], w_down[l]), nheads=nheads, out_scale=out_scale)
        hp = _finish(hp, att, rec, w_out_bf, ffn_g, w_up_bf, w_down_bf)
        outs[0].append(jnp.transpose(k_t.reshape(nheads, 2, head_dim, tp), (3, 0, 1, 2))
                       .reshape(bp, tp, nheads, 2, head_dim))
        outs[1].append(v_p.reshape(bp, tp, nheads, vd))
        outs[2].append(s_new.reshape(bp, nrec, dk, dv).astype(state_rec.dtype))

        hs = _finish(hs, att_s, rec_s, w_out_bf, ffn_g, w_up_bf, w_down_bf)
        outs[3].append(proj3[:, :, aw:2 * aw].reshape(bs, ts, nheads, 2, head_dim))
        outs[4].append(proj3[:, :, 2 * aw:3 * aw].reshape(bs, ts, nheads, vd))
        outs[5].append(s_new_s.astype(state_rec.dtype))

    return (hp.reshape(bp, tp, d), hs,
            jnp.stack(outs[0]), jnp.stack(outs[1]), jnp.stack(outs[2]),
            jnp.stack(outs[3]), jnp.stack(outs[4]), jnp.stack(outs[5]))
```
